```python
import jax, jax.numpy as jnp
from jax import lax
import numpy as np

D_MODEL = 1024
BATCH = 4
SEQ = 8192
DEPTH = 2

N_MEM = 256
POOL_WINDOWS = (2, 4, 8, 16)
POOL_GROUPS = 4
POOL_GROUP_DIM = D_MODEL // 8
POOL_WIDTH = POOL_GROUPS * POOL_GROUP_DIM
RET_HEADS = 4
RET_DK = D_MODEL // 8
RET_DV = D_MODEL // 4
RET_QK_WIDTH = RET_HEADS * RET_DK
RET_V_WIDTH = RET_HEADS * RET_DV
RET_CHUNK = 128
ROPE_BASE = 10000.0
XATTN_HEADS = 4
XATTN_DH = D_MODEL // 8
XATTN_WIDTH = XATTN_HEADS * XATTN_DH
N_BRANCHES = 3
D_FF = 4 * D_MODEL
NORM_EPS = 1e-6
IN_SPLITS = (POOL_WIDTH, RET_QK_WIDTH, RET_QK_WIDTH, RET_V_WIDTH, RET_V_WIDTH, XATTN_WIDTH, N_BRANCHES * D_MODEL)
IN_COLS = sum(IN_SPLITS)
IN_OFFSETS = tuple(int(v) for v in np.cumsum(IN_SPLITS)[:-1])

kernel_name = "gated_pool_retention_memory_hybrid"


def rms_norm(x, g):
    xf = x.astype(jnp.float32)
    y = xf * lax.rsqrt(jnp.mean(xf * xf, axis=-1, keepdims=True) + NORM_EPS)
    return (y * g.astype(jnp.float32)).astype(x.dtype)


def pool_mixer(u, pool_w, pool_scale):
    B, S, _ = u.shape
    uf = u.astype(jnp.float32).reshape(B, S, POOL_GROUPS, POOL_GROUP_DIM)
    csum = jnp.cumsum(uf, axis=1)
    t = jnp.arange(S)
    outs = []
    for gi, w in enumerate(POOL_WINDOWS):
        cg = csum[:, :, gi]
        shifted = jnp.pad(cg, ((0, 0), (w, 0), (0, 0)))[:, :S]
        cnt = jnp.minimum(t + 1, w).astype(jnp.float32)[None, :, None]
        outs.append((cg - shifted) / cnt - uf[:, :, gi])
    d = jnp.stack(outs, axis=2).astype(u.dtype)
    y = jnp.einsum('bsgc,gcd->bsgd', d, pool_w).reshape(B, S, POOL_WIDTH)
    return y * pool_scale


def rotary(x, positions):
    half = x.shape[-1] // 2
    inv_freq = ROPE_BASE ** (-jnp.arange(half, dtype=jnp.float32) / half)
    ang = positions.astype(jnp.float32)[:, :, None] * inv_freq
    cos = jnp.cos(ang)[:, :, None, :]
    sin = jnp.sin(ang)[:, :, None, :]
    x1, x2 = x[..., :half], x[..., half:]
    return jnp.concatenate([x1 * cos - x2 * sin, x2 * cos + x1 * sin], axis=-1)


def retention(q, k, v, gate, positions, ret_norm_g):
    B, S, _ = q.shape
    dt = v.dtype
    N, C, H = S // RET_CHUNK, RET_CHUNK, RET_HEADS
    qf = rotary(q.astype(jnp.float32).reshape(B, S, H, RET_DK), positions)
    kf = rotary(k.astype(jnp.float32).reshape(B, S, H, RET_DK), positions) * (RET_DK ** -0.5)
    vf = v.astype(jnp.float32).reshape(B, S, H, RET_DV)
    qc = qf.reshape(B, N, C, H, RET_DK)
    kc = kf.reshape(B, N, C, H, RET_DK)
    vc = vf.reshape(B, N, C, H, RET_DV)

    log_gamma = jnp.log(1.0 - 2.0 ** (-5.0 - jnp.arange(H, dtype=jnp.float32)))
    idx = jnp.arange(C)
    diff = (idx[:, None] - idx[None, :]).astype(jnp.float32)
    causal = idx[:, None] >= idx[None, :]
    decay_mask = jnp.where(causal[None], jnp.exp(diff[None] * log_gamma[:, None, None]), 0.0)

    scores = jnp.einsum('bnihd,bnjhd->bnhij', qc, kc) * decay_mask[None, None]
    y_intra = jnp.einsum('bnhij,bnjhv->bnihv', scores, vc)

    idx_f = idx.astype(jnp.float32)
    k_dec = kc * jnp.exp((C - 1.0 - idx_f)[:, None] * log_gamma[None, :])[..., None]
    kv = jnp.einsum('bnjhd,bnjhv->bnhdv', k_dec, vc)
    chunk_decay = jnp.exp(C * log_gamma)[None, :, None, None]

    def step(state, kv_n):
        return chunk_decay * state + kv_n, state

    init = jnp.zeros((B, H, RET_DK, RET_DV), jnp.float32)
    _, s_prev = lax.scan(step, init, jnp.moveaxis(kv, 1, 0))
    s_prev = jnp.moveaxis(s_prev, 0, 1)
    q_dec = qc * jnp.exp((idx_f + 1.0)[:, None] * log_gamma[None, :])[..., None]
    y_inter = jnp.einsum('bnihd,bnhdv->bnihv', q_dec, s_prev)

    y = (y_intra + y_inter).reshape(B, S, H, RET_DV)
    y = y * lax.rsqrt(jnp.mean(y * y, axis=-1, keepdims=True) + NORM_EPS)
    y = y.reshape(B, S, RET_V_WIDTH) * ret_norm_g.astype(jnp.float32)
    return (jax.nn.silu(gate.astype(jnp.float32)) * y).astype(dt)


def memory_attention(q, mem_n, w_mem_kv):
    B, S, _ = q.shape
    M = mem_n.shape[1]
    kvm = mem_n @ w_mem_kv
    km = kvm[..., :XATTN_WIDTH].reshape(B, M, XATTN_HEADS, XATTN_DH)
    vm = kvm[..., XATTN_WIDTH:].reshape(B, M, XATTN_HEADS, XATTN_DH)
    qh = q.reshape(B, S, XATTN_HEADS, XATTN_DH)
    s = jnp.einsum('bshd,bmhd->bhsm', qh, km).astype(jnp.float32) * (XATTN_DH ** -0.5)
    p = jax.nn.softmax(s, axis=-1).astype(vm.dtype)
    return jnp.einsum('bhsm,bmhd->bshd', p, vm).reshape(B, S, XATTN_WIDTH)


def setup_inputs(seed: int = 0) -> dict:
    key = jax.random.key(seed)
    ks = jax.random.split(key, 20)
    f32 = jnp.float32

    def w(k, shape, fan_in):
        return jax.random.normal(k, shape, f32) * (fan_in ** -0.5)

    def gain(k, shape):
        return 1.0 + 0.02 * jax.random.normal(k, shape, f32)

    x = jax.random.normal(ks[0], (BATCH, SEQ, D_MODEL), f32)
    mem = jax.random.normal(ks[1], (BATCH, N_MEM, D_MODEL), f32)
    offsets = jax.random.randint(ks[2], (BATCH, 1), 0, 4096, dtype=jnp.int32)
    positions = (offsets + jnp.arange(SEQ, dtype=jnp.int32)[None, :]).astype(jnp.int32)
    return {
        "x": x,
        "mem": mem,
        "positions": positions,
        "norm_mix_g": gain(ks[3], (DEPTH, D_MODEL)),
        "w_in": w(ks[4], (DEPTH, D_MODEL, IN_COLS), D_MODEL),
        "pool_w": w(ks[5], (DEPTH, POOL_GROUPS, POOL_GROUP_DIM, POOL_GROUP_DIM), POOL_GROUP_DIM),
        "pool_scale": gain(ks[6], (DEPTH, POOL_WIDTH)),
        "ret_norm_g": gain(ks[7], (DEPTH, RET_V_WIDTH)),
        "mem_norm_g": gain(ks[8], (DEPTH, D_MODEL)),
        "w_mem_kv": w(ks[9], (DEPTH, D_MODEL, 2 * XATTN_WIDTH), D_MODEL),
        "w_up_pool": w(ks[10], (DEPTH, POOL_WIDTH, D_MODEL), POOL_WIDTH),
        "w_up_ret": w(ks[11], (DEPTH, RET_V_WIDTH, D_MODEL), RET_V_WIDTH),
        "w_up_mem": w(ks[12], (DEPTH, XATTN_WIDTH, D_MODEL), XATTN_WIDTH),
        "w_out": w(ks[13], (DEPTH, D_MODEL, D_MODEL), D_MODEL),
        "norm_mlp_g": gain(ks[14], (DEPTH, D_MODEL)),
        "w_mlp1": w(ks[15], (DEPTH, D_MODEL, D_FF), D_MODEL),
        "w_mlp2": w(ks[16], (DEPTH, D_FF, D_MODEL), D_FF),
        "final_norm_g": gain(ks[17], (D_MODEL,)),
    }


def reference(x, mem, positions, norm_mix_g, w_in, pool_w, pool_scale, ret_norm_g, mem_norm_g,
              w_mem_kv, w_up_pool, w_up_ret, w_up_mem, w_out, norm_mlp_g, w_mlp1, w_mlp2,
              final_norm_g):
    for l in range(DEPTH):
        h = rms_norm(x, norm_mix_g[l])
        proj = h @ w_in[l]
        pool_u, rq, rk, rv, rg, xq, gates = jnp.split(proj, IN_OFFSETS, axis=-1)
        y_pool = pool_mixer(pool_u, pool_w[l], pool_scale[l])
        y_ret = retention(rq, rk, rv, rg, positions, ret_norm_g[l])
        y_mem = memory_attention(xq, rms_norm(mem, mem_norm_g[l]), w_mem_kv[l])
        g_pool, g_ret, g_mem = jnp.split(jax.nn.sigmoid(gates), N_BRANCHES, axis=-1)
        merged = (g_pool * (y_pool @ w_up_pool[l])
                  + g_ret * (y_ret @ w_up_ret[l])
                  + g_mem * (y_mem @ w_up_mem[l]))
        x = x + merged @ w_out[l]
        h2 = rms_norm(x, norm_mlp_g[l])
        x = x + jnp.square(jax.nn.relu(h2 @ w_mlp1[l])) @ w_mlp2[l]
    return rms_norm(x, final_norm_g)
```

```python
import functools

import numpy as np
import jax
import jax.numpy as jnp
from jax import lax
from jax.experimental import pallas as pl
from jax.experimental.pallas import tpu as pltpu

D_MODEL = 1024
N_MEM = 256
POOL_WINDOWS = (2, 4, 8, 16)
POOL_GROUPS = 4
POOL_GROUP_DIM = 128
POOL_WIDTH = POOL_GROUPS * POOL_GROUP_DIM
POOL_HALO = 16
RET_HEADS = 4
RET_DK = 128
RET_DV = 256
RET_QK_WIDTH = RET_HEADS * RET_DK
RET_V_WIDTH = RET_HEADS * RET_DV
RET_CHUNK = 128
ROPE_BASE = 10000.0
XATTN_HEADS = 4
XATTN_DH = 128
XATTN_WIDTH = XATTN_HEADS * XATTN_DH
D_FF = 4 * D_MODEL
NORM_EPS = 1e-6

OFF_POOL = 0
OFF_RQ = OFF_POOL + POOL_WIDTH
OFF_RK = OFF_RQ + RET_QK_WIDTH
OFF_RV = OFF_RK + RET_QK_WIDTH
OFF_RG = OFF_RV + RET_V_WIDTH
OFF_XQ = OFF_RG + RET_V_WIDTH
OFF_GATE = OFF_XQ + XATTN_WIDTH
IN_COLS = OFF_GATE + 3 * D_MODEL

TOKEN_TILE = 512
ROPE_TILE = 1024
N_CHUNK = 512
VMEM_LIMIT_BYTES = 56 * 1024 * 1024

BF16 = jnp.bfloat16
F32 = jnp.float32


def _dot(a, b):
    return jnp.dot(a, b, preferred_element_type=F32)


def _rms_norm(x, g):
    return x * lax.rsqrt(jnp.mean(x * x, axis=-1, keepdims=True) + NORM_EPS) * g


def _const_spec(shape):
    nd = len(shape)
    return pl.BlockSpec(shape, lambda *_: (0,) * nd, pipeline_mode=pl.Buffered(1))


def _rope_kernel(pos_ref, freq_ref, cos_ref, sin_ref):
    ang = pos_ref[...] * freq_ref[...]
    lane = lax.broadcasted_iota(jnp.int32, ang.shape, 1)
    cos_ref[...] = jnp.cos(ang)
    s = jnp.sin(ang)
    sin_ref[...] = jnp.where(lane < RET_DK // 2, -s, s)


def _rope_tables(positions):
    t = positions.size
    half = RET_DK // 2
    inv_freq = ROPE_BASE ** (-jnp.arange(half, dtype=F32) / half)
    freq2 = jnp.concatenate([inv_freq, inv_freq]).reshape(1, RET_DK)
    pos = positions.astype(F32).reshape(t, 1)
    tile = min(ROPE_TILE, t)
    return pl.pallas_call(
        _rope_kernel,
        grid=(t // tile,),
        in_specs=[pl.BlockSpec((tile, 1), lambda i: (i, 0)),
                  pl.BlockSpec((1, RET_DK), lambda i: (0, 0))],
        out_specs=[pl.BlockSpec((tile, RET_DK), lambda i: (i, 0))] * 2,
        out_shape=[jax.ShapeDtypeStruct((t, RET_DK), F32)] * 2,
        name="rope_tables",
    )(pos, freq2)


def _mem_kv_kernel(mem_ref, g_ref, w_ref, kt_ref, v_ref):
    mem_n = _rms_norm(mem_ref[...], g_ref[...]).astype(BF16)
    kv = _dot(mem_n, w_ref[...])
    kt_ref[...] = kv[:, :XATTN_WIDTH].T.astype(BF16)
    v_ref[...] = kv[:, XATTN_WIDTH:].astype(BF16)


def _mem_kv(mem, mem_norm_g, w_mem_kv):
    depth = w_mem_kv.shape[0]
    b, m, d = mem.shape
    return pl.pallas_call(
        _mem_kv_kernel,
        grid=(depth, b),
        in_specs=[pl.BlockSpec((None, m, d), lambda l, i: (i, 0, 0)),
                  pl.BlockSpec((None, 1, d), lambda l, i: (l, 0, 0)),
                  pl.BlockSpec((None, d, 2 * XATTN_WIDTH), lambda l, i: (l, 0, 0))],
        out_specs=[pl.BlockSpec((None, None, XATTN_WIDTH, m), lambda l, i: (l, i, 0, 0)),
                   pl.BlockSpec((None, None, m, XATTN_WIDTH), lambda l, i: (l, i, 0, 0))],
        out_shape=[jax.ShapeDtypeStruct((depth, b, XATTN_WIDTH, m), BF16),
                   jax.ShapeDtypeStruct((depth, b, m, XATTN_WIDTH), BF16)],
        name="mem_kv",
    )(mem, mem_norm_g.reshape(depth, 1, d), w_mem_kv.astype(BF16))


def _retention_tables():
    c = RET_CHUNK
    log_gamma = np.log(1.0 - 2.0 ** (-5.0 - np.arange(RET_HEADS, dtype=np.float64)))
    idx = np.arange(c, dtype=np.float64)
    diff = idx[:, None] - idx[None, :]
    mask = np.where(diff[None] >= 0, np.exp(diff[None] * log_gamma[:, None, None]), 0.0)
    q_dec = np.exp((idx + 1.0)[None, :] * log_gamma[:, None])
    k_dec = np.exp((c - 1.0 - idx)[None, :] * log_gamma[:, None])
    q_dec = np.broadcast_to(q_dec[:, :, None], (RET_HEADS, c, RET_DK))
    k_dec = np.broadcast_to(k_dec[:, :, None], (RET_HEADS, c, RET_DK))
    chunk_decay = tuple(float(v) for v in np.exp(c * log_gamma))
    as_f32 = lambda a: jnp.asarray(np.ascontiguousarray(a), dtype=F32)
    return as_f32(mask), as_f32(q_dec), as_f32(k_dec), chunk_decay


def _mixer_kernel(chunk_decay,
                  x_ref, cos_ref, sin_ref, kt_ref, vm_ref, mask_ref, qdec_ref, kdec_ref,
                  gmix_ref, w_in_ref, pool_w_ref, pool_scale_ref, ret_g_ref,
                  w_up_pool_ref, w_up_ret_ref, w_up_mem_ref, w_out_ref,
                  o_ref,
                  ubuf_ref, state_ref, ypool_ref, yret_ref, ymem_ref, merged_ref):
    tm = x_ref.shape[0]
    seq_step = pl.program_id(1)

    @pl.when(seq_step == 0)
    def _():
        ubuf_ref[0:POOL_HALO, :] = jnp.zeros((POOL_HALO, POOL_WIDTH), F32)
        state_ref[...] = jnp.zeros_like(state_ref)

    x = x_ref[...]
    h = _rms_norm(x, gmix_ref[...]).astype(BF16)

    def proj(off, width):
        return _dot(h, w_in_ref[:, off:off + width])

    u = proj(OFF_POOL, POOL_WIDTH)
    ubuf_ref[POOL_HALO:POOL_HALO + tm, :] = u
    row = lax.broadcasted_iota(jnp.int32, (tm, 1), 0) + seq_step * tm
    for gi, w in enumerate(POOL_WINDOWS):
        cols = slice(gi * POOL_GROUP_DIM, (gi + 1) * POOL_GROUP_DIM)
        win = ubuf_ref[:, cols]
        shift = 1
        while shift < w:
            win = win + pltpu.roll(win, shift, 0)
            shift *= 2
        cnt = jnp.minimum(row + 1, w).astype(F32)
        dwin = win[POOL_HALO:, :] / cnt - u[:, cols]
        yg = _dot(dwin.astype(BF16), pool_w_ref[gi]) * pool_scale_ref[:, cols]
        ypool_ref[:, cols] = yg.astype(BF16)
    ubuf_ref[0:POOL_HALO, :] = ubuf_ref[tm:tm + POOL_HALO, :]

    cos2 = cos_ref[...]
    sin2 = sin_ref[...]
    q_all = proj(OFF_RQ, RET_QK_WIDTH)
    k_all = proj(OFF_RK, RET_QK_WIDTH)
    for hh in range(RET_HEADS):
        qk = slice(hh * RET_DK, (hh + 1) * RET_DK)
        vv = slice(hh * RET_DV, (hh + 1) * RET_DV)
        qh = q_all[:, qk]
        kh = k_all[:, qk]
        qr = qh * cos2 + pltpu.roll(qh, RET_DK // 2, 1) * sin2
        kr = (kh * cos2 + pltpu.roll(kh, RET_DK // 2, 1) * sin2) * (RET_DK ** -0.5)
        v_h = proj(OFF_RV + hh * RET_DV, RET_DV).astype(BF16)
        g_h = proj(OFF_RG + hh * RET_DV, RET_DV)
        for c in range(tm // RET_CHUNK):
            rows = slice(c * RET_CHUNK, (c + 1) * RET_CHUNK)
            qc = qr[rows]
            kc = kr[rows]
            vc = v_h[rows]
            scores = lax.dot_general(qc.astype(BF16), kc.astype(BF16),
                                     (((1,), (1,)), ((), ())),
                                     preferred_element_type=F32) * mask_ref[hh]
            q_dec = qc * qdec_ref[hh]
            k_dec = kc * kdec_ref[hh]
            s_prev = state_ref[hh]
            lhs = jnp.concatenate([scores.astype(BF16), q_dec.astype(BF16)], axis=1)
            rhs = jnp.concatenate([vc, s_prev.astype(BF16)], axis=0)
            y = _dot(lhs, rhs)
            kv = lax.dot_general(k_dec.astype(BF16), vc, (((0,), (0,)), ((), ())),
                                 preferred_element_type=F32)
            state_ref[hh] = chunk_decay[hh] * s_prev + kv
            y = y * lax.rsqrt(jnp.mean(y * y, axis=-1, keepdims=True) + NORM_EPS)
            y = y * ret_g_ref[:, vv]
            g = g_h[rows]
            yret_ref[rows, vv] = (g * jax.nn.sigmoid(g) * y).astype(BF16)

    xq = proj(OFF_XQ, XATTN_WIDTH)
    for hh in range(XATTN_HEADS):
        hd = slice(hh * XATTN_DH, (hh + 1) * XATTN_DH)
        s = _dot(xq[:, hd].astype(BF16), kt_ref[hd, :]) * (XATTN_DH ** -0.5)
        s = s - jnp.max(s, axis=-1, keepdims=True)
        p = jnp.exp(s)
        p = p / jnp.sum(p, axis=-1, keepdims=True)
        ymem_ref[:, hd] = _dot(p.astype(BF16), vm_ref[:, hd]).astype(BF16)

    ypool = ypool_ref[...]
    yret = yret_ref[...]
    ymem = ymem_ref[...]
    for j in range(D_MODEL // N_CHUNK):
        nc = slice(j * N_CHUNK, (j + 1) * N_CHUNK)

        def gate(b):
            return jax.nn.sigmoid(proj(OFF_GATE + b * D_MODEL + j * N_CHUNK, N_CHUNK))

        m = gate(0) * _dot(ypool, w_up_pool_ref[:, nc])
        m = m + gate(1) * _dot(yret, w_up_ret_ref[:, nc])
        m = m + gate(2) * _dot(ymem, w_up_mem_ref[:, nc])
        merged_ref[:, nc] = m.astype(BF16)
    o_ref[...] = x + _dot(merged_ref[...], w_out_ref[...])


def _mixer_layer(x2d, batch, cos2, sin2, kt, vm, tables, gmix, w_in, pool_w, pool_scale, ret_g,
                 w_up_pool, w_up_ret, w_up_mem, w_out):
    t, d = x2d.shape
    seq = t // batch
    tm = min(TOKEN_TILE, seq)
    steps = seq // tm
    mask, q_dec, k_dec, chunk_decay = tables
    tok = lambda b, s: (b * steps + s, 0)
    in_specs = [
        pl.BlockSpec((tm, d), tok),
        pl.BlockSpec((tm, RET_DK), tok),
        pl.BlockSpec((tm, RET_DK), tok),
        pl.BlockSpec((None, XATTN_WIDTH, N_MEM), lambda b, s: (b, 0, 0)),
        pl.BlockSpec((None, N_MEM, XATTN_WIDTH), lambda b, s: (b, 0, 0)),
        _const_spec(mask.shape), _const_spec(q_dec.shape), _const_spec(k_dec.shape),
        _const_spec(gmix.shape), _const_spec(w_in.shape), _const_spec(pool_w.shape),
        _const_spec(pool_scale.shape), _const_spec(ret_g.shape),
        _const_spec(w_up_pool.shape), _const_spec(w_up_ret.shape), _const_spec(w_up_mem.shape),
        _const_spec(w_out.shape),
    ]
    return pl.pallas_call(
        functools.partial(_mixer_kernel, chunk_decay),
        grid=(batch, steps),
        in_specs=in_specs,
        out_specs=pl.BlockSpec((tm, d), tok),
        out_shape=jax.ShapeDtypeStruct((t, d), F32),
        scratch_shapes=[
            pltpu.VMEM((POOL_HALO + tm, POOL_WIDTH), F32),
            pltpu.VMEM((RET_HEADS, RET_DK, RET_DV), F32),
            pltpu.VMEM((tm, POOL_WIDTH), BF16),
            pltpu.VMEM((tm, RET_V_WIDTH), BF16),
            pltpu.VMEM((tm, XATTN_WIDTH), BF16),
            pltpu.VMEM((tm, D_MODEL), BF16),
        ],
        compiler_params=pltpu.CompilerParams(
            dimension_semantics=("arbitrary", "arbitrary"),
            vmem_limit_bytes=VMEM_LIMIT_BYTES),
        name="mixer",
    )(x2d, cos2, sin2, kt, vm, mask, q_dec, k_dec, gmix, w_in, pool_w, pool_scale, ret_g,
      w_up_pool, w_up_ret, w_up_mem, w_out)


def _mlp_kernel(apply_final_norm, x_ref, g_ref, w1_ref, w2_ref, gf_ref, o_ref, hid_ref):
    x = x_ref[...]
    h = _rms_norm(x, g_ref[...]).astype(BF16)
    for j in range(D_FF // N_CHUNK):
        nc = slice(j * N_CHUNK, (j + 1) * N_CHUNK)
        a = jnp.maximum(_dot(h, w1_ref[:, nc]), 0.0)
        hid_ref[:, nc] = (a * a).astype(BF16)
    y = x + _dot(hid_ref[...], w2_ref[...])
    if apply_final_norm:
        y = _rms_norm(y, gf_ref[...])
    o_ref[...] = y


def _mlp_layer(x2d, g, w1, w2, g_final, apply_final_norm):
    t, d = x2d.shape
    tm = min(TOKEN_TILE, t)
    return pl.pallas_call(
        functools.partial(_mlp_kernel, apply_final_norm),
        grid=(t // tm,),
        in_specs=[pl.BlockSpec((tm, d), lambda i: (i, 0)),
                  _const_spec(g.shape), _const_spec(w1.shape), _const_spec(w2.shape),
                  _const_spec(g_final.shape)],
        out_specs=pl.BlockSpec((tm, d), lambda i: (i, 0)),
        out_shape=jax.ShapeDtypeStruct((t, d), F32),
        scratch_shapes=[pltpu.VMEM((tm, D_FF), BF16)],
        compiler_params=pltpu.CompilerParams(
            dimension_semantics=("arbitrary",),
            vmem_limit_bytes=VMEM_LIMIT_BYTES),
        name="mlp",
    )(x2d, g, w1, w2, g_final)


def kernel(x, mem, positions, norm_mix_g, w_in, pool_w, pool_scale, ret_norm_g, mem_norm_g,
           w_mem_kv, w_up_pool, w_up_ret, w_up_mem, w_out, norm_mlp_g, w_mlp1, w_mlp2,
           final_norm_g):
    batch, seq, d = x.shape
    depth = w_in.shape[0]
    assert d == D_MODEL and w_in.shape[-1] == IN_COLS and mem.shape[1] == N_MEM
    assert seq % RET_CHUNK == 0 and seq % min(TOKEN_TILE, seq) == 0

    cos2, sin2 = _rope_tables(positions)
    kt, vm = _mem_kv(mem, mem_norm_g, w_mem_kv)
    tables = _retention_tables()
    row = lambda a: a.reshape(1, -1)

    x2d = x.reshape(batch * seq, d)
    for l in range(depth):
        x2d = _mixer_layer(
            x2d, batch, cos2, sin2, kt[l], vm[l], tables, row(norm_mix_g[l]),
            w_in[l].astype(BF16), pool_w[l].astype(BF16), row(pool_scale[l]), row(ret_norm_g[l]),
            w_up_pool[l].astype(BF16), w_up_ret[l].astype(BF16), w_up_mem[l].astype(BF16),
            w_out[l].astype(BF16))
        x2d = _mlp_layer(x2d, row(norm_mlp_g[l]), w_mlp1[l].astype(BF16), w_mlp2[l].astype(BF16),
                         row(final_norm_g), apply_final_norm=(l == depth - 1))
    return x2d.reshape(batch, seq, d)
```

```python
import functools

import numpy as np
import jax
import jax.numpy as jnp
from jax import lax
from jax.experimental import pallas as pl
from jax.experimental.pallas import tpu as pltpu

D_MODEL = 1024
N_MEM = 256
POOL_WINDOWS = (2, 4, 8, 16)
POOL_GROUPS = 4
POOL_GROUP_DIM = 128
POOL_WIDTH = POOL_GROUPS * POOL_GROUP_DIM
POOL_HALO = 16
RET_HEADS = 4
RET_DK = 128
RET_DV = 256
RET_QK_WIDTH = RET_HEADS * RET_DK
RET_V_WIDTH = RET_HEADS * RET_DV
RET_CHUNK = 128
ROPE_BASE = 10000.0
XATTN_HEADS = 4
XATTN_DH = 128
XATTN_WIDTH = XATTN_HEADS * XATTN_DH
D_FF = 4 * D_MODEL
NORM_EPS = 1e-6

OFF_POOL = 0
OFF_RQ = OFF_POOL + POOL_WIDTH
OFF_RK = OFF_RQ + RET_QK_WIDTH
OFF_RV = OFF_RK + RET_QK_WIDTH
OFF_RG = OFF_RV + RET_V_WIDTH
OFF_XQ = OFF_RG + RET_V_WIDTH
OFF_GATE = OFF_XQ + XATTN_WIDTH
IN_COLS = OFF_GATE + 3 * D_MODEL

TOKEN_TILE = 512
ROPE_TILE = 1024
N_CHUNK = 512
VMEM_LIMIT_BYTES = 56 * 1024 * 1024

BF16 = jnp.bfloat16
F32 = jnp.float32


def _dot(a, b):
    return jnp.dot(a, b, preferred_element_type=F32)


def _rms_norm(x, g):
    return x * lax.rsqrt(jnp.mean(x * x, axis=-1, keepdims=True) + NORM_EPS) * g


def _const_spec(shape):
    nd = len(shape)
    return pl.BlockSpec(shape, lambda *_: (0,) * nd, pipeline_mode=pl.Buffered(1))


def _rope_kernel(pos_ref, freq_ref, cos_ref, sin_ref):
    ang = pos_ref[...] * freq_ref[...]
    lane = lax.broadcasted_iota(jnp.int32, ang.shape, 1)
    cos_ref[...] = jnp.cos(ang)
    s = jnp.sin(ang)
    sin_ref[...] = jnp.where(lane < RET_DK // 2, -s, s)


def _rope_tables(positions):
    t = positions.size
    half = RET_DK // 2
    inv_freq = ROPE_BASE ** (-jnp.arange(half, dtype=F32) / half)
    freq2 = jnp.concatenate([inv_freq, inv_freq]).reshape(1, RET_DK)
    pos = positions.astype(F32).reshape(t, 1)
    tile = min(ROPE_TILE, t)
    return pl.pallas_call(
        _rope_kernel,
        grid=(t // tile,),
        in_specs=[pl.BlockSpec((tile, 1), lambda i: (i, 0)),
                  pl.BlockSpec((1, RET_DK), lambda i: (0, 0))],
        out_specs=[pl.BlockSpec((tile, RET_DK), lambda i: (i, 0))] * 2,
        out_shape=[jax.ShapeDtypeStruct((t, RET_DK), F32)] * 2,
        name="rope_tables",
    )(pos, freq2)


def _mem_kv_kernel(mem_ref, g_ref, w_ref, kt_ref, v_ref):
    mem_n = _rms_norm(mem_ref[...], g_ref[...]).astype(BF16)
    kv = _dot(mem_n, w_ref[...])
    kt_ref[...] = kv[:, :XATTN_WIDTH].T.astype(BF16)
    v_ref[...] = kv[:, XATTN_WIDTH:].astype(BF16)


def _mem_kv(mem, mem_norm_g, w_mem_kv):
    depth = w_mem_kv.shape[0]
    b, m, d = mem.shape
    return pl.pallas_call(
        _mem_kv_kernel,
        grid=(depth, b),
        in_specs=[pl.BlockSpec((None, m, d), lambda l, i: (i, 0, 0)),
                  pl.BlockSpec((None, 1, d), lambda l, i: (l, 0, 0)),
                  pl.BlockSpec((None, d, 2 * XATTN_WIDTH), lambda l, i: (l, 0, 0))],
        out_specs=[pl.BlockSpec((None, None, XATTN_WIDTH, m), lambda l, i: (l, i, 0, 0)),
                   pl.BlockSpec((None, None, m, XATTN_WIDTH), lambda l, i: (l, i, 0, 0))],
        out_shape=[jax.ShapeDtypeStruct((depth, b, XATTN_WIDTH, m), BF16),
                   jax.ShapeDtypeStruct((depth, b, m, XATTN_WIDTH), BF16)],
        name="mem_kv",
    )(mem, mem_norm_g.reshape(depth, 1, d), w_mem_kv.astype(BF16))


def _retention_tables():
    c = RET_CHUNK
    log_gamma = np.log(1.0 - 2.0 ** (-5.0 - np.arange(RET_HEADS, dtype=np.float64)))
    idx = np.arange(c, dtype=np.float64)
    diff = idx[:, None] - idx[None, :]
    mask = np.where(diff[None] >= 0, np.exp(diff[None] * log_gamma[:, None, None]), 0.0)
    q_dec = np.exp((idx + 1.0)[None, :] * log_gamma[:, None])
    k_dec = np.exp((c - 1.0 - idx)[None, :] * log_gamma[:, None])
    q_dec = np.broadcast_to(q_dec[:, :, None], (RET_HEADS, c, RET_DK))
    k_dec = np.broadcast_to(k_dec[:, :, None], (RET_HEADS, c, RET_DK))
    chunk_decay = tuple(float(v) for v in np.exp(c * log_gamma))
    as_f32 = lambda a: jnp.asarray(np.ascontiguousarray(a), dtype=F32)
    return as_f32(mask), as_f32(q_dec), as_f32(k_dec), chunk_decay


def _mixer_kernel(chunk_decay,
                  x_ref, cos_ref, sin_ref, kt_ref, vm_ref, mask_ref, qdec_ref, kdec_ref,
                  gmix_ref, w_in_ref, pool_w_ref, pool_scale_ref, ret_g_ref,
                  w_up_pool_ref, w_up_ret_ref, w_up_mem_ref, w_out_ref,
                  o_ref,
                  h_ref, ubuf_ref, state_ref, qb_ref, kb_ref, kd_ref, lhs_ref, rhs_ref, gsilu_ref,
                  ypool_ref, yret_ref, ymem_ref, merged_ref):
    tm = x_ref.shape[0]
    n_chunks = tm // RET_CHUNK
    seq_step = pl.program_id(1)

    @pl.when(seq_step == 0)
    def _():
        ubuf_ref[0:POOL_HALO, :] = jnp.zeros((POOL_HALO, POOL_WIDTH), F32)
        state_ref[...] = jnp.zeros_like(state_ref)

    h_ref[...] = _rms_norm(x_ref[...], gmix_ref[...]).astype(BF16)

    def proj(off, width):
        return _dot(h_ref[...], w_in_ref[:, off:off + width])

    def chunk_rows(c):
        return slice(c * RET_CHUNK, (c + 1) * RET_CHUNK)

    cos2 = cos_ref[...]
    sin2 = sin_ref[...]
    q_all = proj(OFF_RQ, RET_QK_WIDTH)
    k_all = proj(OFF_RK, RET_QK_WIDTH)
    for hh in range(RET_HEADS):
        qk = slice(hh * RET_DK, (hh + 1) * RET_DK)
        qh = q_all[:, qk]
        kh = k_all[:, qk]
        qr = qh * cos2 + pltpu.roll(qh, RET_DK // 2, 1) * sin2
        kr = (kh * cos2 + pltpu.roll(kh, RET_DK // 2, 1) * sin2) * (RET_DK ** -0.5)
        qb_ref[:, qk] = qr.astype(BF16)
        kb_ref[:, qk] = kr.astype(BF16)
        for c in range(n_chunks):
            rows = chunk_rows(c)
            lhs_ref[hh, rows, RET_CHUNK:] = (qr[rows] * qdec_ref[hh]).astype(BF16)
            kd_ref[rows, qk] = (kr[rows] * kdec_ref[hh]).astype(BF16)

    u = proj(OFF_POOL, POOL_WIDTH)
    ubuf_ref[POOL_HALO:POOL_HALO + tm, :] = u
    v_all = proj(OFF_RV, RET_V_WIDTH)
    for hh in range(RET_HEADS):
        for c in range(n_chunks):
            rhs_ref[hh * n_chunks + c, 0:RET_CHUNK, :] = (
                v_all[chunk_rows(c), hh * RET_DV:(hh + 1) * RET_DV].astype(BF16))
    row = lax.broadcasted_iota(jnp.int32, (tm, 1), 0) + seq_step * tm
    for gi, w in enumerate(POOL_WINDOWS):
        cols = slice(gi * POOL_GROUP_DIM, (gi + 1) * POOL_GROUP_DIM)
        win = ubuf_ref[:, cols]
        shift = 1
        while shift < w:
            win = win + pltpu.roll(win, shift, 0)
            shift *= 2
        cnt = jnp.minimum(row + 1, w).astype(F32)
        dwin = win[POOL_HALO:, :] / cnt - ubuf_ref[POOL_HALO:POOL_HALO + tm, cols]
        yg = _dot(dwin.astype(BF16), pool_w_ref[gi]) * pool_scale_ref[:, cols]
        ypool_ref[:, cols] = yg.astype(BF16)
    ubuf_ref[0:POOL_HALO, :] = ubuf_ref[tm:tm + POOL_HALO, :]

    for hh in range(RET_HEADS):
        qk = slice(hh * RET_DK, (hh + 1) * RET_DK)
        for c in range(n_chunks):
            rows = chunk_rows(c)
            scores = lax.dot_general(qb_ref[rows, qk], kb_ref[rows, qk],
                                     (((1,), (1,)), ((), ())),
                                     preferred_element_type=F32) * mask_ref[hh]
            lhs_ref[hh, rows, 0:RET_CHUNK] = scores.astype(BF16)

    g_all = proj(OFF_RG, RET_V_WIDTH)
    gsilu_ref[...] = g_all * jax.nn.sigmoid(g_all)

    for hh in range(RET_HEADS):
        qk = slice(hh * RET_DK, (hh + 1) * RET_DK)
        kvs = [lax.dot_general(kd_ref[chunk_rows(c), qk], rhs_ref[hh * n_chunks + c, 0:RET_CHUNK, :],
                               (((0,), (0,)), ((), ())), preferred_element_type=F32)
               for c in range(n_chunks)]
        s = state_ref[hh]
        for c in range(n_chunks):
            rhs_ref[hh * n_chunks + c, RET_CHUNK:, :] = s.astype(BF16)
            s = chunk_decay[hh] * s + kvs[c]
        state_ref[hh] = s

    xq = proj(OFF_XQ, XATTN_WIDTH).astype(BF16)

    for hh in range(RET_HEADS):
        vv = slice(hh * RET_DV, (hh + 1) * RET_DV)
        for c in range(n_chunks):
            rows = chunk_rows(c)
            y = _dot(lhs_ref[hh, rows, :], rhs_ref[hh * n_chunks + c])
            y = y * lax.rsqrt(jnp.mean(y * y, axis=-1, keepdims=True) + NORM_EPS)
            y = y * ret_g_ref[:, vv]
            yret_ref[rows, vv] = (gsilu_ref[rows, vv] * y).astype(BF16)

    for hh in range(XATTN_HEADS):
        hd = slice(hh * XATTN_DH, (hh + 1) * XATTN_DH)
        s = _dot(xq[:, hd], kt_ref[hd, :]) * (XATTN_DH ** -0.5)
        s = s - jnp.max(s, axis=-1, keepdims=True)
        p = jnp.exp(s)
        p = p / jnp.sum(p, axis=-1, keepdims=True)
        ymem_ref[:, hd] = _dot(p.astype(BF16), vm_ref[:, hd]).astype(BF16)

    for j in range(D_MODEL // N_CHUNK):
        nc = slice(j * N_CHUNK, (j + 1) * N_CHUNK)

        def gate(b):
            return jax.nn.sigmoid(proj(OFF_GATE + b * D_MODEL + j * N_CHUNK, N_CHUNK))

        m = gate(0) * _dot(ypool_ref[...], w_up_pool_ref[:, nc])
        m = m + gate(1) * _dot(yret_ref[...], w_up_ret_ref[:, nc])
        m = m + gate(2) * _dot(ymem_ref[...], w_up_mem_ref[:, nc])
        merged_ref[:, nc] = m.astype(BF16)
    o_ref[...] = x_ref[...] + _dot(merged_ref[...], w_out_ref[...])


def _mixer_layer(x2d, batch, cos2, sin2, kt, vm, tables, gmix, w_in, pool_w, pool_scale, ret_g,
                 w_up_pool, w_up_ret, w_up_mem, w_out):
    t, d = x2d.shape
    seq = t // batch
    tm = min(TOKEN_TILE, seq)
    steps = seq // tm
    mask, q_dec, k_dec, chunk_decay = tables
    tok = lambda b, s: (b * steps + s, 0)
    in_specs = [
        pl.BlockSpec((tm, d), tok),
        pl.BlockSpec((tm, RET_DK), tok),
        pl.BlockSpec((tm, RET_DK), tok),
        pl.BlockSpec((None, XATTN_WIDTH, N_MEM), lambda b, s: (b, 0, 0)),
        pl.BlockSpec((None, N_MEM, XATTN_WIDTH), lambda b, s: (b, 0, 0)),
        _const_spec(mask.shape), _const_spec(q_dec.shape), _const_spec(k_dec.shape),
        _const_spec(gmix.shape), _const_spec(w_in.shape), _const_spec(pool_w.shape),
        _const_spec(pool_scale.shape), _const_spec(ret_g.shape),
        _const_spec(w_up_pool.shape), _const_spec(w_up_ret.shape), _const_spec(w_up_mem.shape),
        _const_spec(w_out.shape),
    ]
    return pl.pallas_call(
        functools.partial(_mixer_kernel, chunk_decay),
        grid=(batch, steps),
        in_specs=in_specs,
        out_specs=pl.BlockSpec((tm, d), tok),
        out_shape=jax.ShapeDtypeStruct((t, d), F32),
        scratch_shapes=[
            pltpu.VMEM((tm, d), BF16),
            pltpu.VMEM((POOL_HALO + tm, POOL_WIDTH), F32),
            pltpu.VMEM((RET_HEADS, RET_DK, RET_DV), F32),
            pltpu.VMEM((tm, RET_QK_WIDTH), BF16),
            pltpu.VMEM((tm, RET_QK_WIDTH), BF16),
            pltpu.VMEM((tm, RET_QK_WIDTH), BF16),
            pltpu.VMEM((RET_HEADS, tm, RET_CHUNK + RET_DK), BF16),
            pltpu.VMEM((RET_HEADS * (tm // RET_CHUNK), RET_CHUNK + RET_DK, RET_DV), BF16),
            pltpu.VMEM((tm, RET_V_WIDTH), F32),
            pltpu.VMEM((tm, POOL_WIDTH), BF16),
            pltpu.VMEM((tm, RET_V_WIDTH), BF16),
            pltpu.VMEM((tm, XATTN_WIDTH), BF16),
            pltpu.VMEM((tm, D_MODEL), BF16),
        ],
        compiler_params=pltpu.CompilerParams(
            dimension_semantics=("arbitrary", "arbitrary"),
            vmem_limit_bytes=VMEM_LIMIT_BYTES),
        name="mixer",
    )(x2d, cos2, sin2, kt, vm, mask, q_dec, k_dec, gmix, w_in, pool_w, pool_scale, ret_g,
      w_up_pool, w_up_ret, w_up_mem, w_out)


def _mlp_kernel(apply_final_norm, x_ref, g_ref, w1_ref, w2_ref, gf_ref, o_ref, hid_ref):
    x = x_ref[...]
    h = _rms_norm(x, g_ref[...]).astype(BF16)
    for j in range(D_FF // N_CHUNK):
        nc = slice(j * N_CHUNK, (j + 1) * N_CHUNK)
        a = jnp.maximum(_dot(h, w1_ref[:, nc]), 0.0)
        hid_ref[:, nc] = (a * a).astype(BF16)
    y = x + _dot(hid_ref[...], w2_ref[...])
    if apply_final_norm:
        y = _rms_norm(y, gf_ref[...])
    o_ref[...] = y


def _mlp_layer(x2d, g, w1, w2, g_final, apply_final_norm):
    t, d = x2d.shape
    tm = min(TOKEN_TILE, t)
    return pl.pallas_call(
        functools.partial(_mlp_kernel, apply_final_norm),
        grid=(t // tm,),
        in_specs=[pl.BlockSpec((tm, d), lambda i: (i, 0)),
                  _const_spec(g.shape), _const_spec(w1.shape), _const_spec(w2.shape),
                  _const_spec(g_final.shape)],
        out_specs=pl.BlockSpec((tm, d), lambda i: (i, 0)),
        out_shape=jax.ShapeDtypeStruct((t, d), F32),
        scratch_shapes=[pltpu.VMEM((tm, D_FF), BF16)],
        compiler_params=pltpu.CompilerParams(
            dimension_semantics=("arbitrary",),
            vmem_limit_bytes=VMEM_LIMIT_BYTES),
        name="mlp",
    )(x2d, g, w1, w2, g_final)


def kernel(x, mem, positions, norm_mix_g, w_in, pool_w, pool_scale, ret_norm_g, mem_norm_g,
           w_mem_kv, w_up_pool, w_up_ret, w_up_mem, w_out, norm_mlp_g, w_mlp1, w_mlp2,
           final_norm_g):
    batch, seq, d = x.shape
    depth = w_in.shape[0]
    assert d == D_MODEL and w_in.shape[-1] == IN_COLS and mem.shape[1] == N_MEM
    assert seq % RET_CHUNK == 0 and seq % min(TOKEN_TILE, seq) == 0

    cos2, sin2 = _rope_tables(positions)
    kt, vm = _mem_kv(mem, mem_norm_g, w_mem_kv)
    tables = _retention_tables()
    row = lambda a: a.reshape(1, -1)

    x2d = x.reshape(batch * seq, d)
    for l in range(depth):
        x2d = _mixer_layer(
            x2d, batch, cos2, sin2, kt[l], vm[l], tables, row(norm_mix_g[l]),
            w_in[l].astype(BF16), pool_w[l].astype(BF16), row(pool_scale[l]), row(ret_norm_g[l]),
            w_up_pool[l].astype(BF16), w_up_ret[l].astype(BF16), w_up_mem[l].astype(BF16),
            w_out[l].astype(BF16))
        x2d = _mlp_layer(x2d, row(norm_mlp_g[l]), w_mlp1[l].astype(BF16), w_mlp2[l].astype(BF16),
                         row(final_norm_g), apply_final_norm=(l == depth - 1))
    return x2d.reshape(batch, seq, d)
```

```python
import functools

import numpy as np
import jax
import jax.numpy as jnp
from jax import lax
from jax.experimental import pallas as pl
from jax.experimental.pallas import tpu as pltpu

D_MODEL = 1024
N_MEM = 256
POOL_WINDOWS = (2, 4, 8, 16)
POOL_GROUPS = 4
POOL_GROUP_DIM = 128
POOL_WIDTH = POOL_GROUPS * POOL_GROUP_DIM
POOL_HALO = 16
RET_HEADS = 4
RET_DK = 128
RET_DV = 256
RET_QK_WIDTH = RET_HEADS * RET_DK
RET_V_WIDTH = RET_HEADS * RET_DV
RET_CHUNK = 128
ROPE_BASE = 10000.0
XATTN_HEADS = 4
XATTN_DH = 128
XATTN_WIDTH = XATTN_HEADS * XATTN_DH
D_FF = 4 * D_MODEL
NORM_EPS = 1e-6

OFF_POOL = 0
OFF_RQ = OFF_POOL + POOL_WIDTH
OFF_RK = OFF_RQ + RET_QK_WIDTH
OFF_RV = OFF_RK + RET_QK_WIDTH
OFF_RG = OFF_RV + RET_V_WIDTH
OFF_XQ = OFF_RG + RET_V_WIDTH
OFF_GATE = OFF_XQ + XATTN_WIDTH
IN_COLS = OFF_GATE + 3 * D_MODEL

TOKEN_TILE = 512
ROPE_TILE = 1024
N_CHUNK = 512
MERGE_CHUNK = 256
VMEM_LIMIT_BYTES = 56 * 1024 * 1024

BF16 = jnp.bfloat16
F32 = jnp.float32


def _dot(a, b):
    return jnp.dot(a, b, preferred_element_type=F32)


def _rms_norm(x, g):
    return x * lax.rsqrt(jnp.mean(x * x, axis=-1, keepdims=True) + NORM_EPS) * g


def _const_spec(shape):
    nd = len(shape)
    return pl.BlockSpec(shape, lambda *_: (0,) * nd, pipeline_mode=pl.Buffered(1))


def _rope_kernel(pos_ref, freq_ref, cos_ref, sin_ref):
    ang = pos_ref[...] * freq_ref[...]
    lane = lax.broadcasted_iota(jnp.int32, ang.shape, 1)
    cos_ref[...] = jnp.cos(ang)
    s = jnp.sin(ang)
    sin_ref[...] = jnp.where(lane < RET_DK // 2, -s, s)


def _rope_tables(positions):
    t = positions.size
    half = RET_DK // 2
    inv_freq = ROPE_BASE ** (-jnp.arange(half, dtype=F32) / half)
    freq2 = jnp.concatenate([inv_freq, inv_freq]).reshape(1, RET_DK)
    pos = positions.astype(F32).reshape(t, 1)
    tile = min(ROPE_TILE, t)
    return pl.pallas_call(
        _rope_kernel,
        grid=(t // tile,),
        in_specs=[pl.BlockSpec((tile, 1), lambda i: (i, 0)),
                  pl.BlockSpec((1, RET_DK), lambda i: (0, 0))],
        out_specs=[pl.BlockSpec((tile, RET_DK), lambda i: (i, 0))] * 2,
        out_shape=[jax.ShapeDtypeStruct((t, RET_DK), F32)] * 2,
        name="rope_tables",
    )(pos, freq2)


def _mem_kv_kernel(mem_ref, g_ref, w_ref, kt_ref, v_ref):
    mem_n = _rms_norm(mem_ref[...], g_ref[...]).astype(BF16)
    kv = _dot(mem_n, w_ref[...])
    kt_ref[...] = kv[:, :XATTN_WIDTH].T.astype(BF16)
    v_ref[...] = kv[:, XATTN_WIDTH:].astype(BF16)


def _mem_kv(mem, mem_norm_g, w_mem_kv):
    depth = w_mem_kv.shape[0]
    b, m, d = mem.shape
    return pl.pallas_call(
        _mem_kv_kernel,
        grid=(depth, b),
        in_specs=[pl.BlockSpec((None, m, d), lambda l, i: (i, 0, 0)),
                  pl.BlockSpec((None, 1, d), lambda l, i: (l, 0, 0)),
                  pl.BlockSpec((None, d, 2 * XATTN_WIDTH), lambda l, i: (l, 0, 0))],
        out_specs=[pl.BlockSpec((None, None, XATTN_WIDTH, m), lambda l, i: (l, i, 0, 0)),
                   pl.BlockSpec((None, None, m, XATTN_WIDTH), lambda l, i: (l, i, 0, 0))],
        out_shape=[jax.ShapeDtypeStruct((depth, b, XATTN_WIDTH, m), BF16),
                   jax.ShapeDtypeStruct((depth, b, m, XATTN_WIDTH), BF16)],
        name="mem_kv",
    )(mem, mem_norm_g.reshape(depth, 1, d), w_mem_kv.astype(BF16))


def _retention_tables():
    c = RET_CHUNK
    log_gamma = np.log(1.0 - 2.0 ** (-5.0 - np.arange(RET_HEADS, dtype=np.float64)))
    idx = np.arange(c, dtype=np.float64)
    diff = idx[:, None] - idx[None, :]
    mask = np.where(diff[None] >= 0, np.exp(diff[None] * log_gamma[:, None, None]), 0.0)
    q_dec = np.exp((idx + 1.0)[None, :] * log_gamma[:, None])
    k_dec = np.exp((c - 1.0 - idx)[None, :] * log_gamma[:, None])
    q_dec = np.broadcast_to(q_dec[:, :, None], (RET_HEADS, c, RET_DK))
    k_dec = np.broadcast_to(k_dec[:, :, None], (RET_HEADS, c, RET_DK))
    chunk_decay = tuple(float(v) for v in np.exp(c * log_gamma))
    as_f32 = lambda a: jnp.asarray(np.ascontiguousarray(a), dtype=F32)
    return as_f32(mask), as_f32(q_dec), as_f32(k_dec), chunk_decay


def _mixer_kernel(chunk_decay,
                  x_ref, cos_ref, sin_ref, kt_ref, vm_ref, mask_ref, qdec_ref, kdec_ref,
                  gmix_ref, w_in_ref, pool_w_ref, pool_scale_ref, ret_g_ref,
                  w_up_pool_ref, w_up_ret_ref, w_up_mem_ref, w_out_ref,
                  o_ref,
                  h_ref, ubuf_ref, state_ref, qb_ref, kb_ref, kd_ref, lhs_ref, rhs_ref, gsilu_ref,
                  ypool_ref, yret_ref, ymem_ref, macc_ref, merged_ref):
    tm = x_ref.shape[0]
    n_chunks = tm // RET_CHUNK
    seq_step = pl.program_id(1)

    @pl.when(seq_step == 0)
    def _():
        ubuf_ref[0:POOL_HALO, :] = jnp.zeros((POOL_HALO, POOL_WIDTH), F32)
        state_ref[...] = jnp.zeros_like(state_ref)

    h_ref[...] = _rms_norm(x_ref[...], gmix_ref[...]).astype(BF16)

    def proj(off, width):
        return _dot(h_ref[...], w_in_ref[:, off:off + width])

    def chunk_rows(c):
        return slice(c * RET_CHUNK, (c + 1) * RET_CHUNK)

    def merge_chunk(b, y_ref, w_up_ref, j):
        cols = slice(j * MERGE_CHUNK, (j + 1) * MERGE_CHUNK)
        g = jax.nn.sigmoid(proj(OFF_GATE + b * D_MODEL + j * MERGE_CHUNK, MERGE_CHUNK))
        return g * _dot(y_ref[...], w_up_ref[:, cols])


    cos2 = cos_ref[...]
    sin2 = sin_ref[...]
    q_all = proj(OFF_RQ, RET_QK_WIDTH)
    k_all = proj(OFF_RK, RET_QK_WIDTH)
    for hh in range(RET_HEADS):
        qk = slice(hh * RET_DK, (hh + 1) * RET_DK)
        qh = q_all[:, qk]
        kh = k_all[:, qk]
        qr = qh * cos2 + pltpu.roll(qh, RET_DK // 2, 1) * sin2
        kr = (kh * cos2 + pltpu.roll(kh, RET_DK // 2, 1) * sin2) * (RET_DK ** -0.5)
        qb_ref[:, qk] = qr.astype(BF16)
        kb_ref[:, qk] = kr.astype(BF16)
        v_h = proj(OFF_RV + hh * RET_DV, RET_DV).astype(BF16)
        for c in range(n_chunks):
            rows = chunk_rows(c)
            lhs_ref[hh, rows, RET_CHUNK:] = (qr[rows] * qdec_ref[hh]).astype(BF16)
            kd_ref[rows, qk] = (kr[rows] * kdec_ref[hh]).astype(BF16)
            rhs_ref[hh * n_chunks + c, 0:RET_CHUNK, :] = v_h[rows]

    ubuf_ref[POOL_HALO:POOL_HALO + tm, :] = proj(OFF_POOL, POOL_WIDTH)
    row = lax.broadcasted_iota(jnp.int32, (tm, 1), 0) + seq_step * tm
    for gi, w in enumerate(POOL_WINDOWS):
        cols = slice(gi * POOL_GROUP_DIM, (gi + 1) * POOL_GROUP_DIM)
        win = ubuf_ref[:, cols]
        shift = 1
        while shift < w:
            win = win + pltpu.roll(win, shift, 0)
            shift *= 2
        cnt = jnp.minimum(row + 1, w).astype(F32)
        dwin = win[POOL_HALO:, :] / cnt - ubuf_ref[POOL_HALO:POOL_HALO + tm, cols]
        yg = _dot(dwin.astype(BF16), pool_w_ref[gi]) * pool_scale_ref[:, cols]
        ypool_ref[:, cols] = yg.astype(BF16)
        vv = slice(gi * RET_DV, (gi + 1) * RET_DV)
        g_h = proj(OFF_RG + gi * RET_DV, RET_DV)
        gsilu_ref[:, vv] = g_h * jax.nn.sigmoid(g_h)
    ubuf_ref[0:POOL_HALO, :] = ubuf_ref[tm:tm + POOL_HALO, :]

    for hh in range(RET_HEADS):
        qk = slice(hh * RET_DK, (hh + 1) * RET_DK)
        for c in range(n_chunks):
            rows = chunk_rows(c)
            scores = lax.dot_general(qb_ref[rows, qk], kb_ref[rows, qk],
                                     (((1,), (1,)), ((), ())),
                                     preferred_element_type=F32) * mask_ref[hh]
            lhs_ref[hh, rows, 0:RET_CHUNK] = scores.astype(BF16)

    for hh in range(RET_HEADS):
        qk = slice(hh * RET_DK, (hh + 1) * RET_DK)
        kvs = [lax.dot_general(kd_ref[chunk_rows(c), qk], rhs_ref[hh * n_chunks + c, 0:RET_CHUNK, :],
                               (((0,), (0,)), ((), ())), preferred_element_type=F32)
               for c in range(n_chunks)]
        s = state_ref[hh]
        for c in range(n_chunks):
            rhs_ref[hh * n_chunks + c, RET_CHUNK:, :] = s.astype(BF16)
            s = chunk_decay[hh] * s + kvs[c]
        state_ref[hh] = s

    xq = proj(OFF_XQ, XATTN_WIDTH).astype(BF16)

    for hh in range(RET_HEADS):
        vv = slice(hh * RET_DV, (hh + 1) * RET_DV)
        for c in range(n_chunks):
            rows = chunk_rows(c)
            y = _dot(lhs_ref[hh, rows, :], rhs_ref[hh * n_chunks + c])
            y = y * lax.rsqrt(jnp.mean(y * y, axis=-1, keepdims=True) + NORM_EPS)
            y = y * ret_g_ref[:, vv]
            yret_ref[rows, vv] = (gsilu_ref[rows, vv] * y).astype(BF16)
        mc = slice(hh * MERGE_CHUNK, (hh + 1) * MERGE_CHUNK)
        macc_ref[:, mc] = merge_chunk(0, ypool_ref, w_up_pool_ref, hh)

    for hh in range(XATTN_HEADS):
        hd = slice(hh * XATTN_DH, (hh + 1) * XATTN_DH)
        s = _dot(xq[:, hd], kt_ref[hd, :]) * (XATTN_DH ** -0.5)
        s = s - jnp.max(s, axis=-1, keepdims=True)
        p = jnp.exp(s)
        p = p / jnp.sum(p, axis=-1, keepdims=True)
        ymem_ref[:, hd] = _dot(p.astype(BF16), vm_ref[:, hd]).astype(BF16)
        mc = slice(hh * MERGE_CHUNK, (hh + 1) * MERGE_CHUNK)
        macc_ref[:, mc] = macc_ref[:, mc] + merge_chunk(1, yret_ref, w_up_ret_ref, hh)

    for j in range(D_MODEL // MERGE_CHUNK):
        mc = slice(j * MERGE_CHUNK, (j + 1) * MERGE_CHUNK)
        merged_ref[:, mc] = (macc_ref[:, mc]
                             + merge_chunk(2, ymem_ref, w_up_mem_ref, j)).astype(BF16)
    o_ref[...] = x_ref[...] + _dot(merged_ref[...], w_out_ref[...])


def _mixer_layer(x2d, batch, cos2, sin2, kt, vm, tables, gmix, w_in, pool_w, pool_scale, ret_g,
                 w_up_pool, w_up_ret, w_up_mem, w_out):
    t, d = x2d.shape
    seq = t // batch
    tm = min(TOKEN_TILE, seq)
    steps = seq // tm
    mask, q_dec, k_dec, chunk_decay = tables
    tok = lambda b, s: (b * steps + s, 0)
    in_specs = [
        pl.BlockSpec((tm, d), tok),
        pl.BlockSpec((tm, RET_DK), tok),
        pl.BlockSpec((tm, RET_DK), tok),
        pl.BlockSpec((None, XATTN_WIDTH, N_MEM), lambda b, s: (b, 0, 0)),
        pl.BlockSpec((None, N_MEM, XATTN_WIDTH), lambda b, s: (b, 0, 0)),
        _const_spec(mask.shape), _const_spec(q_dec.shape), _const_spec(k_dec.shape),
        _const_spec(gmix.shape), _const_spec(w_in.shape), _const_spec(pool_w.shape),
        _const_spec(pool_scale.shape), _const_spec(ret_g.shape),
        _const_spec(w_up_pool.shape), _const_spec(w_up_ret.shape), _const_spec(w_up_mem.shape),
        _const_spec(w_out.shape),
    ]
    return pl.pallas_call(
        functools.partial(_mixer_kernel, chunk_decay),
        grid=(batch, steps),
        in_specs=in_specs,
        out_specs=pl.BlockSpec((tm, d), tok),
        out_shape=jax.ShapeDtypeStruct((t, d), F32),
        scratch_shapes=[
            pltpu.VMEM((tm, d), BF16),
            pltpu.VMEM((POOL_HALO + tm, POOL_WIDTH), F32),
            pltpu.VMEM((RET_HEADS, RET_DK, RET_DV), F32),
            pltpu.VMEM((tm, RET_QK_WIDTH), BF16),
            pltpu.VMEM((tm, RET_QK_WIDTH), BF16),
            pltpu.VMEM((tm, RET_QK_WIDTH), BF16),
            pltpu.VMEM((RET_HEADS, tm, RET_CHUNK + RET_DK), BF16),
            pltpu.VMEM((RET_HEADS * (tm // RET_CHUNK), RET_CHUNK + RET_DK, RET_DV), BF16),
            pltpu.VMEM((tm, RET_V_WIDTH), F32),
            pltpu.VMEM((tm, POOL_WIDTH), BF16),
            pltpu.VMEM((tm, RET_V_WIDTH), BF16),
            pltpu.VMEM((tm, XATTN_WIDTH), BF16),
            pltpu.VMEM((tm, D_MODEL), F32),
            pltpu.VMEM((tm, D_MODEL), BF16),
        ],
        compiler_params=pltpu.CompilerParams(
            dimension_semantics=("arbitrary", "arbitrary"),
            vmem_limit_bytes=VMEM_LIMIT_BYTES),
        name="mixer",
    )(x2d, cos2, sin2, kt, vm, mask, q_dec, k_dec, gmix, w_in, pool_w, pool_scale, ret_g,
      w_up_pool, w_up_ret, w_up_mem, w_out)


def _mlp_kernel(apply_final_norm, x_ref, g_ref, w1_ref, w2_ref, gf_ref, o_ref, hid_ref):
    x = x_ref[...]
    h = _rms_norm(x, g_ref[...]).astype(BF16)
    for j in range(D_FF // N_CHUNK):
        nc = slice(j * N_CHUNK, (j + 1) * N_CHUNK)
        a = jnp.maximum(_dot(h, w1_ref[:, nc]), 0.0)
        hid_ref[:, nc] = (a * a).astype(BF16)
    y = x + _dot(hid_ref[...], w2_ref[...])
    if apply_final_norm:
        y = _rms_norm(y, gf_ref[...])
    o_ref[...] = y


def _mlp_layer(x2d, g, w1, w2, g_final, apply_final_norm):
    t, d = x2d.shape
    tm = min(TOKEN_TILE, t)
    return pl.pallas_call(
        functools.partial(_mlp_kernel, apply_final_norm),
        grid=(t // tm,),
        in_specs=[pl.BlockSpec((tm, d), lambda i: (i, 0)),
                  _const_spec(g.shape), _const_spec(w1.shape), _const_spec(w2.shape),
                  _const_spec(g_final.shape)],
        out_specs=pl.BlockSpec((tm, d), lambda i: (i, 0)),
        out_shape=jax.ShapeDtypeStruct((t, d), F32),
        scratch_shapes=[pltpu.VMEM((tm, D_FF), BF16)],
        compiler_params=pltpu.CompilerParams(
            dimension_semantics=("arbitrary",),
            vmem_limit_bytes=VMEM_LIMIT_BYTES),
        name="mlp",
    )(x2d, g, w1, w2, g_final)


def kernel(x, mem, positions, norm_mix_g, w_in, pool_w, pool_scale, ret_norm_g, mem_norm_g,
           w_mem_kv, w_up_pool, w_up_ret, w_up_mem, w_out, norm_mlp_g, w_mlp1, w_mlp2,
           final_norm_g):
    batch, seq, d = x.shape
    depth = w_in.shape[0]
    assert d == D_MODEL and w_in.shape[-1] == IN_COLS and mem.shape[1] == N_MEM
    assert seq % RET_CHUNK == 0 and seq % min(TOKEN_TILE, seq) == 0

    cos2, sin2 = _rope_tables(positions)
    kt, vm = _mem_kv(mem, mem_norm_g, w_mem_kv)
    tables = _retention_tables()
    row = lambda a: a.reshape(1, -1)

    x2d = x.reshape(batch * seq, d)
    for l in range(depth):
        x2d = _mixer_layer(
            x2d, batch, cos2, sin2, kt[l], vm[l], tables, row(norm_mix_g[l]),
            w_in[l].astype(BF16), pool_w[l].astype(BF16), row(pool_scale[l]), row(ret_norm_g[l]),
            w_up_pool[l].astype(BF16), w_up_ret[l].astype(BF16), w_up_mem[l].astype(BF16),
            w_out[l].astype(BF16))
        x2d = _mlp_layer(x2d, row(norm_mlp_g[l]), w_mlp1[l].astype(BF16), w_mlp2[l].astype(BF16),
                         row(final_norm_g), apply_final_norm=(l == depth - 1))
    return x2d.reshape(batch, seq, d)
```

```python
import functools

import numpy as np
import jax
import jax.numpy as jnp
from jax import lax
from jax.experimental import pallas as pl
from jax.experimental.pallas import tpu as pltpu

D_MODEL = 1024
N_MEM = 256
POOL_WINDOWS = (2, 4, 8, 16)
POOL_GROUPS = 4
POOL_GROUP_DIM = 128
POOL_WIDTH = POOL_GROUPS * POOL_GROUP_DIM
POOL_HALO = 16
RET_HEADS = 4
RET_DK = 128
RET_DV = 256
RET_QK_WIDTH = RET_HEADS * RET_DK
RET_V_WIDTH = RET_HEADS * RET_DV
RET_CHUNK = 128
ROPE_BASE = 10000.0
XATTN_HEADS = 4
XATTN_DH = 128
XATTN_WIDTH = XATTN_HEADS * XATTN_DH
D_FF = 4 * D_MODEL
NORM_EPS = 1e-6

OFF_POOL = 0
OFF_RQ = OFF_POOL + POOL_WIDTH
OFF_RK = OFF_RQ + RET_QK_WIDTH
OFF_RV = OFF_RK + RET_QK_WIDTH
OFF_RG = OFF_RV + RET_V_WIDTH
OFF_XQ = OFF_RG + RET_V_WIDTH
OFF_GATE = OFF_XQ + XATTN_WIDTH
IN_COLS = OFF_GATE + 3 * D_MODEL

TOKEN_TILE = 512
MLP_TOKEN_TILE = 1024
ROPE_TILE = 1024
N_CHUNK = 512
MERGE_CHUNK = 256
VMEM_LIMIT_BYTES = 56 * 1024 * 1024

SOFTMAX_EXP2_SCALE = float(XATTN_DH ** -0.5 * np.log2(np.e))

BF16 = jnp.bfloat16
F32 = jnp.float32


def _dot(a, b):
    return jnp.dot(a, b, preferred_element_type=F32)


def _rms_norm(x, g):
    return x * lax.rsqrt(jnp.mean(x * x, axis=-1, keepdims=True) + NORM_EPS) * g


def _const_spec(shape):
    nd = len(shape)
    return pl.BlockSpec(shape, lambda *_: (0,) * nd, pipeline_mode=pl.Buffered(1))


def _layer_spec(stacked_shape, layer):
    nd = len(stacked_shape) - 1
    return pl.BlockSpec((None,) + tuple(stacked_shape[1:]), lambda *_: (layer,) + (0,) * nd,
                        pipeline_mode=pl.Buffered(1))


def _rope_kernel(pos_ref, freq_ref, cos_ref, sin_ref):
    lane = lax.broadcasted_iota(jnp.int32, cos_ref.shape, 1)
    pos = pos_ref[...]
    ang = jnp.where(lane < RET_DK // 2, pos[:, 0:1], pos[:, 1:2]) * freq_ref[...]
    cos_ref[...] = jnp.cos(ang)
    sin_ref[...] = jnp.sin(ang)


def _rope_tables(positions, token_tile):
    t = positions.size
    half_tile = token_tile // 2
    half = RET_DK // 2
    inv_freq = ROPE_BASE ** (-jnp.arange(half, dtype=F32) / half)
    freq2 = jnp.concatenate([inv_freq, inv_freq]).reshape(1, RET_DK)
    pos = positions.astype(F32).reshape(t // token_tile, 2, half_tile)
    pos = jnp.swapaxes(pos, 1, 2).reshape(t // 2, 2)
    rows = min(ROPE_TILE, t // 2)
    return pl.pallas_call(
        _rope_kernel,
        grid=(t // 2 // rows,),
        in_specs=[pl.BlockSpec((rows, 2), lambda i: (i, 0)),
                  pl.BlockSpec((1, RET_DK), lambda i: (0, 0))],
        out_specs=[pl.BlockSpec((rows, RET_DK), lambda i: (i, 0))] * 2,
        out_shape=[jax.ShapeDtypeStruct((t // 2, RET_DK), F32)] * 2,
        name="rope_tables",
    )(pos, freq2)


def _mem_kv_kernel(mem_ref, g_ref, w_ref, kt_ref, v_ref):
    mem_n = _rms_norm(mem_ref[...], g_ref[...]).astype(BF16)
    kv = _dot(mem_n, w_ref[...])
    kt_ref[...] = kv[:, :XATTN_WIDTH].T.astype(BF16)
    v_ref[...] = kv[:, XATTN_WIDTH:].astype(BF16)


def _mem_kv(mem, mem_norm_g, w_mem_kv):
    depth = w_mem_kv.shape[0]
    b, m, d = mem.shape
    return pl.pallas_call(
        _mem_kv_kernel,
        grid=(depth, b),
        in_specs=[pl.BlockSpec((None, m, d), lambda l, i: (i, 0, 0)),
                  pl.BlockSpec((None, 1, d), lambda l, i: (l, 0, 0)),
                  pl.BlockSpec((None, d, 2 * XATTN_WIDTH), lambda l, i: (l, 0, 0))],
        out_specs=[pl.BlockSpec((None, None, XATTN_WIDTH, m), lambda l, i: (l, i, 0, 0)),
                   pl.BlockSpec((None, None, m, XATTN_WIDTH), lambda l, i: (l, i, 0, 0))],
        out_shape=[jax.ShapeDtypeStruct((depth, b, XATTN_WIDTH, m), BF16),
                   jax.ShapeDtypeStruct((depth, b, m, XATTN_WIDTH), BF16)],
        name="mem_kv",
    )(mem, mem_norm_g.reshape(depth, 1, d), w_mem_kv.astype(BF16))


def _retention_tables():
    c = RET_CHUNK
    log_gamma = np.log(1.0 - 2.0 ** (-5.0 - np.arange(RET_HEADS, dtype=np.float64)))
    idx = np.arange(c, dtype=np.float64)
    diff = idx[:, None] - idx[None, :]
    mask = np.where(diff[None] >= 0, np.exp(diff[None] * log_gamma[:, None, None]), 0.0)
    q_dec = np.exp((idx + 1.0)[None, :] * log_gamma[:, None])
    k_dec = np.exp((c - 1.0 - idx)[None, :] * log_gamma[:, None])
    q_dec = np.broadcast_to(q_dec[:, :, None], (RET_HEADS, c, RET_DK))
    k_dec = np.broadcast_to(k_dec[:, :, None], (RET_HEADS, c, RET_DK))
    chunk_decay = tuple(float(v) for v in np.exp(c * log_gamma))
    as_f32 = lambda a: jnp.asarray(np.ascontiguousarray(a), dtype=F32)
    return as_f32(mask), as_f32(q_dec), as_f32(k_dec), chunk_decay


def _mixer_kernel(chunk_decay,
                  x_ref, cos_ref, sin_ref, kt_ref, vm_ref, mask_ref, qdec_ref, kdec_ref,
                  gmix_ref, w_in_ref, pool_w_ref, pool_scale_ref, ret_g_ref,
                  w_up_pool_ref, w_up_ret_ref, w_up_mem_ref, w_out_ref,
                  o_ref,
                  h_ref, ubuf_ref, state_ref, qb_ref, kb_ref, kd_ref, lhs_ref, rhs_ref, gsilu_ref,
                  ypool_ref, yret_ref, ymem_ref, macc_ref, merged_ref):
    tm = x_ref.shape[0]
    n_chunks = tm // RET_CHUNK
    seq_step = pl.program_id(1)

    @pl.when(seq_step == 0)
    def _():
        ubuf_ref[0:POOL_HALO, :] = jnp.zeros((POOL_HALO, POOL_WIDTH), F32)
        state_ref[...] = jnp.zeros_like(state_ref)

    h_ref[...] = _rms_norm(x_ref[...], gmix_ref[...]).astype(BF16)

    def proj(off, width):
        return _dot(h_ref[...], w_in_ref[:, off:off + width])

    def chunk_rows(c):
        return slice(c * RET_CHUNK, (c + 1) * RET_CHUNK)

    def merge_chunk(b, y_ref, w_up_ref, j):
        cols = slice(j * MERGE_CHUNK, (j + 1) * MERGE_CHUNK)
        g = jax.nn.sigmoid(proj(OFF_GATE + b * D_MODEL + j * MERGE_CHUNK, MERGE_CHUNK))
        return g * _dot(y_ref[...], w_up_ref[:, cols])


    lane = lax.broadcasted_iota(jnp.int32, cos_ref.shape, 1)
    first = lane < RET_DK // 2
    cos_p = cos_ref[...]
    sin_p = sin_ref[...]
    cos_s = pltpu.roll(cos_p, RET_DK // 2, 1)
    sin_s = pltpu.roll(sin_p, RET_DK // 2, 1)
    cos2 = jnp.concatenate([jnp.where(first, cos_p, cos_s), jnp.where(first, cos_s, cos_p)], axis=0)
    sin2 = jnp.concatenate([jnp.where(first, -sin_p, sin_s), jnp.where(first, -sin_s, sin_p)], axis=0)
    q_all = proj(OFF_RQ, RET_QK_WIDTH)
    k_all = proj(OFF_RK, RET_QK_WIDTH)
    for hh in range(RET_HEADS):
        qk = slice(hh * RET_DK, (hh + 1) * RET_DK)
        qh = q_all[:, qk]
        kh = k_all[:, qk]
        qr = qh * cos2 + pltpu.roll(qh, RET_DK // 2, 1) * sin2
        kr = (kh * cos2 + pltpu.roll(kh, RET_DK // 2, 1) * sin2) * (RET_DK ** -0.5)
        qb_ref[:, qk] = qr.astype(BF16)
        kb_ref[:, qk] = kr.astype(BF16)
        v_h = proj(OFF_RV + hh * RET_DV, RET_DV).astype(BF16)
        for c in range(n_chunks):
            rows = chunk_rows(c)
            lhs_ref[hh, rows, RET_CHUNK:] = (qr[rows] * qdec_ref[hh]).astype(BF16)
            kd_ref[rows, qk] = (kr[rows] * kdec_ref[hh]).astype(BF16)
            rhs_ref[hh * n_chunks + c, 0:RET_CHUNK, :] = v_h[rows]

    ubuf_ref[POOL_HALO:POOL_HALO + tm, :] = proj(OFF_POOL, POOL_WIDTH)
    row = lax.broadcasted_iota(jnp.int32, (tm, 1), 0) + seq_step * tm
    for gi, w in enumerate(POOL_WINDOWS):
        cols = slice(gi * POOL_GROUP_DIM, (gi + 1) * POOL_GROUP_DIM)
        win = ubuf_ref[:, cols]
        shift = 1
        while shift < w:
            win = win + pltpu.roll(win, shift, 0)
            shift *= 2
        cnt = jnp.minimum(row + 1, w).astype(F32)
        dwin = win[POOL_HALO:, :] / cnt - ubuf_ref[POOL_HALO:POOL_HALO + tm, cols]
        yg = _dot(dwin.astype(BF16), pool_w_ref[gi]) * pool_scale_ref[:, cols]
        ypool_ref[:, cols] = yg.astype(BF16)
        vv = slice(gi * RET_DV, (gi + 1) * RET_DV)
        g_h = proj(OFF_RG + gi * RET_DV, RET_DV)
        gsilu_ref[:, vv] = g_h * jax.nn.sigmoid(g_h) * ret_g_ref[:, vv]
    ubuf_ref[0:POOL_HALO, :] = ubuf_ref[tm:tm + POOL_HALO, :]

    for hh in range(RET_HEADS):
        qk = slice(hh * RET_DK, (hh + 1) * RET_DK)
        for c in range(n_chunks):
            rows = chunk_rows(c)
            scores = lax.dot_general(qb_ref[rows, qk], kb_ref[rows, qk],
                                     (((1,), (1,)), ((), ())),
                                     preferred_element_type=F32) * mask_ref[hh]
            lhs_ref[hh, rows, 0:RET_CHUNK] = scores.astype(BF16)

    for hh in range(RET_HEADS):
        qk = slice(hh * RET_DK, (hh + 1) * RET_DK)
        kvs = [lax.dot_general(kd_ref[chunk_rows(c), qk], rhs_ref[hh * n_chunks + c, 0:RET_CHUNK, :],
                               (((0,), (0,)), ((), ())), preferred_element_type=F32)
               for c in range(n_chunks)]
        s = state_ref[hh]
        for c in range(n_chunks):
            rhs_ref[hh * n_chunks + c, RET_CHUNK:, :] = s.astype(BF16)
            s = chunk_decay[hh] * s + kvs[c]
        state_ref[hh] = s

    xq = proj(OFF_XQ, XATTN_WIDTH).astype(BF16)

    for hh in range(RET_HEADS):
        vv = slice(hh * RET_DV, (hh + 1) * RET_DV)
        for c in range(n_chunks):
            rows = chunk_rows(c)
            y = _dot(lhs_ref[hh, rows, :], rhs_ref[hh * n_chunks + c])
            y = y * lax.rsqrt(jnp.mean(y * y, axis=-1, keepdims=True) + NORM_EPS)
            yret_ref[rows, vv] = (gsilu_ref[rows, vv] * y).astype(BF16)
        mc = slice(hh * MERGE_CHUNK, (hh + 1) * MERGE_CHUNK)
        macc_ref[:, mc] = merge_chunk(0, ypool_ref, w_up_pool_ref, hh)

    for hh in range(XATTN_HEADS):
        hd = slice(hh * XATTN_DH, (hh + 1) * XATTN_DH)
        s = _dot(xq[:, hd], kt_ref[hd, :])
        p = jnp.exp2((s - jnp.max(s, axis=-1, keepdims=True)) * SOFTMAX_EXP2_SCALE)
        p = p / jnp.sum(p, axis=-1, keepdims=True)
        ymem_ref[:, hd] = _dot(p.astype(BF16), vm_ref[:, hd]).astype(BF16)
        mc = slice(hh * MERGE_CHUNK, (hh + 1) * MERGE_CHUNK)
        macc_ref[:, mc] = macc_ref[:, mc] + merge_chunk(1, yret_ref, w_up_ret_ref, hh)

    for j in range(D_MODEL // MERGE_CHUNK):
        mc = slice(j * MERGE_CHUNK, (j + 1) * MERGE_CHUNK)
        merged_ref[:, mc] = (macc_ref[:, mc]
                             + merge_chunk(2, ymem_ref, w_up_mem_ref, j)).astype(BF16)
    o_ref[...] = x_ref[...] + _dot(merged_ref[...], w_out_ref[...])


def _mixer_layer(layer, x2d, batch, cos2, sin2, kt, vm, tables, gmix, w_in, pool_w, pool_scale,
                 ret_g, w_up_pool, w_up_ret, w_up_mem, w_out):
    t, d = x2d.shape
    seq = t // batch
    tm = min(TOKEN_TILE, seq)
    steps = seq // tm
    mask, q_dec, k_dec, chunk_decay = tables
    tok = lambda b, s: (b * steps + s, 0)
    in_specs = [
        pl.BlockSpec((tm, d), tok),
        pl.BlockSpec((tm // 2, RET_DK), tok),
        pl.BlockSpec((tm // 2, RET_DK), tok),
        pl.BlockSpec((None, None, XATTN_WIDTH, N_MEM), lambda b, s: (layer, b, 0, 0)),
        pl.BlockSpec((None, None, N_MEM, XATTN_WIDTH), lambda b, s: (layer, b, 0, 0)),
        _const_spec(mask.shape), _const_spec(q_dec.shape), _const_spec(k_dec.shape),
    ] + [_layer_spec(p.shape, layer) for p in
         (gmix, w_in, pool_w, pool_scale, ret_g, w_up_pool, w_up_ret, w_up_mem, w_out)]
    return pl.pallas_call(
        functools.partial(_mixer_kernel, chunk_decay),
        grid=(batch, steps),
        in_specs=in_specs,
        out_specs=pl.BlockSpec((tm, d), tok),
        out_shape=jax.ShapeDtypeStruct((t, d), F32),
        scratch_shapes=[
            pltpu.VMEM((tm, d), BF16),
            pltpu.VMEM((POOL_HALO + tm, POOL_WIDTH), F32),
            pltpu.VMEM((RET_HEADS, RET_DK, RET_DV), F32),
            pltpu.VMEM((tm, RET_QK_WIDTH), BF16),
            pltpu.VMEM((tm, RET_QK_WIDTH), BF16),
            pltpu.VMEM((tm, RET_QK_WIDTH), BF16),
            pltpu.VMEM((RET_HEADS, tm, RET_CHUNK + RET_DK), BF16),
            pltpu.VMEM((RET_HEADS * (tm // RET_CHUNK), RET_CHUNK + RET_DK, RET_DV), BF16),
            pltpu.VMEM((tm, RET_V_WIDTH), F32),
            pltpu.VMEM((tm, POOL_WIDTH), BF16),
            pltpu.VMEM((tm, RET_V_WIDTH), BF16),
            pltpu.VMEM((tm, XATTN_WIDTH), BF16),
            pltpu.VMEM((tm, D_MODEL), F32),
            pltpu.VMEM((tm, D_MODEL), BF16),
        ],
        compiler_params=pltpu.CompilerParams(
            dimension_semantics=("arbitrary", "arbitrary"),
            vmem_limit_bytes=VMEM_LIMIT_BYTES),
        name="mixer",
    )(x2d, cos2, sin2, kt, vm, mask, q_dec, k_dec, gmix, w_in, pool_w, pool_scale, ret_g,
      w_up_pool, w_up_ret, w_up_mem, w_out)


def _mlp_kernel(apply_final_norm, x_ref, g_ref, w1_ref, w2_ref, gf_ref, o_ref, hid_ref):
    x = x_ref[...]
    h = _rms_norm(x, g_ref[...]).astype(BF16)
    for j in range(D_FF // N_CHUNK):
        nc = slice(j * N_CHUNK, (j + 1) * N_CHUNK)
        a = jnp.maximum(_dot(h, w1_ref[:, nc]), 0.0)
        hid_ref[:, nc] = (a * a).astype(BF16)
    y = x + _dot(hid_ref[...], w2_ref[...])
    if apply_final_norm:
        y = _rms_norm(y, gf_ref[...])
    o_ref[...] = y


def _mlp_layer(layer, x2d, g, w1, w2, g_final, apply_final_norm):
    t, d = x2d.shape
    tm = min(MLP_TOKEN_TILE, t)
    return pl.pallas_call(
        functools.partial(_mlp_kernel, apply_final_norm),
        grid=(t // tm,),
        in_specs=[pl.BlockSpec((tm, d), lambda i: (i, 0)),
                  _layer_spec(g.shape, layer), _layer_spec(w1.shape, layer),
                  _layer_spec(w2.shape, layer), _const_spec(g_final.shape)],
        out_specs=pl.BlockSpec((tm, d), lambda i: (i, 0)),
        out_shape=jax.ShapeDtypeStruct((t, d), F32),
        scratch_shapes=[pltpu.VMEM((tm, D_FF), BF16)],
        compiler_params=pltpu.CompilerParams(
            dimension_semantics=("arbitrary",),
            vmem_limit_bytes=VMEM_LIMIT_BYTES),
        name="mlp",
    )(x2d, g, w1, w2, g_final)


def kernel(x, mem, positions, norm_mix_g, w_in, pool_w, pool_scale, ret_norm_g, mem_norm_g,
           w_mem_kv, w_up_pool, w_up_ret, w_up_mem, w_out, norm_mlp_g, w_mlp1, w_mlp2,
           final_norm_g):
    batch, seq, d = x.shape
    depth = w_in.shape[0]
    assert d == D_MODEL and w_in.shape[-1] == IN_COLS and mem.shape[1] == N_MEM
    assert seq % RET_CHUNK == 0 and seq % min(TOKEN_TILE, seq) == 0

    cos2, sin2 = _rope_tables(positions, min(TOKEN_TILE, seq))
    kt, vm = _mem_kv(mem, mem_norm_g, w_mem_kv)
    tables = _retention_tables()
    rows = lambda a: a.reshape(depth, 1, -1)
    bf16 = lambda a: a.astype(BF16)
    mixer_params = (rows(norm_mix_g), bf16(w_in), bf16(pool_w), rows(pool_scale), rows(ret_norm_g),
                    bf16(w_up_pool), bf16(w_up_ret), bf16(w_up_mem), bf16(w_out))
    mlp_params = (rows(norm_mlp_g), bf16(w_mlp1), bf16(w_mlp2))
    g_final = final_norm_g.reshape(1, -1)

    x2d = x.reshape(batch * seq, d)
    for l in range(depth):
        x2d = _mixer_layer(l, x2d, batch, cos2, sin2, kt, vm, tables, *mixer_params)
        x2d = _mlp_layer(l, x2d, *mlp_params, g_final, apply_final_norm=(l == depth - 1))
    return x2d.reshape(batch, seq, d)
```

```python
import functools

import numpy as np
import jax
import jax.numpy as jnp
from jax import lax
from jax.experimental import pallas as pl
from jax.experimental.pallas import tpu as pltpu

D_MODEL = 1024
N_MEM = 256
POOL_WINDOWS = (2, 4, 8, 16)
POOL_GROUPS = 4
POOL_GROUP_DIM = 128
POOL_WIDTH = POOL_GROUPS * POOL_GROUP_DIM
POOL_HALO = 16
RET_HEADS = 4
RET_DK = 128
RET_DV = 256
RET_QK_WIDTH = RET_HEADS * RET_DK
RET_V_WIDTH = RET_HEADS * RET_DV
RET_CHUNK = 128
ROPE_BASE = 10000.0
XATTN_HEADS = 4
XATTN_DH = 128
XATTN_WIDTH = XATTN_HEADS * XATTN_DH
D_FF = 4 * D_MODEL
NORM_EPS = 1e-6

OFF_POOL = 0
OFF_RQ = OFF_POOL + POOL_WIDTH
OFF_RK = OFF_RQ + RET_QK_WIDTH
OFF_RV = OFF_RK + RET_QK_WIDTH
OFF_RG = OFF_RV + RET_V_WIDTH
OFF_XQ = OFF_RG + RET_V_WIDTH
OFF_GATE = OFF_XQ + XATTN_WIDTH
IN_COLS = OFF_GATE + 3 * D_MODEL

TOKEN_TILE = 512
MLP_TOKEN_TILE = 1024
ROPE_TILE = 1024
N_CHUNK = 512
MERGE_CHUNK = 256
VMEM_LIMIT_BYTES = 56 * 1024 * 1024
BF16_SUBLANES = 16
N_MIXER_INPUTS = 17
N_MLP_INPUTS = 5

SOFTMAX_EXP2_SCALE = float(XATTN_DH ** -0.5 * np.log2(np.e))

BF16 = jnp.bfloat16
F32 = jnp.float32


def _dot(a, b):
    return jnp.dot(a, b, preferred_element_type=F32)


def _rms_norm(x, g):
    return x * lax.rsqrt(jnp.mean(x * x, axis=-1, keepdims=True) + NORM_EPS) * g


def _const_spec(shape):
    nd = len(shape)
    return pl.BlockSpec(shape, lambda *_: (0,) * nd, pipeline_mode=pl.Buffered(1))


def _layer_spec(stacked_shape, layer):
    nd = len(stacked_shape) - 1
    return pl.BlockSpec((None,) + tuple(stacked_shape[1:]), lambda *_: (layer,) + (0,) * nd,
                        pipeline_mode=pl.Buffered(1))


def _cast_specs(params, n_steps, index_of_step, layer=None):
    views, in_specs, out_specs, out_shapes = [], [], [], []
    for p in params:
        cols = p.shape[-1]
        rows_per_layer = int(np.prod(p.shape[1:-1]))
        rows = rows_per_layer if layer is not None else p.shape[0] * rows_per_layer
        slab, rem = divmod(rows, n_steps)
        assert rem == 0 and slab % BF16_SUBLANES == 0, (p.shape, n_steps)
        first = 0 if layer is None else layer * n_steps
        views.append(p.reshape(-1, cols))
        in_specs.append(pl.BlockSpec((slab, cols), lambda *g, f=first: (f + index_of_step(*g), 0)))
        out_specs.append(pl.BlockSpec((slab, cols), lambda *g: (index_of_step(*g), 0)))
        out_shapes.append(jax.ShapeDtypeStruct((rows, cols), BF16))
    return views, in_specs, out_specs, out_shapes


def _cast_blocks(in_refs, out_refs):
    for src_ref, dst_ref in zip(in_refs, out_refs):
        dst_ref[...] = src_ref[...].astype(BF16)


def _rope_kernel(pos_ref, freq_ref, cos_ref, sin_ref):
    lane = lax.broadcasted_iota(jnp.int32, cos_ref.shape, 1)
    pos = pos_ref[...]
    ang = jnp.where(lane < RET_DK // 2, pos[:, 0:1], pos[:, 1:2]) * freq_ref[...]
    cos_ref[...] = jnp.cos(ang)
    sin_ref[...] = jnp.sin(ang)


def _rope_tables(positions, token_tile):
    t = positions.size
    half_tile = token_tile // 2
    half = RET_DK // 2
    inv_freq = ROPE_BASE ** (-jnp.arange(half, dtype=F32) / half)
    freq2 = jnp.concatenate([inv_freq, inv_freq]).reshape(1, RET_DK)
    pos = positions.astype(F32).reshape(t // token_tile, 2, half_tile)
    pos = jnp.swapaxes(pos, 1, 2).reshape(t // 2, 2)
    rows = min(ROPE_TILE, t // 2)
    return pl.pallas_call(
        _rope_kernel,
        grid=(t // 2 // rows,),
        in_specs=[pl.BlockSpec((rows, 2), lambda i: (i, 0)),
                  pl.BlockSpec((1, RET_DK), lambda i: (0, 0))],
        out_specs=[pl.BlockSpec((rows, RET_DK), lambda i: (i, 0))] * 2,
        out_shape=[jax.ShapeDtypeStruct((t // 2, RET_DK), F32)] * 2,
        name="rope_tables",
    )(pos, freq2)


def _mem_kv_kernel(mem_ref, g_ref, w_ref, kt_ref, v_ref):
    mem_n = _rms_norm(mem_ref[...], g_ref[...]).astype(BF16)
    kv = _dot(mem_n, w_ref[...])
    kt_ref[...] = kv[:, :XATTN_WIDTH].T.astype(BF16)
    v_ref[...] = kv[:, XATTN_WIDTH:].astype(BF16)


def _mem_kv(mem, mem_norm_g, w_mem_kv):
    depth = w_mem_kv.shape[0]
    b, m, d = mem.shape
    return pl.pallas_call(
        _mem_kv_kernel,
        grid=(depth, b),
        in_specs=[pl.BlockSpec((None, m, d), lambda l, i: (i, 0, 0)),
                  pl.BlockSpec((None, 1, d), lambda l, i: (l, 0, 0)),
                  pl.BlockSpec((None, d, 2 * XATTN_WIDTH), lambda l, i: (l, 0, 0))],
        out_specs=[pl.BlockSpec((None, None, XATTN_WIDTH, m), lambda l, i: (l, i, 0, 0)),
                   pl.BlockSpec((None, None, m, XATTN_WIDTH), lambda l, i: (l, i, 0, 0))],
        out_shape=[jax.ShapeDtypeStruct((depth, b, XATTN_WIDTH, m), BF16),
                   jax.ShapeDtypeStruct((depth, b, m, XATTN_WIDTH), BF16)],
        name="mem_kv",
    )(mem, mem_norm_g.reshape(depth, 1, d), w_mem_kv.astype(BF16))


def _retention_tables():
    c = RET_CHUNK
    log_gamma = np.log(1.0 - 2.0 ** (-5.0 - np.arange(RET_HEADS, dtype=np.float64)))
    idx = np.arange(c, dtype=np.float64)
    diff = idx[:, None] - idx[None, :]
    mask = np.where(diff[None] >= 0, np.exp(diff[None] * log_gamma[:, None, None]), 0.0)
    q_dec = np.exp((idx + 1.0)[None, :] * log_gamma[:, None])
    k_dec = np.exp((c - 1.0 - idx)[None, :] * log_gamma[:, None])
    q_dec = np.broadcast_to(q_dec[:, :, None], (RET_HEADS, c, RET_DK))
    k_dec = np.broadcast_to(k_dec[:, :, None], (RET_HEADS, c, RET_DK))
    chunk_decay = tuple(float(v) for v in np.exp(c * log_gamma))
    as_f32 = lambda a: jnp.asarray(np.ascontiguousarray(a), dtype=F32)
    return as_f32(mask), as_f32(q_dec), as_f32(k_dec), chunk_decay


def _mixer_kernel(chunk_decay, n_cast, *refs):
    (x_ref, cos_ref, sin_ref, kt_ref, vm_ref, mask_ref, qdec_ref, kdec_ref,
     gmix_ref, w_in_ref, pool_w_ref, pool_scale_ref, ret_g_ref,
     w_up_pool_ref, w_up_ret_ref, w_up_mem_ref, w_out_ref) = refs[:N_MIXER_INPUTS]
    cast_in = refs[N_MIXER_INPUTS:N_MIXER_INPUTS + n_cast]
    o_ref = refs[N_MIXER_INPUTS + n_cast]
    cast_out = refs[N_MIXER_INPUTS + n_cast + 1:N_MIXER_INPUTS + 2 * n_cast + 1]
    (h_ref, ubuf_ref, state_ref, qb_ref, kb_ref, kd_ref, lhs_ref, rhs_ref, gsilu_ref,
     ypool_ref, yret_ref, ymem_ref, macc_ref, merged_ref) = refs[N_MIXER_INPUTS + 2 * n_cast + 1:]
    _cast_blocks(cast_in, cast_out)
    tm = x_ref.shape[0]
    n_chunks = tm // RET_CHUNK
    seq_step = pl.program_id(1)

    @pl.when(seq_step == 0)
    def _():
        ubuf_ref[0:POOL_HALO, :] = jnp.zeros((POOL_HALO, POOL_WIDTH), F32)
        state_ref[...] = jnp.zeros_like(state_ref)

    h_ref[...] = _rms_norm(x_ref[...], gmix_ref[...]).astype(BF16)

    def proj(off, width):
        return _dot(h_ref[...], w_in_ref[:, off:off + width])

    def chunk_rows(c):
        return slice(c * RET_CHUNK, (c + 1) * RET_CHUNK)

    def merge_chunk(b, y_ref, w_up_ref, j):
        cols = slice(j * MERGE_CHUNK, (j + 1) * MERGE_CHUNK)
        g = jax.nn.sigmoid(proj(OFF_GATE + b * D_MODEL + j * MERGE_CHUNK, MERGE_CHUNK))
        return g * _dot(y_ref[...], w_up_ref[:, cols])


    lane = lax.broadcasted_iota(jnp.int32, cos_ref.shape, 1)
    first = lane < RET_DK // 2
    cos_p = cos_ref[...]
    sin_p = sin_ref[...]
    cos_s = pltpu.roll(cos_p, RET_DK // 2, 1)
    sin_s = pltpu.roll(sin_p, RET_DK // 2, 1)
    cos2 = jnp.concatenate([jnp.where(first, cos_p, cos_s), jnp.where(first, cos_s, cos_p)], axis=0)
    sin2 = jnp.concatenate([jnp.where(first, -sin_p, sin_s), jnp.where(first, -sin_s, sin_p)], axis=0)
    q_all = proj(OFF_RQ, RET_QK_WIDTH)
    k_all = proj(OFF_RK, RET_QK_WIDTH)
    for hh in range(RET_HEADS):
        qk = slice(hh * RET_DK, (hh + 1) * RET_DK)
        qh = q_all[:, qk]
        kh = k_all[:, qk]
        qr = qh * cos2 + pltpu.roll(qh, RET_DK // 2, 1) * sin2
        kr = (kh * cos2 + pltpu.roll(kh, RET_DK // 2, 1) * sin2) * (RET_DK ** -0.5)
        qb_ref[:, qk] = qr.astype(BF16)
        kb_ref[:, qk] = kr.astype(BF16)
        v_h = proj(OFF_RV + hh * RET_DV, RET_DV).astype(BF16)
        for c in range(n_chunks):
            rows = chunk_rows(c)
            lhs_ref[hh, rows, RET_CHUNK:] = (qr[rows] * qdec_ref[hh]).astype(BF16)
            kd_ref[rows, qk] = (kr[rows] * kdec_ref[hh]).astype(BF16)
            rhs_ref[hh * n_chunks + c, 0:RET_CHUNK, :] = v_h[rows]

    ubuf_ref[POOL_HALO:POOL_HALO + tm, :] = proj(OFF_POOL, POOL_WIDTH)
    row = lax.broadcasted_iota(jnp.int32, (tm, 1), 0) + seq_step * tm
    for gi, w in enumerate(POOL_WINDOWS):
        cols = slice(gi * POOL_GROUP_DIM, (gi + 1) * POOL_GROUP_DIM)
        win = ubuf_ref[:, cols]
        shift = 1
        while shift < w:
            win = win + pltpu.roll(win, shift, 0)
            shift *= 2
        cnt = jnp.minimum(row + 1, w).astype(F32)
        dwin = win[POOL_HALO:, :] / cnt - ubuf_ref[POOL_HALO:POOL_HALO + tm, cols]
        yg = _dot(dwin.astype(BF16), pool_w_ref[gi]) * pool_scale_ref[:, cols]
        ypool_ref[:, cols] = yg.astype(BF16)
        vv = slice(gi * RET_DV, (gi + 1) * RET_DV)
        g_h = proj(OFF_RG + gi * RET_DV, RET_DV)
        gsilu_ref[:, vv] = g_h * jax.nn.sigmoid(g_h) * ret_g_ref[:, vv]
    ubuf_ref[0:POOL_HALO, :] = ubuf_ref[tm:tm + POOL_HALO, :]

    for hh in range(RET_HEADS):
        qk = slice(hh * RET_DK, (hh + 1) * RET_DK)
        for c in range(n_chunks):
            rows = chunk_rows(c)
            scores = lax.dot_general(qb_ref[rows, qk], kb_ref[rows, qk],
                                     (((1,), (1,)), ((), ())),
                                     preferred_element_type=F32) * mask_ref[hh]
            lhs_ref[hh, rows, 0:RET_CHUNK] = scores.astype(BF16)

    for hh in range(RET_HEADS):
        qk = slice(hh * RET_DK, (hh + 1) * RET_DK)
        kvs = [lax.dot_general(kd_ref[chunk_rows(c), qk], rhs_ref[hh * n_chunks + c, 0:RET_CHUNK, :],
                               (((0,), (0,)), ((), ())), preferred_element_type=F32)
               for c in range(n_chunks)]
        s = state_ref[hh]
        for c in range(n_chunks):
            rhs_ref[hh * n_chunks + c, RET_CHUNK:, :] = s.astype(BF16)
            s = chunk_decay[hh] * s + kvs[c]
        state_ref[hh] = s

    xq = proj(OFF_XQ, XATTN_WIDTH).astype(BF16)

    for hh in range(RET_HEADS):
        vv = slice(hh * RET_DV, (hh + 1) * RET_DV)
        for c in range(n_chunks):
            rows = chunk_rows(c)
            y = _dot(lhs_ref[hh, rows, :], rhs_ref[hh * n_chunks + c])
            y = y * lax.rsqrt(jnp.mean(y * y, axis=-1, keepdims=True) + NORM_EPS)
            yret_ref[rows, vv] = (gsilu_ref[rows, vv] * y).astype(BF16)
        mc = slice(hh * MERGE_CHUNK, (hh + 1) * MERGE_CHUNK)
        macc_ref[:, mc] = merge_chunk(0, ypool_ref, w_up_pool_ref, hh)

    for hh in range(XATTN_HEADS):
        hd = slice(hh * XATTN_DH, (hh + 1) * XATTN_DH)
        s = _dot(xq[:, hd], kt_ref[hd, :])
        p = jnp.exp2((s - jnp.max(s, axis=-1, keepdims=True)) * SOFTMAX_EXP2_SCALE)
        p = p / jnp.sum(p, axis=-1, keepdims=True)
        ymem_ref[:, hd] = _dot(p.astype(BF16), vm_ref[:, hd]).astype(BF16)
        mc = slice(hh * MERGE_CHUNK, (hh + 1) * MERGE_CHUNK)
        macc_ref[:, mc] = macc_ref[:, mc] + merge_chunk(1, yret_ref, w_up_ret_ref, hh)

    for j in range(D_MODEL // MERGE_CHUNK):
        mc = slice(j * MERGE_CHUNK, (j + 1) * MERGE_CHUNK)
        merged_ref[:, mc] = (macc_ref[:, mc]
                             + merge_chunk(2, ymem_ref, w_up_mem_ref, j)).astype(BF16)
    o_ref[...] = x_ref[...] + _dot(merged_ref[...], w_out_ref[...])


def _mixer_layer(layer, x2d, batch, cos2, sin2, kt, vm, tables, gmix, w_in, pool_w, pool_scale,
                 ret_g, w_up_pool, w_up_ret, w_up_mem, w_out, cast_params=(), weight_layer=0):
    t, d = x2d.shape
    seq = t // batch
    tm = min(TOKEN_TILE, seq)
    steps = seq // tm
    mask, q_dec, k_dec, chunk_decay = tables
    tok = lambda b, s: (b * steps + s, 0)
    in_specs = [
        pl.BlockSpec((tm, d), tok),
        pl.BlockSpec((tm // 2, RET_DK), tok),
        pl.BlockSpec((tm // 2, RET_DK), tok),
        pl.BlockSpec((None, None, XATTN_WIDTH, N_MEM), lambda b, s: (layer, b, 0, 0)),
        pl.BlockSpec((None, None, N_MEM, XATTN_WIDTH), lambda b, s: (layer, b, 0, 0)),
        _const_spec(mask.shape), _const_spec(q_dec.shape), _const_spec(k_dec.shape),
    ] + [_layer_spec(p.shape, lyr) for p, lyr in
         ((gmix, layer), (w_in, weight_layer), (pool_w, weight_layer), (pool_scale, layer),
          (ret_g, layer), (w_up_pool, weight_layer), (w_up_ret, weight_layer),
          (w_up_mem, weight_layer), (w_out, weight_layer))]
    assert len(in_specs) == N_MIXER_INPUTS
    cast_views, cast_in_specs, cast_out_specs, cast_shapes = _cast_specs(
        cast_params, batch * steps, lambda b, s: b * steps + s)
    outs = pl.pallas_call(
        functools.partial(_mixer_kernel, chunk_decay, len(cast_params)),
        grid=(batch, steps),
        in_specs=in_specs + cast_in_specs,
        out_specs=[pl.BlockSpec((tm, d), tok)] + cast_out_specs,
        out_shape=[jax.ShapeDtypeStruct((t, d), F32)] + cast_shapes,
        scratch_shapes=[
            pltpu.VMEM((tm, d), BF16),
            pltpu.VMEM((POOL_HALO + tm, POOL_WIDTH), F32),
            pltpu.VMEM((RET_HEADS, RET_DK, RET_DV), F32),
            pltpu.VMEM((tm, RET_QK_WIDTH), BF16),
            pltpu.VMEM((tm, RET_QK_WIDTH), BF16),
            pltpu.VMEM((tm, RET_QK_WIDTH), BF16),
            pltpu.VMEM((RET_HEADS, tm, RET_CHUNK + RET_DK), BF16),
            pltpu.VMEM((RET_HEADS * (tm // RET_CHUNK), RET_CHUNK + RET_DK, RET_DV), BF16),
            pltpu.VMEM((tm, RET_V_WIDTH), F32),
            pltpu.VMEM((tm, POOL_WIDTH), BF16),
            pltpu.VMEM((tm, RET_V_WIDTH), BF16),
            pltpu.VMEM((tm, XATTN_WIDTH), BF16),
            pltpu.VMEM((tm, D_MODEL), F32),
            pltpu.VMEM((tm, D_MODEL), BF16),
        ],
        compiler_params=pltpu.CompilerParams(
            dimension_semantics=("arbitrary", "arbitrary"),
            vmem_limit_bytes=VMEM_LIMIT_BYTES),
        name="mixer",
    )(x2d, cos2, sin2, kt, vm, mask, q_dec, k_dec, gmix, w_in, pool_w, pool_scale, ret_g,
      w_up_pool, w_up_ret, w_up_mem, w_out, *cast_views)
    return outs[0], [o.reshape(p.shape) for o, p in zip(outs[1:], cast_params)]


def _mlp_kernel(apply_final_norm, n_cast, *refs):
    x_ref, g_ref, w1_ref, w2_ref, gf_ref = refs[:N_MLP_INPUTS]
    cast_in = refs[N_MLP_INPUTS:N_MLP_INPUTS + n_cast]
    o_ref = refs[N_MLP_INPUTS + n_cast]
    cast_out = refs[N_MLP_INPUTS + n_cast + 1:N_MLP_INPUTS + 2 * n_cast + 1]
    (hid_ref,) = refs[N_MLP_INPUTS + 2 * n_cast + 1:]
    _cast_blocks(cast_in, cast_out)
    x = x_ref[...]
    h = _rms_norm(x, g_ref[...]).astype(BF16)
    for j in range(D_FF // N_CHUNK):
        nc = slice(j * N_CHUNK, (j + 1) * N_CHUNK)
        a = jnp.maximum(_dot(h, w1_ref[:, nc]), 0.0)
        hid_ref[:, nc] = (a * a).astype(BF16)
    y = x + _dot(hid_ref[...], w2_ref[...])
    if apply_final_norm:
        y = _rms_norm(y, gf_ref[...])
    o_ref[...] = y


def _mlp_layer(layer, x2d, g, w1, w2, g_final, apply_final_norm, cast_params=(), cast_layer=None):
    t, d = x2d.shape
    tm = min(MLP_TOKEN_TILE, t)
    steps = t // tm
    cast_views, cast_in_specs, cast_out_specs, cast_shapes = _cast_specs(
        cast_params, steps, lambda i: i, layer=cast_layer)
    outs = pl.pallas_call(
        functools.partial(_mlp_kernel, apply_final_norm, len(cast_params)),
        grid=(steps,),
        in_specs=[pl.BlockSpec((tm, d), lambda i: (i, 0)),
                  _layer_spec(g.shape, layer), _layer_spec(w1.shape, layer),
                  _layer_spec(w2.shape, layer), _const_spec(g_final.shape)] + cast_in_specs,
        out_specs=[pl.BlockSpec((tm, d), lambda i: (i, 0))] + cast_out_specs,
        out_shape=[jax.ShapeDtypeStruct((t, d), F32)] + cast_shapes,
        scratch_shapes=[pltpu.VMEM((tm, D_FF), BF16)],
        compiler_params=pltpu.CompilerParams(
            dimension_semantics=("arbitrary",),
            vmem_limit_bytes=VMEM_LIMIT_BYTES),
        name="mlp",
    )(x2d, g, w1, w2, g_final, *cast_views)
    return outs[0], [o.reshape((1,) + p.shape[1:]) for o, p in zip(outs[1:], cast_params)]


def kernel(x, mem, positions, norm_mix_g, w_in, pool_w, pool_scale, ret_norm_g, mem_norm_g,
           w_mem_kv, w_up_pool, w_up_ret, w_up_mem, w_out, norm_mlp_g, w_mlp1, w_mlp2,
           final_norm_g):
    batch, seq, d = x.shape
    depth = w_in.shape[0]
    assert d == D_MODEL and w_in.shape[-1] == IN_COLS and mem.shape[1] == N_MEM
    assert seq % RET_CHUNK == 0 and seq % min(TOKEN_TILE, seq) == 0

    cos2, sin2 = _rope_tables(positions, min(TOKEN_TILE, seq))
    kt, vm = _mem_kv(mem, mem_norm_g, w_mem_kv)
    tables = _retention_tables()
    rows = lambda a: a.reshape(depth, 1, -1)
    g_final = final_norm_g.reshape(1, -1)
    gains = (rows(norm_mix_g), rows(pool_scale), rows(ret_norm_g))
    mixer_w_f32 = (w_in, pool_w, w_up_pool, w_up_ret, w_up_mem, w_out)
    mixer_w = [w[0:1].astype(BF16) for w in mixer_w_f32]
    mixer_w_layer = 0

    x2d = x.reshape(batch * seq, d)
    for l in range(depth):
        w_in_b, pool_w_b, w_up_pool_b, w_up_ret_b, w_up_mem_b, w_out_b = mixer_w
        x2d, cast = _mixer_layer(
            l, x2d, batch, cos2, sin2, kt, vm, tables, gains[0], w_in_b, pool_w_b, gains[1],
            gains[2], w_up_pool_b, w_up_ret_b, w_up_mem_b, w_out_b,
            cast_params=(w_mlp1, w_mlp2) if l == 0 else (), weight_layer=mixer_w_layer)
        if l == 0:
            mlp_w = cast
        last = l == depth - 1
        x2d, mixer_w = _mlp_layer(
            l, x2d, rows(norm_mlp_g), mlp_w[0], mlp_w[1], g_final, apply_final_norm=last,
            cast_params=() if last else mixer_w_f32, cast_layer=None if last else l + 1)
    return x2d.reshape(batch, seq, d)
```

```python
import functools

import numpy as np
import jax
import jax.numpy as jnp
from jax import lax
from jax.experimental import pallas as pl
from jax.experimental.pallas import tpu as pltpu

D_MODEL = 1024
N_MEM = 256
POOL_WINDOWS = (2, 4, 8, 16)
POOL_GROUPS = 4
POOL_GROUP_DIM = 128
POOL_WIDTH = POOL_GROUPS * POOL_GROUP_DIM
POOL_HALO = 16
RET_HEADS = 4
RET_DK = 128
RET_DV = 256
RET_QK_WIDTH = RET_HEADS * RET_DK
RET_V_WIDTH = RET_HEADS * RET_DV
RET_CHUNK = 128
ROPE_BASE = 10000.0
XATTN_HEADS = 4
XATTN_DH = 128
XATTN_WIDTH = XATTN_HEADS * XATTN_DH
D_FF = 4 * D_MODEL
NORM_EPS = 1e-6

OFF_POOL = 0
OFF_RQ = OFF_POOL + POOL_WIDTH
OFF_RK = OFF_RQ + RET_QK_WIDTH
OFF_RV = OFF_RK + RET_QK_WIDTH
OFF_RG = OFF_RV + RET_V_WIDTH
OFF_XQ = OFF_RG + RET_V_WIDTH
OFF_GATE = OFF_XQ + XATTN_WIDTH
IN_COLS = OFF_GATE + 3 * D_MODEL

TOKEN_TILE = 512
MLP_TOKEN_TILE = 1024
ROPE_TILE = 1024
N_CHUNK = 512
MERGE_CHUNK = 256
VMEM_LIMIT_BYTES = 56 * 1024 * 1024
BF16_SUBLANES = 16
N_MIXER_INPUTS = 17
N_MLP_INPUTS = 5

SOFTMAX_EXP2_SCALE = float(XATTN_DH ** -0.5 * np.log2(np.e))

BF16 = jnp.bfloat16
F32 = jnp.float32


def _dot(a, b):
    return jnp.dot(a, b, preferred_element_type=F32)


def _rms_norm(x, g):
    return x * lax.rsqrt(jnp.mean(x * x, axis=-1, keepdims=True) + NORM_EPS) * g


def _const_spec(shape):
    nd = len(shape)
    return pl.BlockSpec(shape, lambda *_: (0,) * nd, pipeline_mode=pl.Buffered(1))


def _layer_spec(stacked_shape, layer):
    nd = len(stacked_shape) - 1
    return pl.BlockSpec((None,) + tuple(stacked_shape[1:]), lambda *_: (layer,) + (0,) * nd,
                        pipeline_mode=pl.Buffered(1))


def _cast_specs(params, n_steps, index_of_step, layer=None):
    views, in_specs, out_specs, out_shapes = [], [], [], []
    for p in params:
        cols = p.shape[-1]
        rows_per_layer = int(np.prod(p.shape[1:-1]))
        rows = rows_per_layer if layer is not None else p.shape[0] * rows_per_layer
        slab, rem = divmod(rows, n_steps)
        assert rem == 0 and slab % BF16_SUBLANES == 0, (p.shape, n_steps)
        first = 0 if layer is None else layer * n_steps
        views.append(p.reshape(-1, cols))
        in_specs.append(pl.BlockSpec((slab, cols), lambda *g, f=first: (f + index_of_step(*g), 0)))
        out_specs.append(pl.BlockSpec((slab, cols), lambda *g: (index_of_step(*g), 0)))
        out_shapes.append(jax.ShapeDtypeStruct((rows, cols), BF16))
    return views, in_specs, out_specs, out_shapes


def _cast_blocks(in_refs, out_refs):
    for src_ref, dst_ref in zip(in_refs, out_refs):
        dst_ref[...] = src_ref[...].astype(BF16)


def _rope_kernel(n_cast, pos_ref, freq_ref, *refs):
    cast_in = refs[:n_cast]
    cos_ref, sin_ref = refs[n_cast:n_cast + 2]
    _cast_blocks(cast_in, refs[n_cast + 2:])
    lane = lax.broadcasted_iota(jnp.int32, cos_ref.shape, 1)
    pos = pos_ref[...]
    ang = jnp.where(lane < RET_DK // 2, pos[:, 0:1], pos[:, 1:2]) * freq_ref[...]
    cos_ref[...] = jnp.cos(ang)
    sin_ref[...] = jnp.sin(ang)


def _rope_tables(positions, token_tile, cast_params=(), cast_layer=0):
    t = positions.size
    half_tile = token_tile // 2
    half = RET_DK // 2
    inv_freq = ROPE_BASE ** (-jnp.arange(half, dtype=F32) / half)
    freq2 = jnp.concatenate([inv_freq, inv_freq]).reshape(1, RET_DK)
    pos = positions.astype(F32).reshape(t // token_tile, 2, half_tile)
    pos = jnp.swapaxes(pos, 1, 2).reshape(t // 2, 2)
    rows = min(ROPE_TILE, t // 2)
    steps = t // 2 // rows
    cast_views, cast_in_specs, cast_out_specs, cast_shapes = _cast_specs(
        cast_params, steps, lambda i: i, layer=cast_layer)
    outs = pl.pallas_call(
        functools.partial(_rope_kernel, len(cast_params)),
        grid=(steps,),
        in_specs=[pl.BlockSpec((rows, 2), lambda i: (i, 0)),
                  pl.BlockSpec((1, RET_DK), lambda i: (0, 0))] + cast_in_specs,
        out_specs=[pl.BlockSpec((rows, RET_DK), lambda i: (i, 0))] * 2 + cast_out_specs,
        out_shape=[jax.ShapeDtypeStruct((t // 2, RET_DK), F32)] * 2 + cast_shapes,
        name="rope_tables",
    )(pos, freq2, *cast_views)
    return outs[0], outs[1], [o.reshape((1,) + p.shape[1:]) for o, p in zip(outs[2:], cast_params)]


def _mem_kv_kernel(mem_ref, g_ref, w_ref, kt_ref, v_ref):
    mem_n = _rms_norm(mem_ref[...], g_ref[...]).astype(BF16)
    kv = _dot(mem_n, w_ref[...])
    kt_ref[...] = kv[:, :XATTN_WIDTH].T.astype(BF16)
    v_ref[...] = kv[:, XATTN_WIDTH:].astype(BF16)


def _mem_kv(mem, mem_norm_g, w_mem_kv):
    depth = w_mem_kv.shape[0]
    b, m, d = mem.shape
    return pl.pallas_call(
        _mem_kv_kernel,
        grid=(depth, b),
        in_specs=[pl.BlockSpec((None, m, d), lambda l, i: (i, 0, 0)),
                  pl.BlockSpec((None, 1, d), lambda l, i: (l, 0, 0)),
                  pl.BlockSpec((None, d, 2 * XATTN_WIDTH), lambda l, i: (l, 0, 0))],
        out_specs=[pl.BlockSpec((None, None, XATTN_WIDTH, m), lambda l, i: (l, i, 0, 0)),
                   pl.BlockSpec((None, None, m, XATTN_WIDTH), lambda l, i: (l, i, 0, 0))],
        out_shape=[jax.ShapeDtypeStruct((depth, b, XATTN_WIDTH, m), BF16),
                   jax.ShapeDtypeStruct((depth, b, m, XATTN_WIDTH), BF16)],
        name="mem_kv",
    )(mem, mem_norm_g.reshape(depth, 1, d), w_mem_kv.astype(BF16))


def _retention_tables():
    c = RET_CHUNK
    log_gamma = np.log(1.0 - 2.0 ** (-5.0 - np.arange(RET_HEADS, dtype=np.float64)))
    idx = np.arange(c, dtype=np.float64)
    diff = idx[:, None] - idx[None, :]
    mask = np.where(diff[None] >= 0, np.exp(diff[None] * log_gamma[:, None, None]), 0.0)
    q_dec = np.exp((idx + 1.0)[None, :] * log_gamma[:, None])
    k_dec = np.exp((c - 1.0 - idx)[None, :] * log_gamma[:, None])
    q_dec = np.broadcast_to(q_dec[:, :, None], (RET_HEADS, c, RET_DK))
    k_dec = np.broadcast_to(k_dec[:, :, None], (RET_HEADS, c, RET_DK))
    chunk_decay = tuple(float(v) for v in np.exp(c * log_gamma))
    as_f32 = lambda a: jnp.asarray(np.ascontiguousarray(a), dtype=F32)
    return as_f32(mask), as_f32(q_dec), as_f32(k_dec), chunk_decay


def _mixer_kernel(chunk_decay, layer, n_cast, *refs):
    (x_ref, cos_ref, sin_ref, kt_ref, vm_ref, mask_ref, qdec_ref, kdec_ref,
     gmix_ref, w_in_ref, pool_w_ref, pool_scale_ref, ret_g_ref,
     w_up_pool_ref, w_up_ret_ref, w_up_mem_ref, w_out_ref) = refs[:N_MIXER_INPUTS]
    cast_in = refs[N_MIXER_INPUTS:N_MIXER_INPUTS + n_cast]
    o_ref = refs[N_MIXER_INPUTS + n_cast]
    cast_out = refs[N_MIXER_INPUTS + n_cast + 1:N_MIXER_INPUTS + 2 * n_cast + 1]
    (h_ref, ubuf_ref, state_ref, qb_ref, kb_ref, kd_ref, lhs_ref, rhs_ref, gsilu_ref,
     ypool_ref, yret_ref, ymem_ref, macc_ref, merged_ref) = refs[N_MIXER_INPUTS + 2 * n_cast + 1:]
    _cast_blocks(cast_in, cast_out)
    gmix = gmix_ref[layer:layer + 1, :]
    pool_scale = pool_scale_ref[layer:layer + 1, :]
    ret_g = ret_g_ref[layer:layer + 1, :]
    tm = x_ref.shape[0]
    n_chunks = tm // RET_CHUNK
    seq_step = pl.program_id(1)

    @pl.when(seq_step == 0)
    def _():
        ubuf_ref[0:POOL_HALO, :] = jnp.zeros((POOL_HALO, POOL_WIDTH), F32)
        state_ref[...] = jnp.zeros_like(state_ref)

    h_ref[...] = _rms_norm(x_ref[...], gmix).astype(BF16)

    def proj(off, width):
        return _dot(h_ref[...], w_in_ref[:, off:off + width])

    def chunk_rows(c):
        return slice(c * RET_CHUNK, (c + 1) * RET_CHUNK)

    def merge_chunk(b, y_ref, w_up_ref, j):
        cols = slice(j * MERGE_CHUNK, (j + 1) * MERGE_CHUNK)
        g = jax.nn.sigmoid(proj(OFF_GATE + b * D_MODEL + j * MERGE_CHUNK, MERGE_CHUNK))
        return g * _dot(y_ref[...], w_up_ref[:, cols])


    lane = lax.broadcasted_iota(jnp.int32, cos_ref.shape, 1)
    first = lane < RET_DK // 2
    cos_p = cos_ref[...]
    sin_p = sin_ref[...]
    cos_s = pltpu.roll(cos_p, RET_DK // 2, 1)
    sin_s = pltpu.roll(sin_p, RET_DK // 2, 1)
    cos2 = jnp.concatenate([jnp.where(first, cos_p, cos_s), jnp.where(first, cos_s, cos_p)], axis=0)
    sin2 = jnp.concatenate([jnp.where(first, -sin_p, sin_s), jnp.where(first, -sin_s, sin_p)], axis=0)
    q_all = proj(OFF_RQ, RET_QK_WIDTH)
    k_all = proj(OFF_RK, RET_QK_WIDTH)
    for hh in range(RET_HEADS):
        qk = slice(hh * RET_DK, (hh + 1) * RET_DK)
        qh = q_all[:, qk]
        kh = k_all[:, qk]
        qr = qh * cos2 + pltpu.roll(qh, RET_DK // 2, 1) * sin2
        kr = (kh * cos2 + pltpu.roll(kh, RET_DK // 2, 1) * sin2) * (RET_DK ** -0.5)
        qb_ref[:, qk] = qr.astype(BF16)
        kb_ref[:, qk] = kr.astype(BF16)
        v_h = proj(OFF_RV + hh * RET_DV, RET_DV).astype(BF16)
        for c in range(n_chunks):
            rows = chunk_rows(c)
            lhs_ref[hh, rows, RET_CHUNK:] = (qr[rows] * qdec_ref[hh]).astype(BF16)
            kd_ref[rows, qk] = (kr[rows] * kdec_ref[hh]).astype(BF16)
            rhs_ref[hh * n_chunks + c, 0:RET_CHUNK, :] = v_h[rows]

    ubuf_ref[POOL_HALO:POOL_HALO + tm, :] = proj(OFF_POOL, POOL_WIDTH)
    row = lax.broadcasted_iota(jnp.int32, (tm, 1), 0) + seq_step * tm
    for gi, w in enumerate(POOL_WINDOWS):
        cols = slice(gi * POOL_GROUP_DIM, (gi + 1) * POOL_GROUP_DIM)
        win = ubuf_ref[:, cols]
        shift = 1
        while shift < w:
            win = win + pltpu.roll(win, shift, 0)
            shift *= 2
        cnt = jnp.minimum(row + 1, w).astype(F32)
        dwin = win[POOL_HALO:, :] / cnt - ubuf_ref[POOL_HALO:POOL_HALO + tm, cols]
        yg = _dot(dwin.astype(BF16), pool_w_ref[gi]) * pool_scale[:, cols]
        ypool_ref[:, cols] = yg.astype(BF16)
        vv = slice(gi * RET_DV, (gi + 1) * RET_DV)
        g_h = proj(OFF_RG + gi * RET_DV, RET_DV)
        gsilu_ref[:, vv] = g_h * jax.nn.sigmoid(g_h) * ret_g[:, vv]
    ubuf_ref[0:POOL_HALO, :] = ubuf_ref[tm:tm + POOL_HALO, :]

    for hh in range(RET_HEADS):
        qk = slice(hh * RET_DK, (hh + 1) * RET_DK)
        for c in range(n_chunks):
            rows = chunk_rows(c)
            scores = lax.dot_general(qb_ref[rows, qk], kb_ref[rows, qk],
                                     (((1,), (1,)), ((), ())),
                                     preferred_element_type=F32) * mask_ref[hh]
            lhs_ref[hh, rows, 0:RET_CHUNK] = scores.astype(BF16)

    for hh in range(RET_HEADS):
        qk = slice(hh * RET_DK, (hh + 1) * RET_DK)
        kvs = [lax.dot_general(kd_ref[chunk_rows(c), qk], rhs_ref[hh * n_chunks + c, 0:RET_CHUNK, :],
                               (((0,), (0,)), ((), ())), preferred_element_type=F32)
               for c in range(n_chunks)]
        s = state_ref[hh]
        for c in range(n_chunks):
            rhs_ref[hh * n_chunks + c, RET_CHUNK:, :] = s.astype(BF16)
            s = chunk_decay[hh] * s + kvs[c]
        state_ref[hh] = s

    xq = proj(OFF_XQ, XATTN_WIDTH).astype(BF16)

    for hh in range(RET_HEADS):
        vv = slice(hh * RET_DV, (hh + 1) * RET_DV)
        for c in range(n_chunks):
            rows = chunk_rows(c)
            y = _dot(lhs_ref[hh, rows, :], rhs_ref[hh * n_chunks + c])
            y = y * lax.rsqrt(jnp.mean(y * y, axis=-1, keepdims=True) + NORM_EPS)
            yret_ref[rows, vv] = (gsilu_ref[rows, vv] * y).astype(BF16)
        mc = slice(hh * MERGE_CHUNK, (hh + 1) * MERGE_CHUNK)
        macc_ref[:, mc] = merge_chunk(0, ypool_ref, w_up_pool_ref, hh)

    for hh in range(XATTN_HEADS):
        hd = slice(hh * XATTN_DH, (hh + 1) * XATTN_DH)
        s = _dot(xq[:, hd], kt_ref[hd, :])
        p = jnp.exp2((s - jnp.max(s, axis=-1, keepdims=True)) * SOFTMAX_EXP2_SCALE)
        p = p / jnp.sum(p, axis=-1, keepdims=True)
        ymem_ref[:, hd] = _dot(p.astype(BF16), vm_ref[:, hd]).astype(BF16)
        mc = slice(hh * MERGE_CHUNK, (hh + 1) * MERGE_CHUNK)
        macc_ref[:, mc] = macc_ref[:, mc] + merge_chunk(1, yret_ref, w_up_ret_ref, hh)

    for j in range(D_MODEL // MERGE_CHUNK):
        mc = slice(j * MERGE_CHUNK, (j + 1) * MERGE_CHUNK)
        merged_ref[:, mc] = (macc_ref[:, mc]
                             + merge_chunk(2, ymem_ref, w_up_mem_ref, j)).astype(BF16)
    o_ref[...] = x_ref[...] + _dot(merged_ref[...], w_out_ref[...])


def _mixer_layer(layer, x2d, batch, cos2, sin2, kt, vm, tables, gmix, w_in, pool_w, pool_scale,
                 ret_g, w_up_pool, w_up_ret, w_up_mem, w_out, cast_params=(), weight_layer=0):
    t, d = x2d.shape
    seq = t // batch
    tm = min(TOKEN_TILE, seq)
    steps = seq // tm
    mask, q_dec, k_dec, chunk_decay = tables
    tok = lambda b, s: (b * steps + s, 0)
    in_specs = [
        pl.BlockSpec((tm, d), tok),
        pl.BlockSpec((tm // 2, RET_DK), tok),
        pl.BlockSpec((tm // 2, RET_DK), tok),
        pl.BlockSpec((None, None, XATTN_WIDTH, N_MEM), lambda b, s: (layer, b, 0, 0)),
        pl.BlockSpec((None, None, N_MEM, XATTN_WIDTH), lambda b, s: (layer, b, 0, 0)),
        _const_spec(mask.shape), _const_spec(q_dec.shape), _const_spec(k_dec.shape),
        _const_spec(gmix.shape), _layer_spec(w_in.shape, weight_layer),
        _layer_spec(pool_w.shape, weight_layer), _const_spec(pool_scale.shape),
        _const_spec(ret_g.shape),
    ] + [_layer_spec(w.shape, weight_layer) for w in (w_up_pool, w_up_ret, w_up_mem, w_out)]
    assert len(in_specs) == N_MIXER_INPUTS
    cast_views, cast_in_specs, cast_out_specs, cast_shapes = _cast_specs(
        cast_params, batch * steps, lambda b, s: b * steps + s)
    outs = pl.pallas_call(
        functools.partial(_mixer_kernel, chunk_decay, layer, len(cast_params)),
        grid=(batch, steps),
        in_specs=in_specs + cast_in_specs,
        out_specs=[pl.BlockSpec((tm, d), tok)] + cast_out_specs,
        out_shape=[jax.ShapeDtypeStruct((t, d), F32)] + cast_shapes,
        scratch_shapes=[
            pltpu.VMEM((tm, d), BF16),
            pltpu.VMEM((POOL_HALO + tm, POOL_WIDTH), F32),
            pltpu.VMEM((RET_HEADS, RET_DK, RET_DV), F32),
            pltpu.VMEM((tm, RET_QK_WIDTH), BF16),
            pltpu.VMEM((tm, RET_QK_WIDTH), BF16),
            pltpu.VMEM((tm, RET_QK_WIDTH), BF16),
            pltpu.VMEM((RET_HEADS, tm, RET_CHUNK + RET_DK), BF16),
            pltpu.VMEM((RET_HEADS * (tm // RET_CHUNK), RET_CHUNK + RET_DK, RET_DV), BF16),
            pltpu.VMEM((tm, RET_V_WIDTH), F32),
            pltpu.VMEM((tm, POOL_WIDTH), BF16),
            pltpu.VMEM((tm, RET_V_WIDTH), BF16),
            pltpu.VMEM((tm, XATTN_WIDTH), BF16),
            pltpu.VMEM((tm, D_MODEL), F32),
            pltpu.VMEM((tm, D_MODEL), BF16),
        ],
        compiler_params=pltpu.CompilerParams(
            dimension_semantics=("arbitrary", "arbitrary"),
            vmem_limit_bytes=VMEM_LIMIT_BYTES),
        name="mixer",
    )(x2d, cos2, sin2, kt, vm, mask, q_dec, k_dec, gmix, w_in, pool_w, pool_scale, ret_g,
      w_up_pool, w_up_ret, w_up_mem, w_out, *cast_views)
    return outs[0], [o.reshape(p.shape) for o, p in zip(outs[1:], cast_params)]


def _mlp_kernel(apply_final_norm, layer, n_cast, *refs):
    x_ref, g_ref, w1_ref, w2_ref, gf_ref = refs[:N_MLP_INPUTS]
    cast_in = refs[N_MLP_INPUTS:N_MLP_INPUTS + n_cast]
    o_ref = refs[N_MLP_INPUTS + n_cast]
    cast_out = refs[N_MLP_INPUTS + n_cast + 1:N_MLP_INPUTS + 2 * n_cast + 1]
    (hid_ref,) = refs[N_MLP_INPUTS + 2 * n_cast + 1:]
    _cast_blocks(cast_in, cast_out)
    x = x_ref[...]
    h = _rms_norm(x, g_ref[layer:layer + 1, :]).astype(BF16)
    for j in range(D_FF // N_CHUNK):
        nc = slice(j * N_CHUNK, (j + 1) * N_CHUNK)
        a = jnp.maximum(_dot(h, w1_ref[:, nc]), 0.0)
        hid_ref[:, nc] = (a * a).astype(BF16)
    y = x + _dot(hid_ref[...], w2_ref[...])
    if apply_final_norm:
        y = _rms_norm(y, gf_ref[...])
    o_ref[...] = y


def _mlp_layer(layer, x2d, g, w1, w2, g_final, apply_final_norm, cast_params=(), cast_layer=None):
    t, d = x2d.shape
    tm = min(MLP_TOKEN_TILE, t)
    steps = t // tm
    cast_views, cast_in_specs, cast_out_specs, cast_shapes = _cast_specs(
        cast_params, steps, lambda i: i, layer=cast_layer)
    outs = pl.pallas_call(
        functools.partial(_mlp_kernel, apply_final_norm, layer, len(cast_params)),
        grid=(steps,),
        in_specs=[pl.BlockSpec((tm, d), lambda i: (i, 0)),
                  _const_spec(g.shape), _layer_spec(w1.shape, layer),
                  _layer_spec(w2.shape, layer), _const_spec(g_final.shape)] + cast_in_specs,
        out_specs=[pl.BlockSpec((tm, d), lambda i: (i, 0))] + cast_out_specs,
        out_shape=[jax.ShapeDtypeStruct((t, d), F32)] + cast_shapes,
        scratch_shapes=[pltpu.VMEM((tm, D_FF), BF16)],
        compiler_params=pltpu.CompilerParams(
            dimension_semantics=("arbitrary",),
            vmem_limit_bytes=VMEM_LIMIT_BYTES),
        name="mlp",
    )(x2d, g, w1, w2, g_final, *cast_views)
    return outs[0], [o.reshape((1,) + p.shape[1:]) for o, p in zip(outs[1:], cast_params)]


def kernel(x, mem, positions, norm_mix_g, w_in, pool_w, pool_scale, ret_norm_g, mem_norm_g,
           w_mem_kv, w_up_pool, w_up_ret, w_up_mem, w_out, norm_mlp_g, w_mlp1, w_mlp2,
           final_norm_g):
    batch, seq, d = x.shape
    depth = w_in.shape[0]
    assert d == D_MODEL and w_in.shape[-1] == IN_COLS and mem.shape[1] == N_MEM
    assert seq % RET_CHUNK == 0 and seq % min(TOKEN_TILE, seq) == 0

    mixer_w_f32 = (w_in, pool_w, w_up_pool, w_up_ret, w_up_mem, w_out)
    cos2, sin2, mixer_w = _rope_tables(positions, min(TOKEN_TILE, seq), mixer_w_f32, cast_layer=0)
    kt, vm = _mem_kv(mem, mem_norm_g, w_mem_kv)
    tables = _retention_tables()
    g_final = final_norm_g.reshape(1, -1)

    x2d = x.reshape(batch * seq, d)
    for l in range(depth):
        w_in_b, pool_w_b, w_up_pool_b, w_up_ret_b, w_up_mem_b, w_out_b = mixer_w
        x2d, cast = _mixer_layer(
            l, x2d, batch, cos2, sin2, kt, vm, tables, norm_mix_g, w_in_b, pool_w_b, pool_scale,
            ret_norm_g, w_up_pool_b, w_up_ret_b, w_up_mem_b, w_out_b,
            cast_params=(w_mlp1, w_mlp2) if l == 0 else (), weight_layer=0)
        if l == 0:
            mlp_w = cast
        last = l == depth - 1
        x2d, mixer_w = _mlp_layer(
            l, x2d, norm_mlp_g, mlp_w[0], mlp_w[1], g_final, apply_final_norm=last,
            cast_params=() if last else mixer_w_f32, cast_layer=None if last else l + 1)
    return x2d.reshape(batch, seq, d)
```

```python
import functools

import numpy as np
import jax
import jax.numpy as jnp
from jax import lax
from jax.experimental import pallas as pl
from jax.experimental.pallas import tpu as pltpu

D_MODEL = 1024
N_MEM = 256
POOL_WINDOWS = (2, 4, 8, 16)
POOL_GROUPS = 4
POOL_GROUP_DIM = 128
POOL_WIDTH = POOL_GROUPS * POOL_GROUP_DIM
POOL_HALO = 16
RET_HEADS = 4
RET_DK = 128
RET_DV = 256
RET_QK_WIDTH = RET_HEADS * RET_DK
RET_V_WIDTH = RET_HEADS * RET_DV
RET_CHUNK = 128
ROPE_BASE = 10000.0
XATTN_HEADS = 4
XATTN_DH = 128
XATTN_WIDTH = XATTN_HEADS * XATTN_DH
D_FF = 4 * D_MODEL
NORM_EPS = 1e-6

OFF_POOL = 0
OFF_RQ = OFF_POOL + POOL_WIDTH
OFF_RK = OFF_RQ + RET_QK_WIDTH
OFF_RV = OFF_RK + RET_QK_WIDTH
OFF_RG = OFF_RV + RET_V_WIDTH
OFF_XQ = OFF_RG + RET_V_WIDTH
OFF_GATE = OFF_XQ + XATTN_WIDTH
IN_COLS = OFF_GATE + 3 * D_MODEL

TOKEN_TILE = 512
MIXER_SUBTILES = 2
MLP_TOKEN_TILE = 1024
ROPE_TILE = 1024
N_CHUNK = 512
MERGE_CHUNK = 256
VMEM_LIMIT_BYTES = 60 * 1024 * 1024
BF16_SUBLANES = 16
N_MIXER_INPUTS = 17
N_MLP_INPUTS = 5

SOFTMAX_EXP2_SCALE = float(XATTN_DH ** -0.5 * np.log2(np.e))

BF16 = jnp.bfloat16
F32 = jnp.float32


def _dot(a, b):
    return jnp.dot(a, b, preferred_element_type=F32)


def _rms_norm(x, g):
    return x * lax.rsqrt(jnp.mean(x * x, axis=-1, keepdims=True) + NORM_EPS) * g


def _const_spec(shape):
    nd = len(shape)
    return pl.BlockSpec(shape, lambda *_: (0,) * nd, pipeline_mode=pl.Buffered(1))


def _layer_spec(stacked_shape, layer):
    nd = len(stacked_shape) - 1
    return pl.BlockSpec((None,) + tuple(stacked_shape[1:]), lambda *_: (layer,) + (0,) * nd,
                        pipeline_mode=pl.Buffered(1))


def _cast_specs(params, n_steps, index_of_step, layer=None):
    views, in_specs, out_specs, out_shapes = [], [], [], []
    for p in params:
        cols = p.shape[-1]
        rows_per_layer = int(np.prod(p.shape[1:-1]))
        rows = rows_per_layer if layer is not None else p.shape[0] * rows_per_layer
        slab, rem = divmod(rows, n_steps)
        assert rem == 0 and slab % BF16_SUBLANES == 0, (p.shape, n_steps)
        first = 0 if layer is None else layer * n_steps
        views.append(p.reshape(-1, cols))
        in_specs.append(pl.BlockSpec((slab, cols), lambda *g, f=first: (f + index_of_step(*g), 0)))
        out_specs.append(pl.BlockSpec((slab, cols), lambda *g: (index_of_step(*g), 0)))
        out_shapes.append(jax.ShapeDtypeStruct((rows, cols), BF16))
    return views, in_specs, out_specs, out_shapes


def _cast_blocks(in_refs, out_refs):
    for src_ref, dst_ref in zip(in_refs, out_refs):
        dst_ref[...] = src_ref[...].astype(BF16)


def _rope_kernel(n_cast, pos_ref, freq_ref, *refs):
    cast_in = refs[:n_cast]
    cos_ref, sin_ref = refs[n_cast:n_cast + 2]
    _cast_blocks(cast_in, refs[n_cast + 2:])
    lane = lax.broadcasted_iota(jnp.int32, cos_ref.shape, 1)
    pos = pos_ref[...]
    ang = jnp.where(lane < RET_DK // 2, pos[:, 0:1], pos[:, 1:2]) * freq_ref[...]
    cos_ref[...] = jnp.cos(ang)
    sin_ref[...] = jnp.sin(ang)


def _rope_tables(positions, token_tile, cast_params=(), cast_layer=0):
    t = positions.size
    half_tile = token_tile // 2
    half = RET_DK // 2
    inv_freq = ROPE_BASE ** (-jnp.arange(half, dtype=F32) / half)
    freq2 = jnp.concatenate([inv_freq, inv_freq]).reshape(1, RET_DK)
    pos = positions.astype(F32).reshape(t // token_tile, 2, half_tile)
    pos = jnp.swapaxes(pos, 1, 2).reshape(t // 2, 2)
    rows = min(ROPE_TILE, t // 2)
    steps = t // 2 // rows
    cast_views, cast_in_specs, cast_out_specs, cast_shapes = _cast_specs(
        cast_params, steps, lambda i: i, layer=cast_layer)
    outs = pl.pallas_call(
        functools.partial(_rope_kernel, len(cast_params)),
        grid=(steps,),
        in_specs=[pl.BlockSpec((rows, 2), lambda i: (i, 0)),
                  pl.BlockSpec((1, RET_DK), lambda i: (0, 0))] + cast_in_specs,
        out_specs=[pl.BlockSpec((rows, RET_DK), lambda i: (i, 0))] * 2 + cast_out_specs,
        out_shape=[jax.ShapeDtypeStruct((t // 2, RET_DK), F32)] * 2 + cast_shapes,
        name="rope_tables",
    )(pos, freq2, *cast_views)
    return outs[0], outs[1], [o.reshape((1,) + p.shape[1:]) for o, p in zip(outs[2:], cast_params)]


def _mem_kv_kernel(mem_ref, g_ref, w_ref, kt_ref, v_ref):
    mem_n = _rms_norm(mem_ref[...], g_ref[...]).astype(BF16)
    kv = _dot(mem_n, w_ref[...])
    kt_ref[...] = kv[:, :XATTN_WIDTH].T.astype(BF16)
    v_ref[...] = kv[:, XATTN_WIDTH:].astype(BF16)


def _mem_kv(mem, mem_norm_g, w_mem_kv):
    depth = w_mem_kv.shape[0]
    b, m, d = mem.shape
    return pl.pallas_call(
        _mem_kv_kernel,
        grid=(depth, b),
        in_specs=[pl.BlockSpec((None, m, d), lambda l, i: (i, 0, 0)),
                  pl.BlockSpec((None, 1, d), lambda l, i: (l, 0, 0)),
                  pl.BlockSpec((None, d, 2 * XATTN_WIDTH), lambda l, i: (l, 0, 0))],
        out_specs=[pl.BlockSpec((None, None, XATTN_WIDTH, m), lambda l, i: (l, i, 0, 0)),
                   pl.BlockSpec((None, None, m, XATTN_WIDTH), lambda l, i: (l, i, 0, 0))],
        out_shape=[jax.ShapeDtypeStruct((depth, b, XATTN_WIDTH, m), BF16),
                   jax.ShapeDtypeStruct((depth, b, m, XATTN_WIDTH), BF16)],
        name="mem_kv",
    )(mem, mem_norm_g.reshape(depth, 1, d), w_mem_kv.astype(BF16))


def _retention_tables():
    c = RET_CHUNK
    log_gamma = np.log(1.0 - 2.0 ** (-5.0 - np.arange(RET_HEADS, dtype=np.float64)))
    idx = np.arange(c, dtype=np.float64)
    diff = idx[:, None] - idx[None, :]
    mask = np.where(diff[None] >= 0, np.exp(diff[None] * log_gamma[:, None, None]), 0.0)
    q_dec = np.exp((idx + 1.0)[None, :] * log_gamma[:, None])
    k_dec = np.exp((c - 1.0 - idx)[None, :] * log_gamma[:, None])
    q_dec = np.broadcast_to(q_dec[:, :, None], (RET_HEADS, c, RET_DK))
    k_dec = np.broadcast_to(k_dec[:, :, None], (RET_HEADS, c, RET_DK))
    chunk_decay = tuple(float(v) for v in np.exp(c * log_gamma))
    as_f32 = lambda a: jnp.asarray(np.ascontiguousarray(a), dtype=F32)
    return as_f32(mask), as_f32(q_dec), as_f32(k_dec), chunk_decay


def _mixer_kernel(chunk_decay, layer, n_sub, n_cast, *refs):
    (x_ref, cos_ref, sin_ref, kt_ref, vm_ref, mask_ref, qdec_ref, kdec_ref,
     gmix_ref, w_in_ref, pool_w_ref, pool_scale_ref, ret_g_ref,
     w_up_pool_ref, w_up_ret_ref, w_up_mem_ref, w_out_ref) = refs[:N_MIXER_INPUTS]
    cast_in = refs[N_MIXER_INPUTS:N_MIXER_INPUTS + n_cast]
    o_ref = refs[N_MIXER_INPUTS + n_cast]
    cast_out = refs[N_MIXER_INPUTS + n_cast + 1:N_MIXER_INPUTS + 2 * n_cast + 1]
    (h_ref, ubuf_ref, state_ref, qb_ref, kb_ref, kd_ref, lhs_ref, rhs_ref, gsilu_ref,
     ypool_ref, yret_ref, ymem_ref, macc_ref, merged_ref) = refs[N_MIXER_INPUTS + 2 * n_cast + 1:]
    _cast_blocks(cast_in, cast_out)
    gmix = gmix_ref[layer:layer + 1, :]
    pool_scale = pool_scale_ref[layer:layer + 1, :]
    ret_g = ret_g_ref[layer:layer + 1, :]
    tm = x_ref.shape[0] // n_sub
    n_chunks = tm // RET_CHUNK
    seq_step = pl.program_id(1)

    @pl.when(seq_step == 0)
    def _():
        ubuf_ref[0:POOL_HALO, :] = jnp.zeros((POOL_HALO, POOL_WIDTH), F32)
        state_ref[...] = jnp.zeros_like(state_ref)

    def tile(sub):
        xr = slice(sub * tm, (sub + 1) * tm)
        tr = slice(sub * (tm // 2), (sub + 1) * (tm // 2))
        tile_start = (seq_step * n_sub + sub) * tm
        h_ref[...] = _rms_norm(x_ref[xr, :], gmix).astype(BF16)

        def proj(off, width):
            return _dot(h_ref[...], w_in_ref[:, off:off + width])

        def chunk_rows(c):
            return slice(c * RET_CHUNK, (c + 1) * RET_CHUNK)

        def merge_chunk(b, y_ref, w_up_ref, j):
            cols = slice(j * MERGE_CHUNK, (j + 1) * MERGE_CHUNK)
            g = jax.nn.sigmoid(proj(OFF_GATE + b * D_MODEL + j * MERGE_CHUNK, MERGE_CHUNK))
            return g * _dot(y_ref[...], w_up_ref[:, cols])


        lane = lax.broadcasted_iota(jnp.int32, (tm // 2, RET_DK), 1)
        first = lane < RET_DK // 2
        cos_p = cos_ref[tr, :]
        sin_p = sin_ref[tr, :]
        cos_s = pltpu.roll(cos_p, RET_DK // 2, 1)
        sin_s = pltpu.roll(sin_p, RET_DK // 2, 1)
        cos2 = jnp.concatenate([jnp.where(first, cos_p, cos_s), jnp.where(first, cos_s, cos_p)], axis=0)
        sin2 = jnp.concatenate([jnp.where(first, -sin_p, sin_s), jnp.where(first, -sin_s, sin_p)], axis=0)
        q_all = proj(OFF_RQ, RET_QK_WIDTH)
        k_all = proj(OFF_RK, RET_QK_WIDTH)
        for hh in range(RET_HEADS):
            qk = slice(hh * RET_DK, (hh + 1) * RET_DK)
            qh = q_all[:, qk]
            kh = k_all[:, qk]
            qr = qh * cos2 + pltpu.roll(qh, RET_DK // 2, 1) * sin2
            kr = (kh * cos2 + pltpu.roll(kh, RET_DK // 2, 1) * sin2) * (RET_DK ** -0.5)
            qb_ref[:, qk] = qr.astype(BF16)
            kb_ref[:, qk] = kr.astype(BF16)
            v_h = proj(OFF_RV + hh * RET_DV, RET_DV).astype(BF16)
            for c in range(n_chunks):
                rows = chunk_rows(c)
                lhs_ref[hh, rows, RET_CHUNK:] = (qr[rows] * qdec_ref[hh]).astype(BF16)
                kd_ref[rows, qk] = (kr[rows] * kdec_ref[hh]).astype(BF16)
                rhs_ref[hh * n_chunks + c, 0:RET_CHUNK, :] = v_h[rows]

        ubuf_ref[POOL_HALO:POOL_HALO + tm, :] = proj(OFF_POOL, POOL_WIDTH)
        row = lax.broadcasted_iota(jnp.int32, (tm, 1), 0) + tile_start
        for gi, w in enumerate(POOL_WINDOWS):
            cols = slice(gi * POOL_GROUP_DIM, (gi + 1) * POOL_GROUP_DIM)
            win = ubuf_ref[:, cols]
            shift = 1
            while shift < w:
                win = win + pltpu.roll(win, shift, 0)
                shift *= 2
            cnt = jnp.minimum(row + 1, w).astype(F32)
            dwin = win[POOL_HALO:, :] / cnt - ubuf_ref[POOL_HALO:POOL_HALO + tm, cols]
            yg = _dot(dwin.astype(BF16), pool_w_ref[gi]) * pool_scale[:, cols]
            ypool_ref[:, cols] = yg.astype(BF16)
            vv = slice(gi * RET_DV, (gi + 1) * RET_DV)
            g_h = proj(OFF_RG + gi * RET_DV, RET_DV)
            gsilu_ref[:, vv] = g_h * jax.nn.sigmoid(g_h) * ret_g[:, vv]
        ubuf_ref[0:POOL_HALO, :] = ubuf_ref[tm:tm + POOL_HALO, :]

        for hh in range(RET_HEADS):
            qk = slice(hh * RET_DK, (hh + 1) * RET_DK)
            for c in range(n_chunks):
                rows = chunk_rows(c)
                scores = lax.dot_general(qb_ref[rows, qk], kb_ref[rows, qk],
                                         (((1,), (1,)), ((), ())),
                                         preferred_element_type=F32) * mask_ref[hh]
                lhs_ref[hh, rows, 0:RET_CHUNK] = scores.astype(BF16)

        for hh in range(RET_HEADS):
            qk = slice(hh * RET_DK, (hh + 1) * RET_DK)
            kvs = [lax.dot_general(kd_ref[chunk_rows(c), qk], rhs_ref[hh * n_chunks + c, 0:RET_CHUNK, :],
                                   (((0,), (0,)), ((), ())), preferred_element_type=F32)
                   for c in range(n_chunks)]
            s = state_ref[hh]
            for c in range(n_chunks):
                rhs_ref[hh * n_chunks + c, RET_CHUNK:, :] = s.astype(BF16)
                s = chunk_decay[hh] * s + kvs[c]
            state_ref[hh] = s

        xq = proj(OFF_XQ, XATTN_WIDTH).astype(BF16)

        for hh in range(RET_HEADS):
            vv = slice(hh * RET_DV, (hh + 1) * RET_DV)
            for c in range(n_chunks):
                rows = chunk_rows(c)
                y = _dot(lhs_ref[hh, rows, :], rhs_ref[hh * n_chunks + c])
                y = y * lax.rsqrt(jnp.mean(y * y, axis=-1, keepdims=True) + NORM_EPS)
                yret_ref[rows, vv] = (gsilu_ref[rows, vv] * y).astype(BF16)
            mc = slice(hh * MERGE_CHUNK, (hh + 1) * MERGE_CHUNK)
            macc_ref[:, mc] = merge_chunk(0, ypool_ref, w_up_pool_ref, hh)

        for hh in range(XATTN_HEADS):
            hd = slice(hh * XATTN_DH, (hh + 1) * XATTN_DH)
            s = _dot(xq[:, hd], kt_ref[hd, :])
            p = jnp.exp2((s - jnp.max(s, axis=-1, keepdims=True)) * SOFTMAX_EXP2_SCALE)
            p = p / jnp.sum(p, axis=-1, keepdims=True)
            ymem_ref[:, hd] = _dot(p.astype(BF16), vm_ref[:, hd]).astype(BF16)
            mc = slice(hh * MERGE_CHUNK, (hh + 1) * MERGE_CHUNK)
            macc_ref[:, mc] = macc_ref[:, mc] + merge_chunk(1, yret_ref, w_up_ret_ref, hh)

        for j in range(D_MODEL // MERGE_CHUNK):
            mc = slice(j * MERGE_CHUNK, (j + 1) * MERGE_CHUNK)
            merged_ref[:, mc] = (macc_ref[:, mc]
                                 + merge_chunk(2, ymem_ref, w_up_mem_ref, j)).astype(BF16)
        o_ref[xr, :] = x_ref[xr, :] + _dot(merged_ref[...], w_out_ref[...])

    for sub in range(n_sub):
        tile(sub)


def _mixer_layer(layer, x2d, batch, cos2, sin2, kt, vm, tables, gmix, w_in, pool_w, pool_scale,
                 ret_g, w_up_pool, w_up_ret, w_up_mem, w_out, cast_params=(), weight_layer=0):
    t, d = x2d.shape
    seq = t // batch
    tm = min(TOKEN_TILE, seq)
    n_sub = MIXER_SUBTILES if seq % (MIXER_SUBTILES * tm) == 0 else 1
    bm = n_sub * tm
    steps = seq // bm
    mask, q_dec, k_dec, chunk_decay = tables
    tok = lambda b, s: (b * steps + s, 0)
    in_specs = [
        pl.BlockSpec((bm, d), tok),
        pl.BlockSpec((bm // 2, RET_DK), tok),
        pl.BlockSpec((bm // 2, RET_DK), tok),
        pl.BlockSpec((None, None, XATTN_WIDTH, N_MEM), lambda b, s: (layer, b, 0, 0)),
        pl.BlockSpec((None, None, N_MEM, XATTN_WIDTH), lambda b, s: (layer, b, 0, 0)),
        _const_spec(mask.shape), _const_spec(q_dec.shape), _const_spec(k_dec.shape),
        _const_spec(gmix.shape), _layer_spec(w_in.shape, weight_layer),
        _layer_spec(pool_w.shape, weight_layer), _const_spec(pool_scale.shape),
        _const_spec(ret_g.shape),
    ] + [_layer_spec(w.shape, weight_layer) for w in (w_up_pool, w_up_ret, w_up_mem, w_out)]
    assert len(in_specs) == N_MIXER_INPUTS
    cast_views, cast_in_specs, cast_out_specs, cast_shapes = _cast_specs(
        cast_params, batch * steps, lambda b, s: b * steps + s, layer=layer)
    outs = pl.pallas_call(
        functools.partial(_mixer_kernel, chunk_decay, layer, n_sub, len(cast_params)),
        grid=(batch, steps),
        in_specs=in_specs + cast_in_specs,
        out_specs=[pl.BlockSpec((bm, d), tok)] + cast_out_specs,
        out_shape=[jax.ShapeDtypeStruct((t, d), F32)] + cast_shapes,
        scratch_shapes=[
            pltpu.VMEM((tm, d), BF16),
            pltpu.VMEM((POOL_HALO + tm, POOL_WIDTH), F32),
            pltpu.VMEM((RET_HEADS, RET_DK, RET_DV), F32),
            pltpu.VMEM((tm, RET_QK_WIDTH), BF16),
            pltpu.VMEM((tm, RET_QK_WIDTH), BF16),
            pltpu.VMEM((tm, RET_QK_WIDTH), BF16),
            pltpu.VMEM((RET_HEADS, tm, RET_CHUNK + RET_DK), BF16),
            pltpu.VMEM((RET_HEADS * (tm // RET_CHUNK), RET_CHUNK + RET_DK, RET_DV), BF16),
            pltpu.VMEM((tm, RET_V_WIDTH), F32),
            pltpu.VMEM((tm, POOL_WIDTH), BF16),
            pltpu.VMEM((tm, RET_V_WIDTH), BF16),
            pltpu.VMEM((tm, XATTN_WIDTH), BF16),
            pltpu.VMEM((tm, D_MODEL), F32),
            pltpu.VMEM((tm, D_MODEL), BF16),
        ],
        compiler_params=pltpu.CompilerParams(
            dimension_semantics=("arbitrary", "arbitrary"),
            vmem_limit_bytes=VMEM_LIMIT_BYTES),
        name="mixer",
    )(x2d, cos2, sin2, kt, vm, mask, q_dec, k_dec, gmix, w_in, pool_w, pool_scale, ret_g,
      w_up_pool, w_up_ret, w_up_mem, w_out, *cast_views)
    return outs[0], [o.reshape((1,) + p.shape[1:]) for o, p in zip(outs[1:], cast_params)]


def _mlp_kernel(apply_final_norm, layer, n_cast, *refs):
    x_ref, g_ref, w1_ref, w2_ref, gf_ref = refs[:N_MLP_INPUTS]
    cast_in = refs[N_MLP_INPUTS:N_MLP_INPUTS + n_cast]
    o_ref = refs[N_MLP_INPUTS + n_cast]
    cast_out = refs[N_MLP_INPUTS + n_cast + 1:N_MLP_INPUTS + 2 * n_cast + 1]
    (hid_ref,) = refs[N_MLP_INPUTS + 2 * n_cast + 1:]
    _cast_blocks(cast_in, cast_out)
    x = x_ref[...]
    h = _rms_norm(x, g_ref[layer:layer + 1, :]).astype(BF16)
    for j in range(D_FF // N_CHUNK):
        nc = slice(j * N_CHUNK, (j + 1) * N_CHUNK)
        a = jnp.maximum(_dot(h, w1_ref[:, nc]), 0.0)
        hid_ref[:, nc] = (a * a).astype(BF16)
    y = x + _dot(hid_ref[...], w2_ref[...])
    if apply_final_norm:
        y = _rms_norm(y, gf_ref[...])
    o_ref[...] = y


def _mlp_layer(layer, x2d, g, w1, w2, g_final, apply_final_norm, cast_params=(), cast_layer=None):
    t, d = x2d.shape
    tm = min(MLP_TOKEN_TILE, t)
    steps = t // tm
    cast_views, cast_in_specs, cast_out_specs, cast_shapes = _cast_specs(
        cast_params, steps, lambda i: i, layer=cast_layer)
    outs = pl.pallas_call(
        functools.partial(_mlp_kernel, apply_final_norm, layer, len(cast_params)),
        grid=(steps,),
        in_specs=[pl.BlockSpec((tm, d), lambda i: (i, 0)),
                  _const_spec(g.shape), _layer_spec(w1.shape, 0),
                  _layer_spec(w2.shape, 0), _const_spec(g_final.shape)] + cast_in_specs,
        out_specs=[pl.BlockSpec((tm, d), lambda i: (i, 0))] + cast_out_specs,
        out_shape=[jax.ShapeDtypeStruct((t, d), F32)] + cast_shapes,
        scratch_shapes=[pltpu.VMEM((tm, D_FF), BF16)],
        compiler_params=pltpu.CompilerParams(
            dimension_semantics=("arbitrary",),
            vmem_limit_bytes=VMEM_LIMIT_BYTES),
        name="mlp",
    )(x2d, g, w1, w2, g_final, *cast_views)
    return outs[0], [o.reshape((1,) + p.shape[1:]) for o, p in zip(outs[1:], cast_params)]


def kernel(x, mem, positions, norm_mix_g, w_in, pool_w, pool_scale, ret_norm_g, mem_norm_g,
           w_mem_kv, w_up_pool, w_up_ret, w_up_mem, w_out, norm_mlp_g, w_mlp1, w_mlp2,
           final_norm_g):
    batch, seq, d = x.shape
    depth = w_in.shape[0]
    assert d == D_MODEL and w_in.shape[-1] == IN_COLS and mem.shape[1] == N_MEM
    assert seq % RET_CHUNK == 0 and seq % min(TOKEN_TILE, seq) == 0

    mixer_w_f32 = (w_in, pool_w, w_up_pool, w_up_ret, w_up_mem, w_out)
    cos2, sin2, mixer_w = _rope_tables(positions, min(TOKEN_TILE, seq), mixer_w_f32, cast_layer=0)
    kt, vm = _mem_kv(mem, mem_norm_g, w_mem_kv)
    tables = _retention_tables()
    g_final = final_norm_g.reshape(1, -1)

    x2d = x.reshape(batch * seq, d)
    for l in range(depth):
        w_in_b, pool_w_b, w_up_pool_b, w_up_ret_b, w_up_mem_b, w_out_b = mixer_w
        x2d, mlp_w = _mixer_layer(
            l, x2d, batch, cos2, sin2, kt, vm, tables, norm_mix_g, w_in_b, pool_w_b, pool_scale,
            ret_norm_g, w_up_pool_b, w_up_ret_b, w_up_mem_b, w_out_b,
            cast_params=(w_mlp1, w_mlp2), weight_layer=0)
        last = l == depth - 1
        x2d, mixer_w = _mlp_layer(
            l, x2d, norm_mlp_g, mlp_w[0], mlp_w[1], g_final, apply_final_norm=last,
            cast_params=() if last else mixer_w_f32, cast_layer=None if last else l + 1)
    return x2d.reshape(batch, seq, d)
```

```python
import functools

import numpy as np
import jax
import jax.numpy as jnp
from jax import lax
from jax.experimental import pallas as pl
from jax.experimental.pallas import tpu as pltpu

D_MODEL = 1024
N_MEM = 256
POOL_WINDOWS = (2, 4, 8, 16)
POOL_GROUPS = 4
POOL_GROUP_DIM = 128
POOL_WIDTH = POOL_GROUPS * POOL_GROUP_DIM
POOL_HALO = 16
RET_HEADS = 4
RET_DK = 128
RET_DV = 256
RET_QK_WIDTH = RET_HEADS * RET_DK
RET_V_WIDTH = RET_HEADS * RET_DV
RET_CHUNK = 128
ROPE_BASE = 10000.0
XATTN_HEADS = 4
XATTN_DH = 128
XATTN_WIDTH = XATTN_HEADS * XATTN_DH
D_FF = 4 * D_MODEL
NORM_EPS = 1e-6

OFF_POOL = 0
OFF_RQ = OFF_POOL + POOL_WIDTH
OFF_RK = OFF_RQ + RET_QK_WIDTH
OFF_RV = OFF_RK + RET_QK_WIDTH
OFF_RG = OFF_RV + RET_V_WIDTH
OFF_XQ = OFF_RG + RET_V_WIDTH
OFF_GATE = OFF_XQ + XATTN_WIDTH
IN_COLS = OFF_GATE + 3 * D_MODEL

TOKEN_TILE = 512
MIXER_SUBTILES = 1
MLP_TOKEN_TILE = 1024
MLP_ROW_PARTS = 2
ROPE_TILE = 1024
N_CHUNK = 512
MERGE_CHUNK = 256
VMEM_LIMIT_BYTES = 60 * 1024 * 1024
BF16_SUBLANES = 16
N_MIXER_INPUTS = 17
N_MLP_INPUTS = 5

SOFTMAX_EXP2_SCALE = float(XATTN_DH ** -0.5 * np.log2(np.e))

BF16 = jnp.bfloat16
F32 = jnp.float32


def _dot(a, b):
    return jnp.dot(a, b, preferred_element_type=F32)


def _rms_norm(x, g):
    return x * lax.rsqrt(jnp.mean(x * x, axis=-1, keepdims=True) + NORM_EPS) * g


def _const_spec(shape):
    nd = len(shape)
    return pl.BlockSpec(shape, lambda *_: (0,) * nd, pipeline_mode=pl.Buffered(1))


def _layer_spec(stacked_shape, layer):
    nd = len(stacked_shape) - 1
    return pl.BlockSpec((None,) + tuple(stacked_shape[1:]), lambda *_: (layer,) + (0,) * nd,
                        pipeline_mode=pl.Buffered(1))


def _cast_specs(params, n_steps, index_of_step, layer=None):
    views, in_specs, out_specs, out_shapes = [], [], [], []
    for p in params:
        cols = p.shape[-1]
        rows_per_layer = int(np.prod(p.shape[1:-1]))
        rows = rows_per_layer if layer is not None else p.shape[0] * rows_per_layer
        slab, rem = divmod(rows, n_steps)
        assert rem == 0 and slab % BF16_SUBLANES == 0, (p.shape, n_steps)
        first = 0 if layer is None else layer * n_steps
        views.append(p.reshape(-1, cols))
        in_specs.append(pl.BlockSpec((slab, cols), lambda *g, f=first: (f + index_of_step(*g), 0)))
        out_specs.append(pl.BlockSpec((slab, cols), lambda *g: (index_of_step(*g), 0)))
        out_shapes.append(jax.ShapeDtypeStruct((rows, cols), BF16))
    return views, in_specs, out_specs, out_shapes


def _cast_blocks(in_refs, out_refs):
    for src_ref, dst_ref in zip(in_refs, out_refs):
        dst_ref[...] = src_ref[...].astype(BF16)


def _rope_kernel(n_cast, pos_ref, freq_ref, *refs):
    cast_in = refs[:n_cast]
    cos_ref, sin_ref = refs[n_cast:n_cast + 2]
    _cast_blocks(cast_in, refs[n_cast + 2:])
    lane = lax.broadcasted_iota(jnp.int32, cos_ref.shape, 1)
    pos = pos_ref[...]
    ang = jnp.where(lane < RET_DK // 2, pos[:, 0:1], pos[:, 1:2]) * freq_ref[...]
    cos_ref[...] = jnp.cos(ang)
    sin_ref[...] = jnp.sin(ang)


def _rope_tables(positions, token_tile, cast_params=(), cast_layer=0):
    t = positions.size
    half_tile = token_tile // 2
    half = RET_DK // 2
    inv_freq = ROPE_BASE ** (-jnp.arange(half, dtype=F32) / half)
    freq2 = jnp.concatenate([inv_freq, inv_freq]).reshape(1, RET_DK)
    pos = positions.astype(F32).reshape(t // token_tile, 2, half_tile)
    pos = jnp.swapaxes(pos, 1, 2).reshape(t // 2, 2)
    rows = min(ROPE_TILE, t // 2)
    steps = t // 2 // rows
    cast_views, cast_in_specs, cast_out_specs, cast_shapes = _cast_specs(
        cast_params, steps, lambda i: i, layer=cast_layer)
    outs = pl.pallas_call(
        functools.partial(_rope_kernel, len(cast_params)),
        grid=(steps,),
        in_specs=[pl.BlockSpec((rows, 2), lambda i: (i, 0)),
                  pl.BlockSpec((1, RET_DK), lambda i: (0, 0))] + cast_in_specs,
        out_specs=[pl.BlockSpec((rows, RET_DK), lambda i: (i, 0))] * 2 + cast_out_specs,
        out_shape=[jax.ShapeDtypeStruct((t // 2, RET_DK), F32)] * 2 + cast_shapes,
        name="rope_tables",
    )(pos, freq2, *cast_views)
    return outs[0], outs[1], [o.reshape((1,) + p.shape[1:]) for o, p in zip(outs[2:], cast_params)]


def _mem_kv_kernel(mem_ref, g_ref, w_ref, kt_ref, v_ref):
    mem_n = _rms_norm(mem_ref[...], g_ref[...]).astype(BF16)
    kv = _dot(mem_n, w_ref[...])
    kt_ref[...] = kv[:, :XATTN_WIDTH].T.astype(BF16)
    v_ref[...] = kv[:, XATTN_WIDTH:].astype(BF16)


def _mem_kv(mem, mem_norm_g, w_mem_kv):
    depth = w_mem_kv.shape[0]
    b, m, d = mem.shape
    return pl.pallas_call(
        _mem_kv_kernel,
        grid=(depth, b),
        in_specs=[pl.BlockSpec((None, m, d), lambda l, i: (i, 0, 0)),
                  pl.BlockSpec((None, 1, d), lambda l, i: (l, 0, 0)),
                  pl.BlockSpec((None, d, 2 * XATTN_WIDTH), lambda l, i: (l, 0, 0))],
        out_specs=[pl.BlockSpec((None, None, XATTN_WIDTH, m), lambda l, i: (l, i, 0, 0)),
                   pl.BlockSpec((None, None, m, XATTN_WIDTH), lambda l, i: (l, i, 0, 0))],
        out_shape=[jax.ShapeDtypeStruct((depth, b, XATTN_WIDTH, m), BF16),
                   jax.ShapeDtypeStruct((depth, b, m, XATTN_WIDTH), BF16)],
        name="mem_kv",
    )(mem, mem_norm_g.reshape(depth, 1, d), w_mem_kv.astype(BF16))


def _retention_tables():
    c = RET_CHUNK
    log_gamma = np.log(1.0 - 2.0 ** (-5.0 - np.arange(RET_HEADS, dtype=np.float64)))
    idx = np.arange(c, dtype=np.float64)
    diff = idx[:, None] - idx[None, :]
    mask = np.where(diff[None] >= 0, np.exp(diff[None] * log_gamma[:, None, None]), 0.0)
    q_dec = np.exp((idx + 1.0)[None, :] * log_gamma[:, None])
    k_dec = np.exp((c - 1.0 - idx)[None, :] * log_gamma[:, None])
    q_dec = np.broadcast_to(q_dec[:, :, None], (RET_HEADS, c, RET_DK))
    k_dec = np.broadcast_to(k_dec[:, :, None], (RET_HEADS, c, RET_DK))
    chunk_decay = tuple(float(v) for v in np.exp(c * log_gamma))
    as_f32 = lambda a: jnp.asarray(np.ascontiguousarray(a), dtype=F32)
    return as_f32(mask), as_f32(q_dec), as_f32(k_dec), chunk_decay


def _mixer_kernel(chunk_decay, layer, n_sub, n_cast, *refs):
    (x_ref, cos_ref, sin_ref, kt_ref, vm_ref, mask_ref, qdec_ref, kdec_ref,
     gmix_ref, w_in_ref, pool_w_ref, pool_scale_ref, ret_g_ref,
     w_up_pool_ref, w_up_ret_ref, w_up_mem_ref, w_out_ref) = refs[:N_MIXER_INPUTS]
    cast_in = refs[N_MIXER_INPUTS:N_MIXER_INPUTS + n_cast]
    o_ref = refs[N_MIXER_INPUTS + n_cast]
    cast_out = refs[N_MIXER_INPUTS + n_cast + 1:N_MIXER_INPUTS + 2 * n_cast + 1]
    (h_ref, ubuf_ref, state_ref, qb_ref, kb_ref, kd_ref, lhs_ref, rhs_ref, gsilu_ref,
     ypool_ref, yret_ref, ymem_ref, macc_ref, merged_ref) = refs[N_MIXER_INPUTS + 2 * n_cast + 1:]
    _cast_blocks(cast_in, cast_out)
    gmix = gmix_ref[layer:layer + 1, :]
    pool_scale = pool_scale_ref[layer:layer + 1, :]
    ret_g = ret_g_ref[layer:layer + 1, :]
    tm = x_ref.shape[0] // n_sub
    n_chunks = tm // RET_CHUNK
    seq_step = pl.program_id(1)

    @pl.when(seq_step == 0)
    def _():
        ubuf_ref[0:POOL_HALO, :] = jnp.zeros((POOL_HALO, POOL_WIDTH), F32)
        state_ref[...] = jnp.zeros_like(state_ref)

    def tile(sub):
        xr = slice(sub * tm, (sub + 1) * tm)
        tr = slice(sub * (tm // 2), (sub + 1) * (tm // 2))
        tile_start = (seq_step * n_sub + sub) * tm
        half_rows = (slice(0, tm // 2), slice(tm // 2, tm))
        for hr in half_rows:
            h_ref[hr, :] = _rms_norm(x_ref[sub * tm + hr.start:sub * tm + hr.stop, :],
                                     gmix).astype(BF16)

        def proj(off, width):
            return _dot(h_ref[...], w_in_ref[:, off:off + width])

        def chunk_rows(c):
            return slice(c * RET_CHUNK, (c + 1) * RET_CHUNK)

        def merge_chunk(b, y_ref, w_up_ref, j):
            cols = slice(j * MERGE_CHUNK, (j + 1) * MERGE_CHUNK)
            g = jax.nn.sigmoid(proj(OFF_GATE + b * D_MODEL + j * MERGE_CHUNK, MERGE_CHUNK))
            return g * _dot(y_ref[...], w_up_ref[:, cols])


        lane = lax.broadcasted_iota(jnp.int32, (tm // 2, RET_DK), 1)
        first = lane < RET_DK // 2
        cos_p = cos_ref[tr, :]
        sin_p = sin_ref[tr, :]
        cos_s = pltpu.roll(cos_p, RET_DK // 2, 1)
        sin_s = pltpu.roll(sin_p, RET_DK // 2, 1)
        cos2 = jnp.concatenate([jnp.where(first, cos_p, cos_s), jnp.where(first, cos_s, cos_p)], axis=0)
        sin2 = jnp.concatenate([jnp.where(first, -sin_p, sin_s), jnp.where(first, -sin_s, sin_p)], axis=0)
        q_all = jnp.concatenate(
            [_dot(h_ref[hr, :], w_in_ref[:, OFF_RQ:OFF_RQ + RET_QK_WIDTH]) for hr in half_rows], axis=0)
        k_all = proj(OFF_RK, RET_QK_WIDTH)
        for hh in range(RET_HEADS):
            qk = slice(hh * RET_DK, (hh + 1) * RET_DK)
            qh = q_all[:, qk]
            kh = k_all[:, qk]
            qr = qh * cos2 + pltpu.roll(qh, RET_DK // 2, 1) * sin2
            kr = (kh * cos2 + pltpu.roll(kh, RET_DK // 2, 1) * sin2) * (RET_DK ** -0.5)
            qb_ref[:, qk] = qr.astype(BF16)
            kb_ref[:, qk] = kr.astype(BF16)
            v_h = proj(OFF_RV + hh * RET_DV, RET_DV).astype(BF16)
            for c in range(n_chunks):
                rows = chunk_rows(c)
                lhs_ref[hh, rows, RET_CHUNK:] = (qr[rows] * qdec_ref[hh]).astype(BF16)
                kd_ref[rows, qk] = (kr[rows] * kdec_ref[hh]).astype(BF16)
                rhs_ref[hh * n_chunks + c, 0:RET_CHUNK, :] = v_h[rows]

        ubuf_ref[POOL_HALO:POOL_HALO + tm, :] = proj(OFF_POOL, POOL_WIDTH)
        row = lax.broadcasted_iota(jnp.int32, (tm, 1), 0) + tile_start
        for gi, w in enumerate(POOL_WINDOWS):
            cols = slice(gi * POOL_GROUP_DIM, (gi + 1) * POOL_GROUP_DIM)
            win = ubuf_ref[:, cols]
            shift = 1
            while shift < w:
                win = win + pltpu.roll(win, shift, 0)
                shift *= 2
            cnt = jnp.minimum(row + 1, w).astype(F32)
            dwin = win[POOL_HALO:, :] / cnt - ubuf_ref[POOL_HALO:POOL_HALO + tm, cols]
            yg = _dot(dwin.astype(BF16), pool_w_ref[gi]) * pool_scale[:, cols]
            ypool_ref[:, cols] = yg.astype(BF16)
            vv = slice(gi * RET_DV, (gi + 1) * RET_DV)
            g_h = proj(OFF_RG + gi * RET_DV, RET_DV)
            gsilu_ref[:, vv] = g_h * jax.nn.sigmoid(g_h) * ret_g[:, vv]
        ubuf_ref[0:POOL_HALO, :] = ubuf_ref[tm:tm + POOL_HALO, :]

        for hh in range(RET_HEADS):
            qk = slice(hh * RET_DK, (hh + 1) * RET_DK)
            for c in range(n_chunks):
                rows = chunk_rows(c)
                scores = lax.dot_general(qb_ref[rows, qk], kb_ref[rows, qk],
                                         (((1,), (1,)), ((), ())),
                                         preferred_element_type=F32) * mask_ref[hh]
                lhs_ref[hh, rows, 0:RET_CHUNK] = scores.astype(BF16)

        for hh in range(RET_HEADS):
            qk = slice(hh * RET_DK, (hh + 1) * RET_DK)
            kvs = [lax.dot_general(kd_ref[chunk_rows(c), qk], rhs_ref[hh * n_chunks + c, 0:RET_CHUNK, :],
                                   (((0,), (0,)), ((), ())), preferred_element_type=F32)
                   for c in range(n_chunks)]
            s = state_ref[hh]
            for c in range(n_chunks):
                rhs_ref[hh * n_chunks + c, RET_CHUNK:, :] = s.astype(BF16)
                s = chunk_decay[hh] * s + kvs[c]
            state_ref[hh] = s

        xq = proj(OFF_XQ, XATTN_WIDTH).astype(BF16)

        for hh in range(RET_HEADS):
            vv = slice(hh * RET_DV, (hh + 1) * RET_DV)
            for c in range(n_chunks):
                rows = chunk_rows(c)
                y = _dot(lhs_ref[hh, rows, :], rhs_ref[hh * n_chunks + c])
                y = y * lax.rsqrt(jnp.mean(y * y, axis=-1, keepdims=True) + NORM_EPS)
                yret_ref[rows, vv] = (gsilu_ref[rows, vv] * y).astype(BF16)
            mc = slice(hh * MERGE_CHUNK, (hh + 1) * MERGE_CHUNK)
            macc_ref[:, mc] = merge_chunk(0, ypool_ref, w_up_pool_ref, hh)

        for hh in range(XATTN_HEADS):
            hd = slice(hh * XATTN_DH, (hh + 1) * XATTN_DH)
            s = _dot(xq[:, hd], kt_ref[hd, :])
            p = jnp.exp2((s - jnp.max(s, axis=-1, keepdims=True)) * SOFTMAX_EXP2_SCALE)
            p = p / jnp.sum(p, axis=-1, keepdims=True)
            ymem_ref[:, hd] = _dot(p.astype(BF16), vm_ref[:, hd]).astype(BF16)
            mc = slice(hh * MERGE_CHUNK, (hh + 1) * MERGE_CHUNK)
            macc_ref[:, mc] = macc_ref[:, mc] + merge_chunk(1, yret_ref, w_up_ret_ref, hh)

        for j in range(D_MODEL // MERGE_CHUNK):
            mc = slice(j * MERGE_CHUNK, (j + 1) * MERGE_CHUNK)
            merged_ref[:, mc] = (macc_ref[:, mc]
                                 + merge_chunk(2, ymem_ref, w_up_mem_ref, j)).astype(BF16)
        o_ref[xr, :] = x_ref[xr, :] + _dot(merged_ref[...], w_out_ref[...])

    for sub in range(n_sub):
        tile(sub)


def _mixer_layer(layer, x2d, batch, cos2, sin2, kt, vm, tables, gmix, w_in, pool_w, pool_scale,
                 ret_g, w_up_pool, w_up_ret, w_up_mem, w_out, cast_params=(), weight_layer=0):
    t, d = x2d.shape
    seq = t // batch
    tm = min(TOKEN_TILE, seq)
    n_sub = MIXER_SUBTILES if seq % (MIXER_SUBTILES * tm) == 0 else 1
    bm = n_sub * tm
    steps = seq // bm
    mask, q_dec, k_dec, chunk_decay = tables
    tok = lambda b, s: (b * steps + s, 0)
    in_specs = [
        pl.BlockSpec((bm, d), tok),
        pl.BlockSpec((bm // 2, RET_DK), tok),
        pl.BlockSpec((bm // 2, RET_DK), tok),
        pl.BlockSpec((None, None, XATTN_WIDTH, N_MEM), lambda b, s: (layer, b, 0, 0)),
        pl.BlockSpec((None, None, N_MEM, XATTN_WIDTH), lambda b, s: (layer, b, 0, 0)),
        _const_spec(mask.shape), _const_spec(q_dec.shape), _const_spec(k_dec.shape),
        _const_spec(gmix.shape), _layer_spec(w_in.shape, weight_layer),
        _layer_spec(pool_w.shape, weight_layer), _const_spec(pool_scale.shape),
        _const_spec(ret_g.shape),
    ] + [_layer_spec(w.shape, weight_layer) for w in (w_up_pool, w_up_ret, w_up_mem, w_out)]
    assert len(in_specs) == N_MIXER_INPUTS
    cast_views, cast_in_specs, cast_out_specs, cast_shapes = _cast_specs(
        cast_params, batch * steps, lambda b, s: b * steps + s, layer=layer)
    outs = pl.pallas_call(
        functools.partial(_mixer_kernel, chunk_decay, layer, n_sub, len(cast_params)),
        grid=(batch, steps),
        in_specs=in_specs + cast_in_specs,
        out_specs=[pl.BlockSpec((bm, d), tok)] + cast_out_specs,
        out_shape=[jax.ShapeDtypeStruct((t, d), F32)] + cast_shapes,
        scratch_shapes=[
            pltpu.VMEM((tm, d), BF16),
            pltpu.VMEM((POOL_HALO + tm, POOL_WIDTH), F32),
            pltpu.VMEM((RET_HEADS, RET_DK, RET_DV), F32),
            pltpu.VMEM((tm, RET_QK_WIDTH), BF16),
            pltpu.VMEM((tm, RET_QK_WIDTH), BF16),
            pltpu.VMEM((tm, RET_QK_WIDTH), BF16),
            pltpu.VMEM((RET_HEADS, tm, RET_CHUNK + RET_DK), BF16),
            pltpu.VMEM((RET_HEADS * (tm // RET_CHUNK), RET_CHUNK + RET_DK, RET_DV), BF16),
            pltpu.VMEM((tm, RET_V_WIDTH), F32),
            pltpu.VMEM((tm, POOL_WIDTH), BF16),
            pltpu.VMEM((tm, RET_V_WIDTH), BF16),
            pltpu.VMEM((tm, XATTN_WIDTH), BF16),
            pltpu.VMEM((tm, D_MODEL), F32),
            pltpu.VMEM((tm, D_MODEL), BF16),
        ],
        compiler_params=pltpu.CompilerParams(
            dimension_semantics=("arbitrary", "arbitrary"),
            vmem_limit_bytes=VMEM_LIMIT_BYTES),
        name="mixer",
    )(x2d, cos2, sin2, kt, vm, mask, q_dec, k_dec, gmix, w_in, pool_w, pool_scale, ret_g,
      w_up_pool, w_up_ret, w_up_mem, w_out, *cast_views)
    return outs[0], [o.reshape((1,) + p.shape[1:]) for o, p in zip(outs[1:], cast_params)]


def _mlp_kernel(apply_final_norm, layer, n_cast, *refs):
    x_ref, g_ref, w1_ref, w2_ref, gf_ref = refs[:N_MLP_INPUTS]
    cast_in = refs[N_MLP_INPUTS:N_MLP_INPUTS + n_cast]
    o_ref = refs[N_MLP_INPUTS + n_cast]
    cast_out = refs[N_MLP_INPUTS + n_cast + 1:N_MLP_INPUTS + 2 * n_cast + 1]
    (hid_ref,) = refs[N_MLP_INPUTS + 2 * n_cast + 1:]
    _cast_blocks(cast_in, cast_out)
    g = g_ref[layer:layer + 1, :]
    tm = x_ref.shape[0]
    n_parts = MLP_ROW_PARTS if tm % (MLP_ROW_PARTS * BF16_SUBLANES) == 0 else 1
    part_rows = [slice(i * (tm // n_parts), (i + 1) * (tm // n_parts)) for i in range(n_parts)]
    for rows in part_rows:
        h = _rms_norm(x_ref[rows, :], g).astype(BF16)
        for j in range(D_FF // N_CHUNK):
            nc = slice(j * N_CHUNK, (j + 1) * N_CHUNK)
            a = jnp.maximum(_dot(h, w1_ref[:, nc]), 0.0)
            hid_ref[rows, nc] = (a * a).astype(BF16)
    for rows in part_rows:
        y = x_ref[rows, :] + _dot(hid_ref[rows, :], w2_ref[...])
        if apply_final_norm:
            y = _rms_norm(y, gf_ref[...])
        o_ref[rows, :] = y


def _mlp_layer(layer, x2d, g, w1, w2, g_final, apply_final_norm, cast_params=(), cast_layer=None):
    t, d = x2d.shape
    tm = min(MLP_TOKEN_TILE, t)
    steps = t // tm
    cast_views, cast_in_specs, cast_out_specs, cast_shapes = _cast_specs(
        cast_params, steps, lambda i: i, layer=cast_layer)
    outs = pl.pallas_call(
        functools.partial(_mlp_kernel, apply_final_norm, layer, len(cast_params)),
        grid=(steps,),
        in_specs=[pl.BlockSpec((tm, d), lambda i: (i, 0)),
                  _const_spec(g.shape), _layer_spec(w1.shape, 0),
                  _layer_spec(w2.shape, 0), _const_spec(g_final.shape)] + cast_in_specs,
        out_specs=[pl.BlockSpec((tm, d), lambda i: (i, 0))] + cast_out_specs,
        out_shape=[jax.ShapeDtypeStruct((t, d), F32)] + cast_shapes,
        scratch_shapes=[pltpu.VMEM((tm, D_FF), BF16)],
        compiler_params=pltpu.CompilerParams(
            dimension_semantics=("arbitrary",),
            vmem_limit_bytes=VMEM_LIMIT_BYTES),
        name="mlp",
    )(x2d, g, w1, w2, g_final, *cast_views)
    return outs[0], [o.reshape((1,) + p.shape[1:]) for o, p in zip(outs[1:], cast_params)]


def kernel(x, mem, positions, norm_mix_g, w_in, pool_w, pool_scale, ret_norm_g, mem_norm_g,
           w_mem_kv, w_up_pool, w_up_ret, w_up_mem, w_out, norm_mlp_g, w_mlp1, w_mlp2,
           final_norm_g):
    batch, seq, d = x.shape
    depth = w_in.shape[0]
    assert d == D_MODEL and w_in.shape[-1] == IN_COLS and mem.shape[1] == N_MEM
    assert seq % RET_CHUNK == 0 and seq % min(TOKEN_TILE, seq) == 0

    mixer_w_f32 = (w_in, pool_w, w_up_pool, w_up_ret, w_up_mem, w_out)
    cos2, sin2, mixer_w = _rope_tables(positions, min(TOKEN_TILE, seq), mixer_w_f32, cast_layer=0)
    kt, vm = _mem_kv(mem, mem_norm_g, w_mem_kv)
    tables = _retention_tables()
    g_final = final_norm_g.reshape(1, -1)

    x2d = x.reshape(batch * seq, d)
    for l in range(depth):
        w_in_b, pool_w_b, w_up_pool_b, w_up_ret_b, w_up_mem_b, w_out_b = mixer_w
        x2d, mlp_w = _mixer_layer(
            l, x2d, batch, cos2, sin2, kt, vm, tables, norm_mix_g, w_in_b, pool_w_b, pool_scale,
            ret_norm_g, w_up_pool_b, w_up_ret_b, w_up_mem_b, w_out_b,
            cast_params=(w_mlp1, w_mlp2), weight_layer=0)
        last = l == depth - 1
        x2d, mixer_w = _mlp_layer(
            l, x2d, norm_mlp_g, mlp_w[0], mlp_w[1], g_final, apply_final_norm=last,
            cast_params=() if last else mixer_w_f32, cast_layer=None if last else l + 1)
    return x2d.reshape(batch, seq, d)
```

```python
import functools

import numpy as np
import jax
import jax.numpy as jnp
from jax import lax
from jax.experimental import pallas as pl
from jax.experimental.pallas import tpu as pltpu

D_MODEL = 1024
N_MEM = 256
POOL_WINDOWS = (2, 4, 8, 16)
POOL_GROUPS = 4
POOL_GROUP_DIM = 128
POOL_WIDTH = POOL_GROUPS * POOL_GROUP_DIM
POOL_HALO = 16
RET_HEADS = 4
RET_DK = 128
RET_DV = 256
RET_QK_WIDTH = RET_HEADS * RET_DK
RET_V_WIDTH = RET_HEADS * RET_DV
RET_CHUNK = 128
HEAD_PAIRS = RET_HEADS // 2
ROPE_BASE = 10000.0
XATTN_HEADS = 4
XATTN_DH = 128
XATTN_WIDTH = XATTN_HEADS * XATTN_DH
D_FF = 4 * D_MODEL
NORM_EPS = 1e-6

OFF_POOL = 0
OFF_RQ = OFF_POOL + POOL_WIDTH
OFF_RK = OFF_RQ + RET_QK_WIDTH
OFF_RV = OFF_RK + RET_QK_WIDTH
OFF_RG = OFF_RV + RET_V_WIDTH
OFF_XQ = OFF_RG + RET_V_WIDTH
OFF_GATE = OFF_XQ + XATTN_WIDTH
IN_COLS = OFF_GATE + 3 * D_MODEL

TOKEN_TILE = 512
MIXER_SUBTILES = 1
MLP_TOKEN_TILE = 1024
MLP_ROW_PARTS = 2
ROPE_TILE = 1024
N_CHUNK = 512
MERGE_CHUNK = 256
VMEM_LIMIT_BYTES = 60 * 1024 * 1024
BF16_SUBLANES = 16
N_MIXER_INPUTS = 17
N_MLP_INPUTS = 5

SOFTMAX_EXP2_SCALE = float(XATTN_DH ** -0.5 * np.log2(np.e))

BF16 = jnp.bfloat16
F32 = jnp.float32


def _dot(a, b):
    return jnp.dot(a, b, preferred_element_type=F32)


def _rms_norm(x, g):
    return x * lax.rsqrt(jnp.mean(x * x, axis=-1, keepdims=True) + NORM_EPS) * g


def _const_spec(shape):
    nd = len(shape)
    return pl.BlockSpec(shape, lambda *_: (0,) * nd, pipeline_mode=pl.Buffered(1))


def _layer_spec(stacked_shape, layer):
    nd = len(stacked_shape) - 1
    return pl.BlockSpec((None,) + tuple(stacked_shape[1:]), lambda *_: (layer,) + (0,) * nd,
                        pipeline_mode=pl.Buffered(1))


def _cast_specs(params, n_steps, index_of_step, layer=None):
    views, in_specs, out_specs, out_shapes = [], [], [], []
    for p in params:
        cols = p.shape[-1]
        rows_per_layer = int(np.prod(p.shape[1:-1]))
        rows = rows_per_layer if layer is not None else p.shape[0] * rows_per_layer
        slab, rem = divmod(rows, n_steps)
        assert rem == 0 and slab % BF16_SUBLANES == 0, (p.shape, n_steps)
        first = 0 if layer is None else layer * n_steps
        views.append(p.reshape(-1, cols))
        in_specs.append(pl.BlockSpec((slab, cols), lambda *g, f=first: (f + index_of_step(*g), 0)))
        out_specs.append(pl.BlockSpec((slab, cols), lambda *g: (index_of_step(*g), 0)))
        out_shapes.append(jax.ShapeDtypeStruct((rows, cols), BF16))
    return views, in_specs, out_specs, out_shapes


def _cast_blocks(in_refs, out_refs):
    for src_ref, dst_ref in zip(in_refs, out_refs):
        dst_ref[...] = src_ref[...].astype(BF16)


def _rope_kernel(n_cast, pos_ref, freq_ref, *refs):
    cast_in = refs[:n_cast]
    cos_ref, sin_ref = refs[n_cast:n_cast + 2]
    _cast_blocks(cast_in, refs[n_cast + 2:])
    lane = lax.broadcasted_iota(jnp.int32, cos_ref.shape, 1)
    pos = pos_ref[...]
    ang = jnp.where(lane < RET_DK // 2, pos[:, 0:1], pos[:, 1:2]) * freq_ref[...]
    cos_ref[...] = jnp.cos(ang)
    sin_ref[...] = jnp.sin(ang)


def _rope_tables(positions, token_tile, cast_params=(), cast_layer=0):
    t = positions.size
    half_tile = token_tile // 2
    half = RET_DK // 2
    inv_freq = ROPE_BASE ** (-jnp.arange(half, dtype=F32) / half)
    freq2 = jnp.concatenate([inv_freq, inv_freq]).reshape(1, RET_DK)
    pos = positions.astype(F32).reshape(t // token_tile, 2, half_tile)
    pos = jnp.swapaxes(pos, 1, 2).reshape(t // 2, 2)
    rows = min(ROPE_TILE, t // 2)
    steps = t // 2 // rows
    cast_views, cast_in_specs, cast_out_specs, cast_shapes = _cast_specs(
        cast_params, steps, lambda i: i, layer=cast_layer)
    outs = pl.pallas_call(
        functools.partial(_rope_kernel, len(cast_params)),
        grid=(steps,),
        in_specs=[pl.BlockSpec((rows, 2), lambda i: (i, 0)),
                  pl.BlockSpec((1, RET_DK), lambda i: (0, 0))] + cast_in_specs,
        out_specs=[pl.BlockSpec((rows, RET_DK), lambda i: (i, 0))] * 2 + cast_out_specs,
        out_shape=[jax.ShapeDtypeStruct((t // 2, RET_DK), F32)] * 2 + cast_shapes,
        name="rope_tables",
    )(pos, freq2, *cast_views)
    return outs[0], outs[1], [o.reshape((1,) + p.shape[1:]) for o, p in zip(outs[2:], cast_params)]


def _mem_kv_kernel(mem_ref, g_ref, w_ref, kt_ref, v_ref):
    mem_n = _rms_norm(mem_ref[...], g_ref[...]).astype(BF16)
    kv = _dot(mem_n, w_ref[...])
    kt_ref[...] = kv[:, :XATTN_WIDTH].T.astype(BF16)
    v_ref[...] = kv[:, XATTN_WIDTH:].astype(BF16)


def _mem_kv(mem, mem_norm_g, w_mem_kv):
    depth = w_mem_kv.shape[0]
    b, m, d = mem.shape
    return pl.pallas_call(
        _mem_kv_kernel,
        grid=(depth, b),
        in_specs=[pl.BlockSpec((None, m, d), lambda l, i: (i, 0, 0)),
                  pl.BlockSpec((None, 1, d), lambda l, i: (l, 0, 0)),
                  pl.BlockSpec((None, d, 2 * XATTN_WIDTH), lambda l, i: (l, 0, 0))],
        out_specs=[pl.BlockSpec((None, None, XATTN_WIDTH, m), lambda l, i: (l, i, 0, 0)),
                   pl.BlockSpec((None, None, m, XATTN_WIDTH), lambda l, i: (l, i, 0, 0))],
        out_shape=[jax.ShapeDtypeStruct((depth, b, XATTN_WIDTH, m), BF16),
                   jax.ShapeDtypeStruct((depth, b, m, XATTN_WIDTH), BF16)],
        name="mem_kv",
    )(mem, mem_norm_g.reshape(depth, 1, d), w_mem_kv.astype(BF16))


def _pool_pair_weights(pool_w):
    zeros = jnp.zeros_like(pool_w[:, 0::2])
    top = jnp.concatenate([pool_w[:, 0::2], zeros], axis=-1)
    bottom = jnp.concatenate([zeros, pool_w[:, 1::2]], axis=-1)
    return jnp.concatenate([top, bottom], axis=-2)


def _retention_tables():
    c = RET_CHUNK
    log_gamma = np.log(1.0 - 2.0 ** (-5.0 - np.arange(RET_HEADS, dtype=np.float64)))
    idx = np.arange(c, dtype=np.float64)
    diff = idx[:, None] - idx[None, :]
    mask = np.where(diff[None] >= 0, np.exp(diff[None] * log_gamma[:, None, None]), 0.0)
    q_dec = np.exp((idx + 1.0)[None, :] * log_gamma[:, None])
    k_dec = np.exp((c - 1.0 - idx)[None, :] * log_gamma[:, None])
    q_dec = np.broadcast_to(q_dec[:, :, None], (RET_HEADS, c, RET_DK))
    k_dec = np.broadcast_to(k_dec[:, :, None], (RET_HEADS, c, RET_DK))
    chunk_decay = tuple(float(v) for v in np.exp(c * log_gamma))
    mask = np.concatenate([mask[0::2], mask[1::2]], axis=-1)
    as_f32 = lambda a: jnp.asarray(np.ascontiguousarray(a), dtype=F32)
    return as_f32(mask), as_f32(q_dec), as_f32(k_dec), chunk_decay


def _mixer_kernel(chunk_decay, layer, n_sub, n_cast, *refs):
    (x_ref, cos_ref, sin_ref, kt_ref, vm_ref, mask_ref, qdec_ref, kdec_ref,
     gmix_ref, w_in_ref, pool_w_ref, pool_scale_ref, ret_g_ref,
     w_up_pool_ref, w_up_ret_ref, w_up_mem_ref, w_out_ref) = refs[:N_MIXER_INPUTS]
    cast_in = refs[N_MIXER_INPUTS:N_MIXER_INPUTS + n_cast]
    o_ref = refs[N_MIXER_INPUTS + n_cast]
    cast_out = refs[N_MIXER_INPUTS + n_cast + 1:N_MIXER_INPUTS + 2 * n_cast + 1]
    (h_ref, ubuf_ref, state_ref, qb_ref, kbd_ref, kd_ref, lhs_ref, rhs_ref, gsilu_ref,
     ypool_ref, yret_ref, ymem_ref, macc_ref, merged_ref) = refs[N_MIXER_INPUTS + 2 * n_cast + 1:]
    _cast_blocks(cast_in, cast_out)
    gmix = gmix_ref[layer:layer + 1, :]
    pool_scale = pool_scale_ref[layer:layer + 1, :]
    ret_g = ret_g_ref[layer:layer + 1, :]
    tm = x_ref.shape[0] // n_sub
    n_chunks = tm // RET_CHUNK
    seq_step = pl.program_id(1)

    @pl.when(seq_step == 0)
    def _():
        ubuf_ref[0:POOL_HALO, :] = jnp.zeros((POOL_HALO, POOL_WIDTH), F32)
        state_ref[...] = jnp.zeros_like(state_ref)
        kbd_ref[...] = jnp.zeros_like(kbd_ref)

    def tile(sub):
        xr = slice(sub * tm, (sub + 1) * tm)
        tr = slice(sub * (tm // 2), (sub + 1) * (tm // 2))
        tile_start = (seq_step * n_sub + sub) * tm
        half_rows = (slice(0, tm // 2), slice(tm // 2, tm))
        for hr in half_rows:
            h_ref[hr, :] = _rms_norm(x_ref[sub * tm + hr.start:sub * tm + hr.stop, :],
                                     gmix).astype(BF16)

        def proj(off, width):
            return _dot(h_ref[...], w_in_ref[:, off:off + width])

        def chunk_rows(c):
            return slice(c * RET_CHUNK, (c + 1) * RET_CHUNK)

        def merge_chunk(b, y_ref, w_up_ref, j):
            cols = slice(j * MERGE_CHUNK, (j + 1) * MERGE_CHUNK)
            g = jax.nn.sigmoid(proj(OFF_GATE + b * D_MODEL + j * MERGE_CHUNK, MERGE_CHUNK))
            return g * _dot(y_ref[...], w_up_ref[:, cols])


        lane = lax.broadcasted_iota(jnp.int32, (tm // 2, RET_DK), 1)
        first = lane < RET_DK // 2
        cos_p = cos_ref[tr, :]
        sin_p = sin_ref[tr, :]
        cos_s = pltpu.roll(cos_p, RET_DK // 2, 1)
        sin_s = pltpu.roll(sin_p, RET_DK // 2, 1)
        cos2 = jnp.concatenate([jnp.where(first, cos_p, cos_s), jnp.where(first, cos_s, cos_p)], axis=0)
        sin2 = jnp.concatenate([jnp.where(first, -sin_p, sin_s), jnp.where(first, -sin_s, sin_p)], axis=0)
        q_all = jnp.concatenate(
            [_dot(h_ref[hr, :], w_in_ref[:, OFF_RQ:OFF_RQ + RET_QK_WIDTH]) for hr in half_rows], axis=0)
        k_all = proj(OFF_RK, RET_QK_WIDTH)
        for hh in range(RET_HEADS):
            qk = slice(hh * RET_DK, (hh + 1) * RET_DK)
            qh = q_all[:, qk]
            kh = k_all[:, qk]
            qr = qh * cos2 + pltpu.roll(qh, RET_DK // 2, 1) * sin2
            kr = (kh * cos2 + pltpu.roll(kh, RET_DK // 2, 1) * sin2) * (RET_DK ** -0.5)
            qb_ref[:, qk] = qr.astype(BF16)
            pair, side = divmod(hh, 2)
            diag = slice(side * RET_DK, (side + 1) * RET_DK)
            v_h = proj(OFF_RV + hh * RET_DV, RET_DV).astype(BF16)
            for c in range(n_chunks):
                rows = chunk_rows(c)
                lhs_ref[hh, rows, RET_CHUNK:] = (qr[rows] * qdec_ref[hh]).astype(BF16)
                kd_ref[rows, qk] = (kr[rows] * kdec_ref[hh]).astype(BF16)
                kbd_ref[c * HEAD_PAIRS + pair, diag, diag] = kr[rows].astype(BF16)
                rhs_ref[hh * n_chunks + c, 0:RET_CHUNK, :] = v_h[rows]

        ubuf_ref[POOL_HALO:POOL_HALO + tm, :] = proj(OFF_POOL, POOL_WIDTH)
        row = lax.broadcasted_iota(jnp.int32, (tm, 1), 0) + tile_start
        dwins = []
        for gi, w in enumerate(POOL_WINDOWS):
            cols = slice(gi * POOL_GROUP_DIM, (gi + 1) * POOL_GROUP_DIM)
            win = ubuf_ref[:, cols]
            shift = 1
            while shift < w:
                win = win + pltpu.roll(win, shift, 0)
                shift *= 2
            cnt = jnp.minimum(row + 1, w).astype(F32)
            dwin = win[POOL_HALO:, :] / cnt - ubuf_ref[POOL_HALO:POOL_HALO + tm, cols]
            dwins.append(dwin.astype(BF16))
            if gi % 2 == 1:
                cols2 = slice((gi - 1) * POOL_GROUP_DIM, (gi + 1) * POOL_GROUP_DIM)
                yg = _dot(jnp.concatenate(dwins[gi - 1:gi + 1], axis=1), pool_w_ref[gi // 2])
                ypool_ref[:, cols2] = (yg * pool_scale[:, cols2]).astype(BF16)
            vv = slice(gi * RET_DV, (gi + 1) * RET_DV)
            g_h = proj(OFF_RG + gi * RET_DV, RET_DV)
            gsilu_ref[:, vv] = g_h * jax.nn.sigmoid(g_h) * ret_g[:, vv]
        ubuf_ref[0:POOL_HALO, :] = ubuf_ref[tm:tm + POOL_HALO, :]

        for pair in range(HEAD_PAIRS):
            qk2 = slice(2 * pair * RET_DK, (2 * pair + 2) * RET_DK)
            for c in range(n_chunks):
                rows = chunk_rows(c)
                scores = lax.dot_general(qb_ref[rows, qk2], kbd_ref[c * HEAD_PAIRS + pair],
                                         (((1,), (1,)), ((), ())),
                                         preferred_element_type=F32) * mask_ref[pair]
                scores = scores.astype(BF16)
                lhs_ref[2 * pair, rows, 0:RET_CHUNK] = scores[:, :RET_CHUNK]
                lhs_ref[2 * pair + 1, rows, 0:RET_CHUNK] = scores[:, RET_CHUNK:]

        for hh in range(RET_HEADS):
            qk = slice(hh * RET_DK, (hh + 1) * RET_DK)
            kvs = [lax.dot_general(kd_ref[chunk_rows(c), qk], rhs_ref[hh * n_chunks + c, 0:RET_CHUNK, :],
                                   (((0,), (0,)), ((), ())), preferred_element_type=F32)
                   for c in range(n_chunks)]
            s = state_ref[hh]
            for c in range(n_chunks):
                rhs_ref[hh * n_chunks + c, RET_CHUNK:, :] = s.astype(BF16)
                s = chunk_decay[hh] * s + kvs[c]
            state_ref[hh] = s

        xq = proj(OFF_XQ, XATTN_WIDTH).astype(BF16)

        for hh in range(RET_HEADS):
            vv = slice(hh * RET_DV, (hh + 1) * RET_DV)
            for c in range(n_chunks):
                rows = chunk_rows(c)
                y = _dot(lhs_ref[hh, rows, :], rhs_ref[hh * n_chunks + c])
                y = y * lax.rsqrt(jnp.mean(y * y, axis=-1, keepdims=True) + NORM_EPS)
                yret_ref[rows, vv] = (gsilu_ref[rows, vv] * y).astype(BF16)
            mc = slice(hh * MERGE_CHUNK, (hh + 1) * MERGE_CHUNK)
            macc_ref[:, mc] = merge_chunk(0, ypool_ref, w_up_pool_ref, hh)

        logits_ref, probs_ref = gsilu_ref, lhs_ref
        assert logits_ref.shape == (tm, XATTN_HEADS * N_MEM) and probs_ref.shape == (XATTN_HEADS, tm, N_MEM)
        for hh in range(XATTN_HEADS):
            hd = slice(hh * XATTN_DH, (hh + 1) * XATTN_DH)
            logits_ref[:, hh * N_MEM:(hh + 1) * N_MEM] = _dot(xq[:, hd], kt_ref[hd, :])
        for hh in range(XATTN_HEADS):
            s = logits_ref[:, hh * N_MEM:(hh + 1) * N_MEM]
            p = jnp.exp2((s - jnp.max(s, axis=-1, keepdims=True)) * SOFTMAX_EXP2_SCALE)
            probs_ref[hh] = (p / jnp.sum(p, axis=-1, keepdims=True)).astype(BF16)
            mc = slice(hh * MERGE_CHUNK, (hh + 1) * MERGE_CHUNK)
            macc_ref[:, mc] = macc_ref[:, mc] + merge_chunk(1, yret_ref, w_up_ret_ref, hh)
        for hh in range(XATTN_HEADS):
            hd = slice(hh * XATTN_DH, (hh + 1) * XATTN_DH)
            ymem_ref[:, hd] = _dot(probs_ref[hh], vm_ref[:, hd]).astype(BF16)

        for j in range(D_MODEL // MERGE_CHUNK):
            mc = slice(j * MERGE_CHUNK, (j + 1) * MERGE_CHUNK)
            merged_ref[:, mc] = (macc_ref[:, mc]
                                 + merge_chunk(2, ymem_ref, w_up_mem_ref, j)).astype(BF16)
        o_ref[xr, :] = x_ref[xr, :] + _dot(merged_ref[...], w_out_ref[...])

    for sub in range(n_sub):
        tile(sub)


def _mixer_layer(layer, x2d, batch, cos2, sin2, kt, vm, tables, gmix, w_in, pool_w, pool_scale,
                 ret_g, w_up_pool, w_up_ret, w_up_mem, w_out, cast_params=(), weight_layer=0):
    t, d = x2d.shape
    seq = t // batch
    tm = min(TOKEN_TILE, seq)
    n_sub = MIXER_SUBTILES if seq % (MIXER_SUBTILES * tm) == 0 else 1
    bm = n_sub * tm
    steps = seq // bm
    mask, q_dec, k_dec, chunk_decay = tables
    tok = lambda b, s: (b * steps + s, 0)
    in_specs = [
        pl.BlockSpec((bm, d), tok),
        pl.BlockSpec((bm // 2, RET_DK), tok),
        pl.BlockSpec((bm // 2, RET_DK), tok),
        pl.BlockSpec((None, None, XATTN_WIDTH, N_MEM), lambda b, s: (layer, b, 0, 0)),
        pl.BlockSpec((None, None, N_MEM, XATTN_WIDTH), lambda b, s: (layer, b, 0, 0)),
        _const_spec(mask.shape), _const_spec(q_dec.shape), _const_spec(k_dec.shape),
        _const_spec(gmix.shape), _layer_spec(w_in.shape, weight_layer),
        _layer_spec(pool_w.shape, weight_layer), _const_spec(pool_scale.shape),
        _const_spec(ret_g.shape),
    ] + [_layer_spec(w.shape, weight_layer) for w in (w_up_pool, w_up_ret, w_up_mem, w_out)]
    assert len(in_specs) == N_MIXER_INPUTS
    cast_views, cast_in_specs, cast_out_specs, cast_shapes = _cast_specs(
        cast_params, batch * steps, lambda b, s: b * steps + s, layer=layer)
    outs = pl.pallas_call(
        functools.partial(_mixer_kernel, chunk_decay, layer, n_sub, len(cast_params)),
        grid=(batch, steps),
        in_specs=in_specs + cast_in_specs,
        out_specs=[pl.BlockSpec((bm, d), tok)] + cast_out_specs,
        out_shape=[jax.ShapeDtypeStruct((t, d), F32)] + cast_shapes,
        scratch_shapes=[
            pltpu.VMEM((tm, d), BF16),
            pltpu.VMEM((POOL_HALO + tm, POOL_WIDTH), F32),
            pltpu.VMEM((RET_HEADS, RET_DK, RET_DV), F32),
            pltpu.VMEM((tm, RET_QK_WIDTH), BF16),
            pltpu.VMEM(((tm // RET_CHUNK) * HEAD_PAIRS, 2 * RET_CHUNK, 2 * RET_DK), BF16),
            pltpu.VMEM((tm, RET_QK_WIDTH), BF16),
            pltpu.VMEM((RET_HEADS, tm, RET_CHUNK + RET_DK), BF16),
            pltpu.VMEM((RET_HEADS * (tm // RET_CHUNK), RET_CHUNK + RET_DK, RET_DV), BF16),
            pltpu.VMEM((tm, RET_V_WIDTH), F32),
            pltpu.VMEM((tm, POOL_WIDTH), BF16),
            pltpu.VMEM((tm, RET_V_WIDTH), BF16),
            pltpu.VMEM((tm, XATTN_WIDTH), BF16),
            pltpu.VMEM((tm, D_MODEL), F32),
            pltpu.VMEM((tm, D_MODEL), BF16),
        ],
        compiler_params=pltpu.CompilerParams(
            dimension_semantics=("arbitrary", "arbitrary"),
            vmem_limit_bytes=VMEM_LIMIT_BYTES),
        name="mixer",
    )(x2d, cos2, sin2, kt, vm, mask, q_dec, k_dec, gmix, w_in, pool_w, pool_scale, ret_g,
      w_up_pool, w_up_ret, w_up_mem, w_out, *cast_views)
    return outs[0], [o.reshape((1,) + p.shape[1:]) for o, p in zip(outs[1:], cast_params)]


def _mlp_kernel(apply_final_norm, layer, n_cast, *refs):
    x_ref, g_ref, w1_ref, w2_ref, gf_ref = refs[:N_MLP_INPUTS]
    cast_in = refs[N_MLP_INPUTS:N_MLP_INPUTS + n_cast]
    o_ref = refs[N_MLP_INPUTS + n_cast]
    cast_out = refs[N_MLP_INPUTS + n_cast + 1:N_MLP_INPUTS + 2 * n_cast + 1]
    (hid_ref,) = refs[N_MLP_INPUTS + 2 * n_cast + 1:]
    _cast_blocks(cast_in, cast_out)
    g = g_ref[layer:layer + 1, :]
    tm = x_ref.shape[0]
    n_parts = MLP_ROW_PARTS if tm % (MLP_ROW_PARTS * BF16_SUBLANES) == 0 else 1
    part_rows = [slice(i * (tm // n_parts), (i + 1) * (tm // n_parts)) for i in range(n_parts)]
    for rows in part_rows:
        h = _rms_norm(x_ref[rows, :], g).astype(BF16)
        for j in range(D_FF // N_CHUNK):
            nc = slice(j * N_CHUNK, (j + 1) * N_CHUNK)
            a = jnp.maximum(_dot(h, w1_ref[:, nc]), 0.0)
            hid_ref[rows, nc] = (a * a).astype(BF16)
    for rows in part_rows:
        y = x_ref[rows, :] + _dot(hid_ref[rows, :], w2_ref[...])
        if apply_final_norm:
            y = _rms_norm(y, gf_ref[...])
        o_ref[rows, :] = y


def _mlp_layer(layer, x2d, g, w1, w2, g_final, apply_final_norm, cast_params=(), cast_layer=None):
    t, d = x2d.shape
    tm = min(MLP_TOKEN_TILE, t)
    steps = t // tm
    cast_views, cast_in_specs, cast_out_specs, cast_shapes = _cast_specs(
        cast_params, steps, lambda i: i, layer=cast_layer)
    outs = pl.pallas_call(
        functools.partial(_mlp_kernel, apply_final_norm, layer, len(cast_params)),
        grid=(steps,),
        in_specs=[pl.BlockSpec((tm, d), lambda i: (i, 0)),
                  _const_spec(g.shape), _layer_spec(w1.shape, 0),
                  _layer_spec(w2.shape, 0), _const_spec(g_final.shape)] + cast_in_specs,
        out_specs=[pl.BlockSpec((tm, d), lambda i: (i, 0))] + cast_out_specs,
        out_shape=[jax.ShapeDtypeStruct((t, d), F32)] + cast_shapes,
        scratch_shapes=[pltpu.VMEM((tm, D_FF), BF16)],
        compiler_params=pltpu.CompilerParams(
            dimension_semantics=("arbitrary",),
            vmem_limit_bytes=VMEM_LIMIT_BYTES),
        name="mlp",
    )(x2d, g, w1, w2, g_final, *cast_views)
    return outs[0], [o.reshape((1,) + p.shape[1:]) for o, p in zip(outs[1:], cast_params)]


def kernel(x, mem, positions, norm_mix_g, w_in, pool_w, pool_scale, ret_norm_g, mem_norm_g,
           w_mem_kv, w_up_pool, w_up_ret, w_up_mem, w_out, norm_mlp_g, w_mlp1, w_mlp2,
           final_norm_g):
    batch, seq, d = x.shape
    depth = w_in.shape[0]
    assert d == D_MODEL and w_in.shape[-1] == IN_COLS and mem.shape[1] == N_MEM
    assert seq % RET_CHUNK == 0 and seq % min(TOKEN_TILE, seq) == 0

    mixer_w_f32 = (w_in, _pool_pair_weights(pool_w), w_up_pool, w_up_ret, w_up_mem, w_out)
    cos2, sin2, mixer_w = _rope_tables(positions, min(TOKEN_TILE, seq), mixer_w_f32, cast_layer=0)
    kt, vm = _mem_kv(mem, mem_norm_g, w_mem_kv)
    tables = _retention_tables()
    g_final = final_norm_g.reshape(1, -1)

    x2d = x.reshape(batch * seq, d)
    for l in range(depth):
        w_in_b, pool_w_b, w_up_pool_b, w_up_ret_b, w_up_mem_b, w_out_b = mixer_w
        x2d, mlp_w = _mixer_layer(
            l, x2d, batch, cos2, sin2, kt, vm, tables, norm_mix_g, w_in_b, pool_w_b, pool_scale,
            ret_norm_g, w_up_pool_b, w_up_ret_b, w_up_mem_b, w_out_b,
            cast_params=(w_mlp1, w_mlp2), weight_layer=0)
        last = l == depth - 1
        x2d, mixer_w = _mlp_layer(
            l, x2d, norm_mlp_g, mlp_w[0], mlp_w[1], g_final, apply_final_norm=last,
            cast_params=() if last else mixer_w_f32, cast_layer=None if last else l + 1)
    return x2d.reshape(batch, seq, d)
```

```python
import functools

import numpy as np
import jax
import jax.numpy as jnp
from jax import lax
from jax.experimental import pallas as pl
from jax.experimental.pallas import tpu as pltpu

D_MODEL = 1024
N_MEM = 256
POOL_WINDOWS = (2, 4, 8, 16)
POOL_GROUPS = 4
POOL_GROUP_DIM = 128
POOL_WIDTH = POOL_GROUPS * POOL_GROUP_DIM
POOL_HALO = 16
RET_HEADS = 4
RET_DK = 128
RET_DV = 256
RET_QK_WIDTH = RET_HEADS * RET_DK
RET_V_WIDTH = RET_HEADS * RET_DV
RET_CHUNK = 128
HEAD_PAIRS = RET_HEADS // 2
ROPE_BASE = 10000.0
XATTN_HEADS = 4
XATTN_DH = 128
XATTN_WIDTH = XATTN_HEADS * XATTN_DH
D_FF = 4 * D_MODEL
NORM_EPS = 1e-6

OFF_POOL = 0
OFF_RQ = OFF_POOL + POOL_WIDTH
OFF_RK = OFF_RQ + RET_QK_WIDTH
OFF_RV = OFF_RK + RET_QK_WIDTH
OFF_RG = OFF_RV + RET_V_WIDTH
OFF_XQ = OFF_RG + RET_V_WIDTH
OFF_GATE = OFF_XQ + XATTN_WIDTH
IN_COLS = OFF_GATE + 3 * D_MODEL

TOKEN_TILE = 512
MIXER_SUBTILES = 1
MIXER_NORM_PARTS = 2
MLP_TOKEN_TILE = 1024
MLP_ROW_PARTS = 2
ROPE_TILE = 2048
N_CHUNK = 512
MERGE_CHUNK = 256
VMEM_LIMIT_BYTES = 60 * 1024 * 1024
BF16_SUBLANES = 16
N_MIXER_INPUTS = 17
N_MLP_INPUTS = 5

SOFTMAX_EXP2_SCALE = float(XATTN_DH ** -0.5 * np.log2(np.e))

BF16 = jnp.bfloat16
F32 = jnp.float32


def _dot(a, b):
    return jnp.dot(a, b, preferred_element_type=F32)


def _rms_norm(x, g):
    return x * lax.rsqrt(jnp.mean(x * x, axis=-1, keepdims=True) + NORM_EPS) * g


def _const_spec(shape):
    nd = len(shape)
    return pl.BlockSpec(shape, lambda *_: (0,) * nd, pipeline_mode=pl.Buffered(1))


def _layer_spec(stacked_shape, layer):
    nd = len(stacked_shape) - 1
    return pl.BlockSpec((None,) + tuple(stacked_shape[1:]), lambda *_: (layer,) + (0,) * nd,
                        pipeline_mode=pl.Buffered(1))


def _cast_specs(params, n_steps, index_of_step, layer=None):
    views, in_specs, out_specs, out_shapes = [], [], [], []
    for p in params:
        cols = p.shape[-1]
        rows_per_layer = int(np.prod(p.shape[1:-1]))
        rows = rows_per_layer if layer is not None else p.shape[0] * rows_per_layer
        slab, rem = divmod(rows, n_steps)
        assert rem == 0 and slab % BF16_SUBLANES == 0, (p.shape, n_steps)
        first = 0 if layer is None else layer * n_steps
        views.append(p.reshape(-1, cols))
        in_specs.append(pl.BlockSpec((slab, cols), lambda *g, f=first: (f + index_of_step(*g), 0)))
        out_specs.append(pl.BlockSpec((slab, cols), lambda *g: (index_of_step(*g), 0)))
        out_shapes.append(jax.ShapeDtypeStruct((rows, cols), BF16))
    return views, in_specs, out_specs, out_shapes


def _cast_blocks(in_refs, out_refs):
    for src_ref, dst_ref in zip(in_refs, out_refs):
        dst_ref[...] = src_ref[...].astype(BF16)


def _rope_kernel(n_cast, pos_ref, freq_ref, *refs):
    cast_in = refs[:n_cast]
    cos_ref, sin_ref = refs[n_cast:n_cast + 2]
    _cast_blocks(cast_in, refs[n_cast + 2:])
    lane = lax.broadcasted_iota(jnp.int32, cos_ref.shape, 1)
    pos = pos_ref[...]
    ang = jnp.where(lane < RET_DK // 2, pos[:, 0:1], pos[:, 1:2]) * freq_ref[...]
    cos_ref[...] = jnp.cos(ang)
    sin_ref[...] = jnp.sin(ang)


def _rope_tables(positions, token_tile, cast_params=(), cast_layer=0):
    t = positions.size
    half_tile = token_tile // 2
    half = RET_DK // 2
    inv_freq = ROPE_BASE ** (-jnp.arange(half, dtype=F32) / half)
    freq2 = jnp.concatenate([inv_freq, inv_freq]).reshape(1, RET_DK)
    pos = positions.astype(F32).reshape(t // token_tile, 2, half_tile)
    pos = jnp.swapaxes(pos, 1, 2).reshape(t // 2, 2)
    rows = min(ROPE_TILE, t // 2)
    steps = t // 2 // rows
    cast_views, cast_in_specs, cast_out_specs, cast_shapes = _cast_specs(
        cast_params, steps, lambda i: i, layer=cast_layer)
    outs = pl.pallas_call(
        functools.partial(_rope_kernel, len(cast_params)),
        grid=(steps,),
        in_specs=[pl.BlockSpec((rows, 2), lambda i: (i, 0)),
                  pl.BlockSpec((1, RET_DK), lambda i: (0, 0))] + cast_in_specs,
        out_specs=[pl.BlockSpec((rows, RET_DK), lambda i: (i, 0))] * 2 + cast_out_specs,
        out_shape=[jax.ShapeDtypeStruct((t // 2, RET_DK), F32)] * 2 + cast_shapes,
        name="rope_tables",
    )(pos, freq2, *cast_views)
    return outs[0], outs[1], [o.reshape((1,) + p.shape[1:]) for o, p in zip(outs[2:], cast_params)]


def _mem_kv_kernel(mem_ref, g_ref, w_ref, kt_ref, v_ref):
    w = w_ref[...].astype(BF16)
    for i in range(mem_ref.shape[0]):
        mem_n = _rms_norm(mem_ref[i], g_ref[...]).astype(BF16)
        kv = _dot(mem_n, w)
        kt_ref[i] = kv[:, :XATTN_WIDTH].T.astype(BF16)
        v_ref[i] = kv[:, XATTN_WIDTH:].astype(BF16)


def _mem_kv(mem, mem_norm_g, w_mem_kv):
    depth = w_mem_kv.shape[0]
    b, m, d = mem.shape
    return pl.pallas_call(
        _mem_kv_kernel,
        grid=(depth,),
        in_specs=[pl.BlockSpec((b, m, d), lambda l: (0, 0, 0)),
                  pl.BlockSpec((None, 1, d), lambda l: (l, 0, 0)),
                  pl.BlockSpec((None, d, 2 * XATTN_WIDTH), lambda l: (l, 0, 0))],
        out_specs=[pl.BlockSpec((None, b, XATTN_WIDTH, m), lambda l: (l, 0, 0, 0)),
                   pl.BlockSpec((None, b, m, XATTN_WIDTH), lambda l: (l, 0, 0, 0))],
        out_shape=[jax.ShapeDtypeStruct((depth, b, XATTN_WIDTH, m), BF16),
                   jax.ShapeDtypeStruct((depth, b, m, XATTN_WIDTH), BF16)],
        name="mem_kv",
    )(mem, mem_norm_g.reshape(depth, 1, d), w_mem_kv)


def _pool_pair_weights(pool_w):
    zeros = jnp.zeros_like(pool_w[:, 0::2])
    top = jnp.concatenate([pool_w[:, 0::2], zeros], axis=-1)
    bottom = jnp.concatenate([zeros, pool_w[:, 1::2]], axis=-1)
    return jnp.concatenate([top, bottom], axis=-2)


def _retention_tables():
    c = RET_CHUNK
    log_gamma = np.log(1.0 - 2.0 ** (-5.0 - np.arange(RET_HEADS, dtype=np.float64)))
    idx = np.arange(c, dtype=np.float64)
    diff = idx[:, None] - idx[None, :]
    mask = np.where(diff[None] >= 0, np.exp(diff[None] * log_gamma[:, None, None]), 0.0)
    q_dec = np.exp((idx + 1.0)[None, :] * log_gamma[:, None])
    k_dec = np.exp((c - 1.0 - idx)[None, :] * log_gamma[:, None])
    q_dec = np.broadcast_to(q_dec[:, :, None], (RET_HEADS, c, RET_DK))
    k_dec = np.broadcast_to(k_dec[:, :, None], (RET_HEADS, c, RET_DK))
    chunk_decay = tuple(float(v) for v in np.exp(c * log_gamma))
    mask = np.concatenate([mask[0::2], mask[1::2]], axis=-1)
    as_f32 = lambda a: jnp.asarray(np.ascontiguousarray(a), dtype=F32)
    return as_f32(mask), as_f32(q_dec), as_f32(k_dec), chunk_decay


def _mixer_kernel(chunk_decay, layer, n_sub, n_cast, *refs):
    (x_ref, cos_ref, sin_ref, kt_ref, vm_ref, mask_ref, qdec_ref, kdec_ref,
     gmix_ref, w_in_ref, pool_w_ref, pool_scale_ref, ret_g_ref,
     w_up_pool_ref, w_up_ret_ref, w_up_mem_ref, w_out_ref) = refs[:N_MIXER_INPUTS]
    cast_in = refs[N_MIXER_INPUTS:N_MIXER_INPUTS + n_cast]
    o_ref = refs[N_MIXER_INPUTS + n_cast]
    cast_out = refs[N_MIXER_INPUTS + n_cast + 1:N_MIXER_INPUTS + 2 * n_cast + 1]
    (h_ref, ubuf_ref, state_ref, qb_ref, kbd_ref, kd_ref, lhs_ref, rhs_ref, gsilu_ref,
     ypool_ref, yret_ref, ymem_ref, macc_ref, merged_ref) = refs[N_MIXER_INPUTS + 2 * n_cast + 1:]
    _cast_blocks(cast_in, cast_out)
    gmix = gmix_ref[layer:layer + 1, :]
    pool_scale = pool_scale_ref[layer:layer + 1, :]
    ret_g = ret_g_ref[layer:layer + 1, :]
    tm = x_ref.shape[0] // n_sub
    n_chunks = tm // RET_CHUNK
    seq_step = pl.program_id(1)

    @pl.when(seq_step == 0)
    def _():
        ubuf_ref[0:POOL_HALO, :] = jnp.zeros((POOL_HALO, POOL_WIDTH), F32)
        state_ref[...] = jnp.zeros_like(state_ref)
        kbd_ref[...] = jnp.zeros_like(kbd_ref)

    def tile(sub):
        xr = slice(sub * tm, (sub + 1) * tm)
        tr = slice(sub * (tm // 2), (sub + 1) * (tm // 2))
        tile_start = (seq_step * n_sub + sub) * tm
        part = tm // MIXER_NORM_PARTS
        half_rows = tuple(slice(i * part, (i + 1) * part) for i in range(MIXER_NORM_PARTS))
        for hr in half_rows:
            h_ref[hr, :] = _rms_norm(x_ref[sub * tm + hr.start:sub * tm + hr.stop, :],
                                     gmix).astype(BF16)

        def proj(off, width):
            return _dot(h_ref[...], w_in_ref[:, off:off + width])

        def chunk_rows(c):
            return slice(c * RET_CHUNK, (c + 1) * RET_CHUNK)

        def merge_chunk(b, y_ref, w_up_ref, j):
            cols = slice(j * MERGE_CHUNK, (j + 1) * MERGE_CHUNK)
            g = jax.nn.sigmoid(proj(OFF_GATE + b * D_MODEL + j * MERGE_CHUNK, MERGE_CHUNK))
            return g * _dot(y_ref[...], w_up_ref[:, cols])


        lane = lax.broadcasted_iota(jnp.int32, (tm // 2, RET_DK), 1)
        first = lane < RET_DK // 2
        cos_p = cos_ref[tr, :]
        sin_p = sin_ref[tr, :]
        cos_s = pltpu.roll(cos_p, RET_DK // 2, 1)
        sin_s = pltpu.roll(sin_p, RET_DK // 2, 1)
        cos2 = jnp.concatenate([jnp.where(first, cos_p, cos_s), jnp.where(first, cos_s, cos_p)], axis=0)
        sin2 = jnp.concatenate([jnp.where(first, -sin_p, sin_s), jnp.where(first, -sin_s, sin_p)], axis=0)
        q_all = jnp.concatenate(
            [_dot(h_ref[hr, :], w_in_ref[:, OFF_RQ:OFF_RQ + RET_QK_WIDTH]) for hr in half_rows], axis=0)
        k_all = proj(OFF_RK, RET_QK_WIDTH)
        for hh in range(RET_HEADS):
            qk = slice(hh * RET_DK, (hh + 1) * RET_DK)
            qh = q_all[:, qk]
            kh = k_all[:, qk]
            qr = qh * cos2 + pltpu.roll(qh, RET_DK // 2, 1) * sin2
            kr = (kh * cos2 + pltpu.roll(kh, RET_DK // 2, 1) * sin2) * (RET_DK ** -0.5)
            qb_ref[:, qk] = qr.astype(BF16)
            pair, side = divmod(hh, 2)
            diag = slice(side * RET_DK, (side + 1) * RET_DK)
            v_h = proj(OFF_RV + hh * RET_DV, RET_DV).astype(BF16)
            for c in range(n_chunks):
                rows = chunk_rows(c)
                lhs_ref[hh, rows, RET_CHUNK:] = (qr[rows] * qdec_ref[hh]).astype(BF16)
                kd_ref[rows, qk] = (kr[rows] * kdec_ref[hh]).astype(BF16)
                kbd_ref[c * HEAD_PAIRS + pair, diag, diag] = kr[rows].astype(BF16)
                rhs_ref[hh * n_chunks + c, 0:RET_CHUNK, :] = v_h[rows]

        ubuf_ref[POOL_HALO:POOL_HALO + tm, :] = proj(OFF_POOL, POOL_WIDTH)
        row = lax.broadcasted_iota(jnp.int32, (tm, 1), 0) + tile_start
        dwins = []
        for gi, w in enumerate(POOL_WINDOWS):
            cols = slice(gi * POOL_GROUP_DIM, (gi + 1) * POOL_GROUP_DIM)
            win = ubuf_ref[:, cols]
            shift = 1
            while shift < w:
                win = win + pltpu.roll(win, shift, 0)
                shift *= 2
            cnt = jnp.minimum(row + 1, w).astype(F32)
            dwin = win[POOL_HALO:, :] / cnt - ubuf_ref[POOL_HALO:POOL_HALO + tm, cols]
            dwins.append(dwin.astype(BF16))
            if gi % 2 == 1:
                cols2 = slice((gi - 1) * POOL_GROUP_DIM, (gi + 1) * POOL_GROUP_DIM)
                yg = _dot(jnp.concatenate(dwins[gi - 1:gi + 1], axis=1), pool_w_ref[gi // 2])
                ypool_ref[:, cols2] = (yg * pool_scale[:, cols2]).astype(BF16)
            vv = slice(gi * RET_DV, (gi + 1) * RET_DV)
            g_h = proj(OFF_RG + gi * RET_DV, RET_DV)
            gsilu_ref[:, vv] = g_h * jax.nn.sigmoid(g_h) * ret_g[:, vv]
        ubuf_ref[0:POOL_HALO, :] = ubuf_ref[tm:tm + POOL_HALO, :]

        for pair in range(HEAD_PAIRS):
            qk2 = slice(2 * pair * RET_DK, (2 * pair + 2) * RET_DK)
            for c in range(n_chunks):
                rows = chunk_rows(c)
                scores = lax.dot_general(qb_ref[rows, qk2], kbd_ref[c * HEAD_PAIRS + pair],
                                         (((1,), (1,)), ((), ())),
                                         preferred_element_type=F32) * mask_ref[pair]
                scores = scores.astype(BF16)
                lhs_ref[2 * pair, rows, 0:RET_CHUNK] = scores[:, :RET_CHUNK]
                lhs_ref[2 * pair + 1, rows, 0:RET_CHUNK] = scores[:, RET_CHUNK:]

        for hh in range(RET_HEADS):
            qk = slice(hh * RET_DK, (hh + 1) * RET_DK)
            kvs = [lax.dot_general(kd_ref[chunk_rows(c), qk], rhs_ref[hh * n_chunks + c, 0:RET_CHUNK, :],
                                   (((0,), (0,)), ((), ())), preferred_element_type=F32)
                   for c in range(n_chunks)]
            s = state_ref[hh]
            for c in range(n_chunks):
                rhs_ref[hh * n_chunks + c, RET_CHUNK:, :] = s.astype(BF16)
                s = chunk_decay[hh] * s + kvs[c]
            state_ref[hh] = s

        xq = proj(OFF_XQ, XATTN_WIDTH).astype(BF16)

        for hh in range(RET_HEADS):
            vv = slice(hh * RET_DV, (hh + 1) * RET_DV)
            for c in range(n_chunks):
                rows = chunk_rows(c)
                y = _dot(lhs_ref[hh, rows, :], rhs_ref[hh * n_chunks + c])
                y = y * lax.rsqrt(jnp.mean(y * y, axis=-1, keepdims=True) + NORM_EPS)
                yret_ref[rows, vv] = (gsilu_ref[rows, vv] * y).astype(BF16)
            mc = slice(hh * MERGE_CHUNK, (hh + 1) * MERGE_CHUNK)
            macc_ref[:, mc] = merge_chunk(0, ypool_ref, w_up_pool_ref, hh)

        logits_ref, probs_ref = gsilu_ref, lhs_ref
        assert logits_ref.shape == (tm, XATTN_HEADS * N_MEM) and probs_ref.shape == (XATTN_HEADS, tm, N_MEM)
        for hh in range(XATTN_HEADS):
            hd = slice(hh * XATTN_DH, (hh + 1) * XATTN_DH)
            logits_ref[:, hh * N_MEM:(hh + 1) * N_MEM] = _dot(xq[:, hd], kt_ref[hd, :])
        for hh in range(XATTN_HEADS):
            s = logits_ref[:, hh * N_MEM:(hh + 1) * N_MEM]
            p = jnp.exp2((s - jnp.max(s, axis=-1, keepdims=True)) * SOFTMAX_EXP2_SCALE)
            probs_ref[hh] = (p / jnp.sum(p, axis=-1, keepdims=True)).astype(BF16)
            mc = slice(hh * MERGE_CHUNK, (hh + 1) * MERGE_CHUNK)
            macc_ref[:, mc] = macc_ref[:, mc] + merge_chunk(1, yret_ref, w_up_ret_ref, hh)
        for hh in range(XATTN_HEADS):
            hd = slice(hh * XATTN_DH, (hh + 1) * XATTN_DH)
            ymem_ref[:, hd] = _dot(probs_ref[hh], vm_ref[:, hd]).astype(BF16)

        for j in range(D_MODEL // MERGE_CHUNK):
            mc = slice(j * MERGE_CHUNK, (j + 1) * MERGE_CHUNK)
            merged_ref[:, mc] = (macc_ref[:, mc]
                                 + merge_chunk(2, ymem_ref, w_up_mem_ref, j)).astype(BF16)
        o_ref[xr, :] = x_ref[xr, :] + _dot(merged_ref[...], w_out_ref[...])

    for sub in range(n_sub):
        tile(sub)


def _mixer_layer(layer, x2d, batch, cos2, sin2, kt, vm, tables, gmix, w_in, pool_w, pool_scale,
                 ret_g, w_up_pool, w_up_ret, w_up_mem, w_out, cast_params=(), weight_layer=0):
    t, d = x2d.shape
    seq = t // batch
    tm = min(TOKEN_TILE, seq)
    n_sub = MIXER_SUBTILES if seq % (MIXER_SUBTILES * tm) == 0 else 1
    bm = n_sub * tm
    steps = seq // bm
    mask, q_dec, k_dec, chunk_decay = tables
    tok = lambda b, s: (b * steps + s, 0)
    in_specs = [
        pl.BlockSpec((bm, d), tok),
        pl.BlockSpec((bm // 2, RET_DK), tok),
        pl.BlockSpec((bm // 2, RET_DK), tok),
        pl.BlockSpec((None, None, XATTN_WIDTH, N_MEM), lambda b, s: (layer, b, 0, 0)),
        pl.BlockSpec((None, None, N_MEM, XATTN_WIDTH), lambda b, s: (layer, b, 0, 0)),
        _const_spec(mask.shape), _const_spec(q_dec.shape), _const_spec(k_dec.shape),
        _const_spec(gmix.shape), _layer_spec(w_in.shape, weight_layer),
        _layer_spec(pool_w.shape, weight_layer), _const_spec(pool_scale.shape),
        _const_spec(ret_g.shape),
    ] + [_layer_spec(w.shape, weight_layer) for w in (w_up_pool, w_up_ret, w_up_mem, w_out)]
    assert len(in_specs) == N_MIXER_INPUTS
    cast_views, cast_in_specs, cast_out_specs, cast_shapes = _cast_specs(
        cast_params, batch * steps, lambda b, s: b * steps + s, layer=layer)
    outs = pl.pallas_call(
        functools.partial(_mixer_kernel, chunk_decay, layer, n_sub, len(cast_params)),
        grid=(batch, steps),
        in_specs=in_specs + cast_in_specs,
        out_specs=[pl.BlockSpec((bm, d), tok)] + cast_out_specs,
        out_shape=[jax.ShapeDtypeStruct((t, d), F32)] + cast_shapes,
        scratch_shapes=[
            pltpu.VMEM((tm, d), BF16),
            pltpu.VMEM((POOL_HALO + tm, POOL_WIDTH), F32),
            pltpu.VMEM((RET_HEADS, RET_DK, RET_DV), F32),
            pltpu.VMEM((tm, RET_QK_WIDTH), BF16),
            pltpu.VMEM(((tm // RET_CHUNK) * HEAD_PAIRS, 2 * RET_CHUNK, 2 * RET_DK), BF16),
            pltpu.VMEM((tm, RET_QK_WIDTH), BF16),
            pltpu.VMEM((RET_HEADS, tm, RET_CHUNK + RET_DK), BF16),
            pltpu.VMEM((RET_HEADS * (tm // RET_CHUNK), RET_CHUNK + RET_DK, RET_DV), BF16),
            pltpu.VMEM((tm, RET_V_WIDTH), F32),
            pltpu.VMEM((tm, POOL_WIDTH), BF16),
            pltpu.VMEM((tm, RET_V_WIDTH), BF16),
            pltpu.VMEM((tm, XATTN_WIDTH), BF16),
            pltpu.VMEM((tm, D_MODEL), F32),
            pltpu.VMEM((tm, D_MODEL), BF16),
        ],
        compiler_params=pltpu.CompilerParams(
            dimension_semantics=("arbitrary", "arbitrary"),
            vmem_limit_bytes=VMEM_LIMIT_BYTES),
        name="mixer",
    )(x2d, cos2, sin2, kt, vm, mask, q_dec, k_dec, gmix, w_in, pool_w, pool_scale, ret_g,
      w_up_pool, w_up_ret, w_up_mem, w_out, *cast_views)
    return outs[0], [o.reshape((1,) + p.shape[1:]) for o, p in zip(outs[1:], cast_params)]


def _mlp_kernel(apply_final_norm, layer, n_cast, *refs):
    x_ref, g_ref, w1_ref, w2_ref, gf_ref = refs[:N_MLP_INPUTS]
    cast_in = refs[N_MLP_INPUTS:N_MLP_INPUTS + n_cast]
    o_ref = refs[N_MLP_INPUTS + n_cast]
    cast_out = refs[N_MLP_INPUTS + n_cast + 1:N_MLP_INPUTS + 2 * n_cast + 1]
    (hid_ref,) = refs[N_MLP_INPUTS + 2 * n_cast + 1:]
    _cast_blocks(cast_in, cast_out)
    g = g_ref[layer:layer + 1, :]
    tm = x_ref.shape[0]
    n_parts = MLP_ROW_PARTS if tm % (MLP_ROW_PARTS * BF16_SUBLANES) == 0 else 1
    part_rows = [slice(i * (tm // n_parts), (i + 1) * (tm // n_parts)) for i in range(n_parts)]
    for rows in part_rows:
        h = _rms_norm(x_ref[rows, :], g).astype(BF16)
        for j in range(D_FF // N_CHUNK):
            nc = slice(j * N_CHUNK, (j + 1) * N_CHUNK)
            a = jnp.maximum(_dot(h, w1_ref[:, nc]), 0.0)
            hid_ref[rows, nc] = (a * a).astype(BF16)
    for rows in part_rows:
        y = x_ref[rows, :] + _dot(hid_ref[rows, :], w2_ref[...])
        if apply_final_norm:
            y = _rms_norm(y, gf_ref[...])
        o_ref[rows, :] = y


def _mlp_layer(layer, x2d, g, w1, w2, g_final, apply_final_norm, cast_params=(), cast_layer=None):
    t, d = x2d.shape
    tm = min(MLP_TOKEN_TILE, t)
    steps = t // tm
    cast_views, cast_in_specs, cast_out_specs, cast_shapes = _cast_specs(
        cast_params, steps, lambda i: i, layer=cast_layer)
    outs = pl.pallas_call(
        functools.partial(_mlp_kernel, apply_final_norm, layer, len(cast_params)),
        grid=(steps,),
        in_specs=[pl.BlockSpec((tm, d), lambda i: (i, 0)),
                  _const_spec(g.shape), _layer_spec(w1.shape, 0),
                  _layer_spec(w2.shape, 0), _const_spec(g_final.shape)] + cast_in_specs,
        out_specs=[pl.BlockSpec((tm, d), lambda i: (i, 0))] + cast_out_specs,
        out_shape=[jax.ShapeDtypeStruct((t, d), F32)] + cast_shapes,
        scratch_shapes=[pltpu.VMEM((tm, D_FF), BF16)],
        compiler_params=pltpu.CompilerParams(
            dimension_semantics=("arbitrary",),
            vmem_limit_bytes=VMEM_LIMIT_BYTES),
        name="mlp",
    )(x2d, g, w1, w2, g_final, *cast_views)
    return outs[0], [o.reshape((1,) + p.shape[1:]) for o, p in zip(outs[1:], cast_params)]


def kernel(x, mem, positions, norm_mix_g, w_in, pool_w, pool_scale, ret_norm_g, mem_norm_g,
           w_mem_kv, w_up_pool, w_up_ret, w_up_mem, w_out, norm_mlp_g, w_mlp1, w_mlp2,
           final_norm_g):
    batch, seq, d = x.shape
    depth = w_in.shape[0]
    assert d == D_MODEL and w_in.shape[-1] == IN_COLS and mem.shape[1] == N_MEM
    assert seq % RET_CHUNK == 0 and seq % min(TOKEN_TILE, seq) == 0

    mixer_w_f32 = (w_in, _pool_pair_weights(pool_w), w_up_pool, w_up_ret, w_up_mem, w_out)
    cos2, sin2, mixer_w = _rope_tables(positions, min(TOKEN_TILE, seq), mixer_w_f32, cast_layer=0)
    kt, vm = _mem_kv(mem, mem_norm_g, w_mem_kv)
    tables = _retention_tables()
    g_final = final_norm_g.reshape(1, -1)

    x2d = x.reshape(batch * seq, d)
    for l in range(depth):
        w_in_b, pool_w_b, w_up_pool_b, w_up_ret_b, w_up_mem_b, w_out_b = mixer_w
        x2d, mlp_w = _mixer_layer(
            l, x2d, batch, cos2, sin2, kt, vm, tables, norm_mix_g, w_in_b, pool_w_b, pool_scale,
            ret_norm_g, w_up_pool_b, w_up_ret_b, w_up_mem_b, w_out_b,
            cast_params=(w_mlp1, w_mlp2), weight_layer=0)
        last = l == depth - 1
        x2d, mixer_w = _mlp_layer(
            l, x2d, norm_mlp_g, mlp_w[0], mlp_w[1], g_final, apply_final_norm=last,
            cast_params=() if last else mixer_w_f32, cast_layer=None if last else l + 1)
    return x2d.reshape(batch, seq, d)
```

```python
import functools

import numpy as np
import jax
import jax.numpy as jnp
from jax import lax
from jax.experimental import pallas as pl
from jax.experimental.pallas import tpu as pltpu

D_MODEL = 1024
N_MEM = 256
POOL_WINDOWS = (2, 4, 8, 16)
POOL_GROUPS = 4
POOL_GROUP_DIM = 128
POOL_WIDTH = POOL_GROUPS * POOL_GROUP_DIM
POOL_HALO = 16
RET_HEADS = 4
RET_DK = 128
RET_DV = 256
RET_QK_WIDTH = RET_HEADS * RET_DK
RET_V_WIDTH = RET_HEADS * RET_DV
RET_CHUNK = 128
HEAD_PAIRS = RET_HEADS // 2
ROPE_BASE = 10000.0
XATTN_HEADS = 4
XATTN_DH = 128
XATTN_WIDTH = XATTN_HEADS * XATTN_DH
D_FF = 4 * D_MODEL
NORM_EPS = 1e-6

OFF_POOL = 0
OFF_RQ = OFF_POOL + POOL_WIDTH
OFF_RK = OFF_RQ + RET_QK_WIDTH
OFF_RV = OFF_RK + RET_QK_WIDTH
OFF_RG = OFF_RV + RET_V_WIDTH
OFF_XQ = OFF_RG + RET_V_WIDTH
OFF_GATE = OFF_XQ + XATTN_WIDTH
IN_COLS = OFF_GATE + 3 * D_MODEL

TOKEN_TILE = 512
MIXER_SUBTILES = 1
MIXER_NORM_PARTS = 2
MLP_TOKEN_TILE = 1024
MLP_ROW_PARTS = 2
ROPE_TILE = 2048
N_CHUNK = 512
MERGE_CHUNK = 256
VMEM_LIMIT_BYTES = 60 * 1024 * 1024
BF16_SUBLANES = 16
N_MIXER_INPUTS = 17
N_MLP_INPUTS = 5

SOFTMAX_EXP2_SCALE = float(XATTN_DH ** -0.5 * np.log2(np.e))

BF16 = jnp.bfloat16
F32 = jnp.float32


def _dot(a, b):
    return jnp.dot(a, b, preferred_element_type=F32)


def _rms_norm(x, g):
    return x * lax.rsqrt(jnp.mean(x * x, axis=-1, keepdims=True) + NORM_EPS) * g


def _const_spec(shape):
    nd = len(shape)
    return pl.BlockSpec(shape, lambda *_: (0,) * nd, pipeline_mode=pl.Buffered(1))


def _layer_spec(stacked_shape, layer):
    nd = len(stacked_shape) - 1
    return pl.BlockSpec((None,) + tuple(stacked_shape[1:]), lambda *_: (layer,) + (0,) * nd,
                        pipeline_mode=pl.Buffered(1))


def _cast_specs(params, n_steps, index_of_step, layer=None):
    views, in_specs, out_specs, out_shapes = [], [], [], []
    for p in params:
        cols = p.shape[-1]
        rows_per_layer = int(np.prod(p.shape[1:-1]))
        rows = rows_per_layer if layer is not None else p.shape[0] * rows_per_layer
        slab, rem = divmod(rows, n_steps)
        assert rem == 0 and slab % BF16_SUBLANES == 0, (p.shape, n_steps)
        first = 0 if layer is None else layer * n_steps
        views.append(p.reshape(-1, cols))
        in_specs.append(pl.BlockSpec((slab, cols), lambda *g, f=first: (f + index_of_step(*g), 0)))
        out_specs.append(pl.BlockSpec((slab, cols), lambda *g: (index_of_step(*g), 0)))
        out_shapes.append(jax.ShapeDtypeStruct((rows, cols), BF16))
    return views, in_specs, out_specs, out_shapes


def _cast_blocks(in_refs, out_refs):
    for src_ref, dst_ref in zip(in_refs, out_refs):
        dst_ref[...] = src_ref[...].astype(BF16)


def _rope_kernel(n_cast, pos_ref, freq_ref, *refs):
    cast_in = refs[:n_cast]
    cos_ref, sin_ref = refs[n_cast:n_cast + 2]
    _cast_blocks(cast_in, refs[n_cast + 2:])
    lane = lax.broadcasted_iota(jnp.int32, cos_ref.shape, 1)
    pos = pos_ref[...]
    ang = jnp.where(lane < RET_DK // 2, pos[:, 0:1], pos[:, 1:2]) * freq_ref[...]
    cos_ref[...] = jnp.cos(ang)
    sin_ref[...] = jnp.sin(ang)


def _rope_tables(positions, token_tile, cast_params=(), cast_layer=0):
    t = positions.size
    half_tile = token_tile // 2
    half = RET_DK // 2
    inv_freq = ROPE_BASE ** (-jnp.arange(half, dtype=F32) / half)
    freq2 = jnp.concatenate([inv_freq, inv_freq]).reshape(1, RET_DK)
    pos = positions.astype(F32).reshape(t // token_tile, 2, half_tile)
    pos = jnp.swapaxes(pos, 1, 2).reshape(t // 2, 2)
    rows = min(ROPE_TILE, t // 2)
    steps = t // 2 // rows
    cast_views, cast_in_specs, cast_out_specs, cast_shapes = _cast_specs(
        cast_params, steps, lambda i: i, layer=cast_layer)
    outs = pl.pallas_call(
        functools.partial(_rope_kernel, len(cast_params)),
        grid=(steps,),
        in_specs=[pl.BlockSpec((rows, 2), lambda i: (i, 0)),
                  pl.BlockSpec((1, RET_DK), lambda i: (0, 0))] + cast_in_specs,
        out_specs=[pl.BlockSpec((rows, RET_DK), lambda i: (i, 0))] * 2 + cast_out_specs,
        out_shape=[jax.ShapeDtypeStruct((t // 2, RET_DK), F32)] * 2 + cast_shapes,
        name="rope_tables",
    )(pos, freq2, *cast_views)
    return outs[0], outs[1], [o.reshape((1,) + p.shape[1:]) for o, p in zip(outs[2:], cast_params)]


def _mem_kv_kernel(mem_ref, g_ref, w_ref, kt_ref, v_ref):
    w = w_ref[...].astype(BF16)
    for i in range(mem_ref.shape[0]):
        mem_n = _rms_norm(mem_ref[i], g_ref[...]).astype(BF16)
        kv = _dot(mem_n, w)
        kt_ref[i] = kv[:, :XATTN_WIDTH].T.astype(BF16)
        v_ref[i] = kv[:, XATTN_WIDTH:].astype(BF16)


def _mem_kv(mem, mem_norm_g, w_mem_kv):
    depth = w_mem_kv.shape[0]
    b, m, d = mem.shape
    return pl.pallas_call(
        _mem_kv_kernel,
        grid=(depth,),
        in_specs=[pl.BlockSpec((b, m, d), lambda l: (0, 0, 0)),
                  pl.BlockSpec((None, 1, d), lambda l: (l, 0, 0)),
                  pl.BlockSpec((None, d, 2 * XATTN_WIDTH), lambda l: (l, 0, 0))],
        out_specs=[pl.BlockSpec((None, b, XATTN_WIDTH, m), lambda l: (l, 0, 0, 0)),
                   pl.BlockSpec((None, b, m, XATTN_WIDTH), lambda l: (l, 0, 0, 0))],
        out_shape=[jax.ShapeDtypeStruct((depth, b, XATTN_WIDTH, m), BF16),
                   jax.ShapeDtypeStruct((depth, b, m, XATTN_WIDTH), BF16)],
        name="mem_kv",
    )(mem, mem_norm_g.reshape(depth, 1, d), w_mem_kv)


def _pool_pair_weights(pool_w):
    zeros = jnp.zeros_like(pool_w[:, 0::2])
    top = jnp.concatenate([pool_w[:, 0::2], zeros], axis=-1)
    bottom = jnp.concatenate([zeros, pool_w[:, 1::2]], axis=-1)
    return jnp.concatenate([top, bottom], axis=-2)


def _retention_tables():
    c = RET_CHUNK
    log_gamma = np.log(1.0 - 2.0 ** (-5.0 - np.arange(RET_HEADS, dtype=np.float64)))
    idx = np.arange(c, dtype=np.float64)
    diff = idx[:, None] - idx[None, :]
    mask = np.where(diff[None] >= 0, np.exp(diff[None] * log_gamma[:, None, None]), 0.0)
    q_dec = np.exp((idx + 1.0)[None, :] * log_gamma[:, None])
    k_dec = np.exp((c - 1.0 - idx)[None, :] * log_gamma[:, None])
    q_dec = np.broadcast_to(q_dec[:, :, None], (RET_HEADS, c, RET_DK))
    k_dec = np.broadcast_to(k_dec[:, :, None], (RET_HEADS, c, RET_DK))
    chunk_decay = tuple(float(v) for v in np.exp(c * log_gamma))
    mask = np.concatenate([mask[0::2], mask[1::2]], axis=-1)
    as_f32 = lambda a: jnp.asarray(np.ascontiguousarray(a), dtype=F32)
    return as_f32(mask), as_f32(q_dec), as_f32(k_dec), chunk_decay


def _mixer_kernel(chunk_decay, layer, n_sub, n_cast, *refs):
    (x_ref, cos_ref, sin_ref, kt_ref, vm_ref, mask_ref, qdec_ref, kdec_ref,
     gmix_ref, w_in_ref, pool_w_ref, pool_scale_ref, ret_g_ref,
     w_up_pool_ref, w_up_ret_ref, w_up_mem_ref, w_out_ref) = refs[:N_MIXER_INPUTS]
    cast_in = refs[N_MIXER_INPUTS:N_MIXER_INPUTS + n_cast]
    o_ref = refs[N_MIXER_INPUTS + n_cast]
    cast_out = refs[N_MIXER_INPUTS + n_cast + 1:N_MIXER_INPUTS + 2 * n_cast + 1]
    (h_ref, ubuf_ref, state_ref, qb_ref, kbd_ref, kd_ref, lhs_ref, rhs_ref, gsilu_ref,
     ypool_ref, yret_ref, ymem_ref, macc_ref, merged_ref) = refs[N_MIXER_INPUTS + 2 * n_cast + 1:]
    _cast_blocks(cast_in, cast_out)
    gmix = gmix_ref[layer:layer + 1, :]
    pool_scale = pool_scale_ref[layer:layer + 1, :]
    ret_g = ret_g_ref[layer:layer + 1, :]
    tm = x_ref.shape[0] // n_sub
    n_chunks = tm // RET_CHUNK
    seq_step = pl.program_id(1)

    @pl.when(seq_step == 0)
    def _():
        ubuf_ref[0:POOL_HALO, :] = jnp.zeros((POOL_HALO, POOL_WIDTH), F32)
        state_ref[...] = jnp.zeros_like(state_ref)
        kbd_ref[...] = jnp.zeros_like(kbd_ref)

    def tile(sub):
        xr = slice(sub * tm, (sub + 1) * tm)
        tr = slice(sub * (tm // 2), (sub + 1) * (tm // 2))
        tile_start = (seq_step * n_sub + sub) * tm
        part = tm // MIXER_NORM_PARTS
        half_rows = tuple(slice(i * part, (i + 1) * part) for i in range(MIXER_NORM_PARTS))
        for hr in half_rows:
            h_ref[hr, :] = _rms_norm(x_ref[sub * tm + hr.start:sub * tm + hr.stop, :],
                                     gmix).astype(BF16)

        def proj(off, width):
            return _dot(h_ref[...], w_in_ref[:, off:off + width])

        def chunk_rows(c):
            return slice(c * RET_CHUNK, (c + 1) * RET_CHUNK)

        def merge_chunk(b, y_ref, w_up_ref, j):
            cols = slice(j * MERGE_CHUNK, (j + 1) * MERGE_CHUNK)
            g = jax.nn.sigmoid(proj(OFF_GATE + b * D_MODEL + j * MERGE_CHUNK, MERGE_CHUNK))
            return g * _dot(y_ref[...], w_up_ref[:, cols])


        lane = lax.broadcasted_iota(jnp.int32, (tm // 2, RET_DK), 1)
        first = lane < RET_DK // 2
        cos_p = cos_ref[tr, :]
        sin_p = sin_ref[tr, :]
        cos_s = pltpu.roll(cos_p, RET_DK // 2, 1)
        sin_s = pltpu.roll(sin_p, RET_DK // 2, 1)
        cos2 = jnp.concatenate([jnp.where(first, cos_p, cos_s), jnp.where(first, cos_s, cos_p)], axis=0)
        sin2 = jnp.concatenate([jnp.where(first, -sin_p, sin_s), jnp.where(first, -sin_s, sin_p)], axis=0)
        q_all = jnp.concatenate(
            [_dot(h_ref[hr, :], w_in_ref[:, OFF_RQ:OFF_RQ + RET_QK_WIDTH]) for hr in half_rows], axis=0)
        k_all = proj(OFF_RK, RET_QK_WIDTH)
        for hh in range(RET_HEADS):
            qk = slice(hh * RET_DK, (hh + 1) * RET_DK)
            qh = q_all[:, qk]
            kh = k_all[:, qk]
            qr = qh * cos2 + pltpu.roll(qh, RET_DK // 2, 1) * sin2
            kr = (kh * cos2 + pltpu.roll(kh, RET_DK // 2, 1) * sin2) * (RET_DK ** -0.5)
            qb_ref[:, qk] = qr.astype(BF16)
            pair, side = divmod(hh, 2)
            diag = slice(side * RET_DK, (side + 1) * RET_DK)
            v_h = proj(OFF_RV + hh * RET_DV, RET_DV).astype(BF16)
            for c in range(n_chunks):
                rows = chunk_rows(c)
                lhs_ref[hh, rows, RET_CHUNK:] = (qr[rows] * qdec_ref[hh]).astype(BF16)
                kd_ref[rows, qk] = (kr[rows] * kdec_ref[hh]).astype(BF16)
                kbd_ref[c * HEAD_PAIRS + pair, diag, diag] = kr[rows].T.astype(BF16)
                rhs_ref[hh * n_chunks + c, 0:RET_CHUNK, :] = v_h[rows]

        ubuf_ref[POOL_HALO:POOL_HALO + tm, :] = proj(OFF_POOL, POOL_WIDTH)
        row = lax.broadcasted_iota(jnp.int32, (tm, 1), 0) + tile_start
        dwins = []
        for gi, w in enumerate(POOL_WINDOWS):
            cols = slice(gi * POOL_GROUP_DIM, (gi + 1) * POOL_GROUP_DIM)
            win = ubuf_ref[:, cols]
            shift = 1
            while shift < w:
                win = win + pltpu.roll(win, shift, 0)
                shift *= 2
            cnt = jnp.minimum(row + 1, w).astype(F32)
            dwin = win[POOL_HALO:, :] / cnt - ubuf_ref[POOL_HALO:POOL_HALO + tm, cols]
            dwins.append(dwin.astype(BF16))
            if gi % 2 == 1:
                cols2 = slice((gi - 1) * POOL_GROUP_DIM, (gi + 1) * POOL_GROUP_DIM)
                yg = _dot(jnp.concatenate(dwins[gi - 1:gi + 1], axis=1), pool_w_ref[gi // 2])
                ypool_ref[:, cols2] = (yg * pool_scale[:, cols2]).astype(BF16)
            vv = slice(gi * RET_DV, (gi + 1) * RET_DV)
            g_h = proj(OFF_RG + gi * RET_DV, RET_DV)
            gsilu_ref[:, vv] = g_h * jax.nn.sigmoid(g_h) * ret_g[:, vv]
        ubuf_ref[0:POOL_HALO, :] = ubuf_ref[tm:tm + POOL_HALO, :]

        for pair in range(HEAD_PAIRS):
            qk2 = slice(2 * pair * RET_DK, (2 * pair + 2) * RET_DK)
            for c in range(n_chunks):
                rows = chunk_rows(c)
                scores = _dot(qb_ref[rows, qk2], kbd_ref[c * HEAD_PAIRS + pair]) * mask_ref[pair]
                scores = scores.astype(BF16)
                lhs_ref[2 * pair, rows, 0:RET_CHUNK] = scores[:, :RET_CHUNK]
                lhs_ref[2 * pair + 1, rows, 0:RET_CHUNK] = scores[:, RET_CHUNK:]

        kvs = [[lax.dot_general(kd_ref[chunk_rows(c), hh * RET_DK:(hh + 1) * RET_DK],
                                rhs_ref[hh * n_chunks + c, 0:RET_CHUNK, :],
                                (((0,), (0,)), ((), ())), preferred_element_type=F32)
                for c in range(n_chunks)] for hh in range(RET_HEADS)]
        for hh in range(RET_HEADS):
            s = state_ref[hh]
            for c in range(n_chunks):
                rhs_ref[hh * n_chunks + c, RET_CHUNK:, :] = s.astype(BF16)
                s = chunk_decay[hh] * s + kvs[hh][c]
            state_ref[hh] = s

        xq = proj(OFF_XQ, XATTN_WIDTH).astype(BF16)

        for hh in range(RET_HEADS):
            vv = slice(hh * RET_DV, (hh + 1) * RET_DV)
            for c in range(n_chunks):
                rows = chunk_rows(c)
                y = _dot(lhs_ref[hh, rows, :], rhs_ref[hh * n_chunks + c])
                y = y * lax.rsqrt(jnp.mean(y * y, axis=-1, keepdims=True) + NORM_EPS)
                yret_ref[rows, vv] = (gsilu_ref[rows, vv] * y).astype(BF16)
            mc = slice(hh * MERGE_CHUNK, (hh + 1) * MERGE_CHUNK)
            macc_ref[:, mc] = merge_chunk(0, ypool_ref, w_up_pool_ref, hh)

        logits_ref, probs_ref = gsilu_ref, lhs_ref
        assert logits_ref.shape == (tm, XATTN_HEADS * N_MEM) and probs_ref.shape == (XATTN_HEADS, tm, N_MEM)
        for hh in range(XATTN_HEADS):
            hd = slice(hh * XATTN_DH, (hh + 1) * XATTN_DH)
            logits_ref[:, hh * N_MEM:(hh + 1) * N_MEM] = _dot(xq[:, hd], kt_ref[hd, :])
        for hh in range(XATTN_HEADS):
            s = logits_ref[:, hh * N_MEM:(hh + 1) * N_MEM]
            p = jnp.exp2((s - jnp.max(s, axis=-1, keepdims=True)) * SOFTMAX_EXP2_SCALE)
            probs_ref[hh] = (p / jnp.sum(p, axis=-1, keepdims=True)).astype(BF16)
            mc = slice(hh * MERGE_CHUNK, (hh + 1) * MERGE_CHUNK)
            macc_ref[:, mc] = macc_ref[:, mc] + merge_chunk(1, yret_ref, w_up_ret_ref, hh)
        for hh in range(XATTN_HEADS):
            hd = slice(hh * XATTN_DH, (hh + 1) * XATTN_DH)
            ymem_ref[:, hd] = _dot(probs_ref[hh], vm_ref[:, hd]).astype(BF16)

        for j in range(D_MODEL // MERGE_CHUNK):
            mc = slice(j * MERGE_CHUNK, (j + 1) * MERGE_CHUNK)
            merged_ref[:, mc] = (macc_ref[:, mc]
                                 + merge_chunk(2, ymem_ref, w_up_mem_ref, j)).astype(BF16)
        o_ref[xr, :] = x_ref[xr, :] + _dot(merged_ref[...], w_out_ref[...])

    for sub in range(n_sub):
        tile(sub)


def _mixer_layer(layer, x2d, batch, cos2, sin2, kt, vm, tables, gmix, w_in, pool_w, pool_scale,
                 ret_g, w_up_pool, w_up_ret, w_up_mem, w_out, cast_params=(), weight_layer=0):
    t, d = x2d.shape
    seq = t // batch
    tm = min(TOKEN_TILE, seq)
    n_sub = MIXER_SUBTILES if seq % (MIXER_SUBTILES * tm) == 0 else 1
    bm = n_sub * tm
    steps = seq // bm
    mask, q_dec, k_dec, chunk_decay = tables
    tok = lambda b, s: (b * steps + s, 0)
    in_specs = [
        pl.BlockSpec((bm, d), tok),
        pl.BlockSpec((bm // 2, RET_DK), tok),
        pl.BlockSpec((bm // 2, RET_DK), tok),
        pl.BlockSpec((None, None, XATTN_WIDTH, N_MEM), lambda b, s: (layer, b, 0, 0)),
        pl.BlockSpec((None, None, N_MEM, XATTN_WIDTH), lambda b, s: (layer, b, 0, 0)),
        _const_spec(mask.shape), _const_spec(q_dec.shape), _const_spec(k_dec.shape),
        _const_spec(gmix.shape), _layer_spec(w_in.shape, weight_layer),
        _layer_spec(pool_w.shape, weight_layer), _const_spec(pool_scale.shape),
        _const_spec(ret_g.shape),
    ] + [_layer_spec(w.shape, weight_layer) for w in (w_up_pool, w_up_ret, w_up_mem, w_out)]
    assert len(in_specs) == N_MIXER_INPUTS
    cast_views, cast_in_specs, cast_out_specs, cast_shapes = _cast_specs(
        cast_params, batch * steps, lambda b, s: b * steps + s, layer=layer)
    outs = pl.pallas_call(
        functools.partial(_mixer_kernel, chunk_decay, layer, n_sub, len(cast_params)),
        grid=(batch, steps),
        in_specs=in_specs + cast_in_specs,
        out_specs=[pl.BlockSpec((bm, d), tok)] + cast_out_specs,
        out_shape=[jax.ShapeDtypeStruct((t, d), F32)] + cast_shapes,
        scratch_shapes=[
            pltpu.VMEM((tm, d), BF16),
            pltpu.VMEM((POOL_HALO + tm, POOL_WIDTH), F32),
            pltpu.VMEM((RET_HEADS, RET_DK, RET_DV), F32),
            pltpu.VMEM((tm, RET_QK_WIDTH), BF16),
            pltpu.VMEM(((tm // RET_CHUNK) * HEAD_PAIRS, 2 * RET_CHUNK, 2 * RET_DK), BF16),
            pltpu.VMEM((tm, RET_QK_WIDTH), BF16),
            pltpu.VMEM((RET_HEADS, tm, RET_CHUNK + RET_DK), BF16),
            pltpu.VMEM((RET_HEADS * (tm // RET_CHUNK), RET_CHUNK + RET_DK, RET_DV), BF16),
            pltpu.VMEM((tm, RET_V_WIDTH), F32),
            pltpu.VMEM((tm, POOL_WIDTH), BF16),
            pltpu.VMEM((tm, RET_V_WIDTH), BF16),
            pltpu.VMEM((tm, XATTN_WIDTH), BF16),
            pltpu.VMEM((tm, D_MODEL), F32),
            pltpu.VMEM((tm, D_MODEL), BF16),
        ],
        compiler_params=pltpu.CompilerParams(
            dimension_semantics=("arbitrary", "arbitrary"),
            vmem_limit_bytes=VMEM_LIMIT_BYTES),
        name="mixer",
    )(x2d, cos2, sin2, kt, vm, mask, q_dec, k_dec, gmix, w_in, pool_w, pool_scale, ret_g,
      w_up_pool, w_up_ret, w_up_mem, w_out, *cast_views)
    return outs[0], [o.reshape((1,) + p.shape[1:]) for o, p in zip(outs[1:], cast_params)]


def _mlp_kernel(apply_final_norm, layer, n_cast, *refs):
    x_ref, g_ref, w1_ref, w2_ref, gf_ref = refs[:N_MLP_INPUTS]
    cast_in = refs[N_MLP_INPUTS:N_MLP_INPUTS + n_cast]
    o_ref = refs[N_MLP_INPUTS + n_cast]
    cast_out = refs[N_MLP_INPUTS + n_cast + 1:N_MLP_INPUTS + 2 * n_cast + 1]
    (hid_ref,) = refs[N_MLP_INPUTS + 2 * n_cast + 1:]
    _cast_blocks(cast_in, cast_out)
    g = g_ref[layer:layer + 1, :]
    tm = x_ref.shape[0]
    n_parts = MLP_ROW_PARTS if tm % (MLP_ROW_PARTS * BF16_SUBLANES) == 0 else 1
    part_rows = [slice(i * (tm // n_parts), (i + 1) * (tm // n_parts)) for i in range(n_parts)]
    for rows in part_rows:
        h = _rms_norm(x_ref[rows, :], g).astype(BF16)
        for j in range(D_FF // N_CHUNK):
            nc = slice(j * N_CHUNK, (j + 1) * N_CHUNK)
            a = jnp.maximum(_dot(h, w1_ref[:, nc]), 0.0)
            hid_ref[rows, nc] = (a * a).astype(BF16)
    for rows in part_rows:
        y = x_ref[rows, :] + _dot(hid_ref[rows, :], w2_ref[...])
        if apply_final_norm:
            y = _rms_norm(y, gf_ref[...])
        o_ref[rows, :] = y


def _mlp_layer(layer, x2d, g, w1, w2, g_final, apply_final_norm, cast_params=(), cast_layer=None):
    t, d = x2d.shape
    tm = min(MLP_TOKEN_TILE, t)
    steps = t // tm
    cast_views, cast_in_specs, cast_out_specs, cast_shapes = _cast_specs(
        cast_params, steps, lambda i: i, layer=cast_layer)
    outs = pl.pallas_call(
        functools.partial(_mlp_kernel, apply_final_norm, layer, len(cast_params)),
        grid=(steps,),
        in_specs=[pl.BlockSpec((tm, d), lambda i: (i, 0)),
                  _const_spec(g.shape), _layer_spec(w1.shape, 0),
                  _layer_spec(w2.shape, 0), _const_spec(g_final.shape)] + cast_in_specs,
        out_specs=[pl.BlockSpec((tm, d), lambda i: (i, 0))] + cast_out_specs,
        out_shape=[jax.ShapeDtypeStruct((t, d), F32)] + cast_shapes,
        scratch_shapes=[pltpu.VMEM((tm, D_FF), BF16)],
        compiler_params=pltpu.CompilerParams(
            dimension_semantics=("arbitrary",),
            vmem_limit_bytes=VMEM_LIMIT_BYTES),
        name="mlp",
    )(x2d, g, w1, w2, g_final, *cast_views)
    return outs[0], [o.reshape((1,) + p.shape[1:]) for o, p in zip(outs[1:], cast_params)]


def kernel(x, mem, positions, norm_mix_g, w_in, pool_w, pool_scale, ret_norm_g, mem_norm_g,
           w_mem_kv, w_up_pool, w_up_ret, w_up_mem, w_out, norm_mlp_g, w_mlp1, w_mlp2,
           final_norm_g):
    batch, seq, d = x.shape
    depth = w_in.shape[0]
    assert d == D_MODEL and w_in.shape[-1] == IN_COLS and mem.shape[1] == N_MEM
    assert seq % RET_CHUNK == 0 and seq % min(TOKEN_TILE, seq) == 0

    mixer_w_f32 = (w_in, _pool_pair_weights(pool_w), w_up_pool, w_up_ret, w_up_mem, w_out)
    cos2, sin2, mixer_w = _rope_tables(positions, min(TOKEN_TILE, seq), mixer_w_f32, cast_layer=0)
    kt, vm = _mem_kv(mem, mem_norm_g, w_mem_kv)
    tables = _retention_tables()
    g_final = final_norm_g.reshape(1, -1)

    x2d = x.reshape(batch * seq, d)
    for l in range(depth):
        w_in_b, pool_w_b, w_up_pool_b, w_up_ret_b, w_up_mem_b, w_out_b = mixer_w
        x2d, mlp_w = _mixer_layer(
            l, x2d, batch, cos2, sin2, kt, vm, tables, norm_mix_g, w_in_b, pool_w_b, pool_scale,
            ret_norm_g, w_up_pool_b, w_up_ret_b, w_up_mem_b, w_out_b,
            cast_params=(w_mlp1, w_mlp2), weight_layer=0)
        last = l == depth - 1
        x2d, mixer_w = _mlp_layer(
            l, x2d, norm_mlp_g, mlp_w[0], mlp_w[1], g_final, apply_final_norm=last,
            cast_params=() if last else mixer_w_f32, cast_layer=None if last else l + 1)
    return x2d.reshape(batch, seq, d)
```

```python
import functools

import numpy as np
import jax
import jax.numpy as jnp
from jax import lax
from jax.experimental import pallas as pl
from jax.experimental.pallas import tpu as pltpu

D_MODEL = 1024
N_MEM = 256
POOL_WINDOWS = (2, 4, 8, 16)
POOL_GROUPS = 4
POOL_GROUP_DIM = 128
POOL_WIDTH = POOL_GROUPS * POOL_GROUP_DIM
POOL_HALO = 16
RET_HEADS = 4
RET_DK = 128
RET_DV = 256
RET_QK_WIDTH = RET_HEADS * RET_DK
RET_V_WIDTH = RET_HEADS * RET_DV
RET_CHUNK = 128
HEAD_PAIRS = RET_HEADS // 2
ROPE_BASE = 10000.0
XATTN_HEADS = 4
XATTN_DH = 128
XATTN_WIDTH = XATTN_HEADS * XATTN_DH
D_FF = 4 * D_MODEL
NORM_EPS = 1e-6

OFF_POOL = 0
OFF_RQ = OFF_POOL + POOL_WIDTH
OFF_RK = OFF_RQ + RET_QK_WIDTH
OFF_RV = OFF_RK + RET_QK_WIDTH
OFF_RG = OFF_RV + RET_V_WIDTH
OFF_XQ = OFF_RG + RET_V_WIDTH
OFF_GATE = OFF_XQ + XATTN_WIDTH
IN_COLS = OFF_GATE + 3 * D_MODEL

TOKEN_TILE = 512
MIXER_SUBTILES = 1
MIXER_NORM_PARTS = 2
MLP_TOKEN_TILE = 1024
MLP_ROW_PARTS = 2
ROPE_TILE = 2048
N_CHUNK = 512
MERGE_CHUNK = 256
VMEM_LIMIT_BYTES = 60 * 1024 * 1024
BF16_SUBLANES = 16
N_MIXER_INPUTS = 17
N_MLP_INPUTS = 5

SOFTMAX_EXP2_SCALE = float(XATTN_DH ** -0.5 * np.log2(np.e))

BF16 = jnp.bfloat16
F32 = jnp.float32


def _dot(a, b):
    return jnp.dot(a, b, preferred_element_type=F32)


def _rms_norm(x, g):
    return x * lax.rsqrt(jnp.mean(x * x, axis=-1, keepdims=True) + NORM_EPS) * g


def _const_spec(shape):
    nd = len(shape)
    return pl.BlockSpec(shape, lambda *_: (0,) * nd, pipeline_mode=pl.Buffered(1))


def _layer_spec(stacked_shape, layer):
    nd = len(stacked_shape) - 1
    return pl.BlockSpec((None,) + tuple(stacked_shape[1:]), lambda *_: (layer,) + (0,) * nd,
                        pipeline_mode=pl.Buffered(1))


def _cast_specs(params, n_steps, index_of_step, layer=None):
    views, in_specs, out_specs, out_shapes = [], [], [], []
    for p in params:
        cols = p.shape[-1]
        rows_per_layer = int(np.prod(p.shape[1:-1]))
        rows = rows_per_layer if layer is not None else p.shape[0] * rows_per_layer
        slab, rem = divmod(rows, n_steps)
        assert rem == 0 and slab % BF16_SUBLANES == 0, (p.shape, n_steps)
        first = 0 if layer is None else layer * n_steps
        views.append(p.reshape(-1, cols))
        in_specs.append(pl.BlockSpec((slab, cols), lambda *g, f=first: (f + index_of_step(*g), 0)))
        out_specs.append(pl.BlockSpec((slab, cols), lambda *g: (index_of_step(*g), 0)))
        out_shapes.append(jax.ShapeDtypeStruct((rows, cols), BF16))
    return views, in_specs, out_specs, out_shapes


def _cast_blocks(in_refs, out_refs):
    for src_ref, dst_ref in zip(in_refs, out_refs):
        dst_ref[...] = src_ref[...].astype(BF16)


def _rope_kernel(n_cast, pos_ref, freq_ref, *refs):
    cast_in = refs[:n_cast]
    cos_ref, sin_ref = refs[n_cast:n_cast + 2]
    _cast_blocks(cast_in, refs[n_cast + 2:])
    lane = lax.broadcasted_iota(jnp.int32, cos_ref.shape, 1)
    pos = pos_ref[...]
    ang = jnp.where(lane < RET_DK // 2, pos[:, 0:1], pos[:, 1:2]) * freq_ref[...]
    cos_ref[...] = jnp.cos(ang)
    sin_ref[...] = jnp.sin(ang)


def _rope_tables(positions, token_tile, cast_params=(), cast_layer=0):
    t = positions.size
    half_tile = token_tile // 2
    half = RET_DK // 2
    inv_freq = ROPE_BASE ** (-jnp.arange(half, dtype=F32) / half)
    freq2 = jnp.concatenate([inv_freq, inv_freq]).reshape(1, RET_DK)
    pos = positions.astype(F32).reshape(t // token_tile, 2, half_tile)
    pos = jnp.swapaxes(pos, 1, 2).reshape(t // 2, 2)
    rows = min(ROPE_TILE, t // 2)
    steps = t // 2 // rows
    cast_views, cast_in_specs, cast_out_specs, cast_shapes = _cast_specs(
        cast_params, steps, lambda i: i, layer=cast_layer)
    outs = pl.pallas_call(
        functools.partial(_rope_kernel, len(cast_params)),
        grid=(steps,),
        in_specs=[pl.BlockSpec((rows, 2), lambda i: (i, 0)),
                  pl.BlockSpec((1, RET_DK), lambda i: (0, 0))] + cast_in_specs,
        out_specs=[pl.BlockSpec((rows, RET_DK), lambda i: (i, 0))] * 2 + cast_out_specs,
        out_shape=[jax.ShapeDtypeStruct((t // 2, RET_DK), F32)] * 2 + cast_shapes,
        name="rope_tables",
    )(pos, freq2, *cast_views)
    return outs[0], outs[1], [o.reshape((1,) + p.shape[1:]) for o, p in zip(outs[2:], cast_params)]


def _mem_kv_kernel(mem_ref, g_ref, w_ref, kt_ref, v_ref):
    w = w_ref[...].astype(BF16)
    for i in range(mem_ref.shape[0]):
        mem_n = _rms_norm(mem_ref[i], g_ref[...]).astype(BF16)
        kv = _dot(mem_n, w)
        kt_ref[i] = kv[:, :XATTN_WIDTH].T.astype(BF16)
        v_ref[i] = kv[:, XATTN_WIDTH:].astype(BF16)


def _mem_kv(mem, mem_norm_g, w_mem_kv):
    depth = w_mem_kv.shape[0]
    b, m, d = mem.shape
    return pl.pallas_call(
        _mem_kv_kernel,
        grid=(depth,),
        in_specs=[pl.BlockSpec((b, m, d), lambda l: (0, 0, 0)),
                  pl.BlockSpec((None, 1, d), lambda l: (l, 0, 0)),
                  pl.BlockSpec((None, d, 2 * XATTN_WIDTH), lambda l: (l, 0, 0))],
        out_specs=[pl.BlockSpec((None, b, XATTN_WIDTH, m), lambda l: (l, 0, 0, 0)),
                   pl.BlockSpec((None, b, m, XATTN_WIDTH), lambda l: (l, 0, 0, 0))],
        out_shape=[jax.ShapeDtypeStruct((depth, b, XATTN_WIDTH, m), BF16),
                   jax.ShapeDtypeStruct((depth, b, m, XATTN_WIDTH), BF16)],
        name="mem_kv",
    )(mem, mem_norm_g.reshape(depth, 1, d), w_mem_kv)


def _pool_pair_weights(pool_w):
    zeros = jnp.zeros_like(pool_w[:, 0::2])
    top = jnp.concatenate([pool_w[:, 0::2], zeros], axis=-1)
    bottom = jnp.concatenate([zeros, pool_w[:, 1::2]], axis=-1)
    return jnp.concatenate([top, bottom], axis=-2)


def _retention_tables():
    c = RET_CHUNK
    log_gamma = np.log(1.0 - 2.0 ** (-5.0 - np.arange(RET_HEADS, dtype=np.float64)))
    idx = np.arange(c, dtype=np.float64)
    diff = idx[:, None] - idx[None, :]
    mask = np.where(diff[None] >= 0, np.exp(diff[None] * log_gamma[:, None, None]), 0.0)
    q_dec = np.exp((idx + 1.0)[None, :] * log_gamma[:, None])
    k_dec = np.exp((c - 1.0 - idx)[None, :] * log_gamma[:, None])
    q_dec = np.broadcast_to(q_dec[:, :, None], (RET_HEADS, c, RET_DK))
    k_dec = np.broadcast_to(k_dec[:, :, None], (RET_HEADS, c, RET_DK))
    chunk_decay = tuple(float(v) for v in np.exp(c * log_gamma))
    mask = np.concatenate([mask[0::2], mask[1::2]], axis=-1)
    as_f32 = lambda a: jnp.asarray(np.ascontiguousarray(a), dtype=F32)
    return as_f32(mask), as_f32(q_dec), as_f32(k_dec), chunk_decay


def _mixer_kernel(chunk_decay, layer, n_sub, n_cast, *refs):
    (x_ref, cos_ref, sin_ref, kt_ref, vm_ref, mask_ref, qdec_ref, kdec_ref,
     gmix_ref, w_in_ref, pool_w_ref, pool_scale_ref, ret_g_ref,
     w_up_pool_ref, w_up_ret_ref, w_up_mem_ref, w_out_ref) = refs[:N_MIXER_INPUTS]
    cast_in = refs[N_MIXER_INPUTS:N_MIXER_INPUTS + n_cast]
    o_ref = refs[N_MIXER_INPUTS + n_cast]
    cast_out = refs[N_MIXER_INPUTS + n_cast + 1:N_MIXER_INPUTS + 2 * n_cast + 1]
    (h_ref, ubuf_ref, state_ref, qb_ref, kbd_ref, kd_ref, lhs_ref, rhs_ref, gsilu_ref,
     ypool_ref, yret_ref, ymem_ref, macc_ref, merged_ref) = refs[N_MIXER_INPUTS + 2 * n_cast + 1:]
    gmix = gmix_ref[layer:layer + 1, :]
    pool_scale = pool_scale_ref[layer:layer + 1, :]
    ret_g = ret_g_ref[layer:layer + 1, :]
    tm = x_ref.shape[0] // n_sub
    n_chunks = tm // RET_CHUNK
    seq_step = pl.program_id(1)

    @pl.when(seq_step == 0)
    def _():
        ubuf_ref[0:POOL_HALO, :] = jnp.zeros((POOL_HALO, POOL_WIDTH), F32)
        state_ref[...] = jnp.zeros_like(state_ref)
        kbd_ref[...] = jnp.zeros_like(kbd_ref)

    def tile(sub):
        xr = slice(sub * tm, (sub + 1) * tm)
        tr = slice(sub * (tm // 2), (sub + 1) * (tm // 2))
        tile_start = (seq_step * n_sub + sub) * tm
        part = tm // MIXER_NORM_PARTS
        half_rows = tuple(slice(i * part, (i + 1) * part) for i in range(MIXER_NORM_PARTS))
        for hr in half_rows:
            h_ref[hr, :] = _rms_norm(x_ref[sub * tm + hr.start:sub * tm + hr.stop, :],
                                     gmix).astype(BF16)

        def proj(off, width):
            return _dot(h_ref[...], w_in_ref[:, off:off + width])

        def chunk_rows(c):
            return slice(c * RET_CHUNK, (c + 1) * RET_CHUNK)

        def merge_chunk(b, y_ref, w_up_ref, j):
            cols = slice(j * MERGE_CHUNK, (j + 1) * MERGE_CHUNK)
            g = jax.nn.sigmoid(proj(OFF_GATE + b * D_MODEL + j * MERGE_CHUNK, MERGE_CHUNK))
            return g * _dot(y_ref[...], w_up_ref[:, cols])


        lane = lax.broadcasted_iota(jnp.int32, (tm // 2, RET_DK), 1)
        first = lane < RET_DK // 2
        cos_p = cos_ref[tr, :]
        sin_p = sin_ref[tr, :]
        cos_s = pltpu.roll(cos_p, RET_DK // 2, 1)
        sin_s = pltpu.roll(sin_p, RET_DK // 2, 1)
        cos2 = jnp.concatenate([jnp.where(first, cos_p, cos_s), jnp.where(first, cos_s, cos_p)], axis=0)
        sin2 = jnp.concatenate([jnp.where(first, -sin_p, sin_s), jnp.where(first, -sin_s, sin_p)], axis=0)
        q_all = jnp.concatenate(
            [_dot(h_ref[hr, :], w_in_ref[:, OFF_RQ:OFF_RQ + RET_QK_WIDTH]) for hr in half_rows], axis=0)
        k_all = proj(OFF_RK, RET_QK_WIDTH)
        for hh in range(RET_HEADS):
            qk = slice(hh * RET_DK, (hh + 1) * RET_DK)
            qh = q_all[:, qk]
            kh = k_all[:, qk]
            qr = qh * cos2 + pltpu.roll(qh, RET_DK // 2, 1) * sin2
            kr = (kh * cos2 + pltpu.roll(kh, RET_DK // 2, 1) * sin2) * (RET_DK ** -0.5)
            qb_ref[:, qk] = qr.astype(BF16)
            pair, side = divmod(hh, 2)
            diag = slice(side * RET_DK, (side + 1) * RET_DK)
            v_h = proj(OFF_RV + hh * RET_DV, RET_DV).astype(BF16)
            for c in range(n_chunks):
                rows = chunk_rows(c)
                lhs_ref[hh, rows, RET_CHUNK:] = (qr[rows] * qdec_ref[hh]).astype(BF16)
                kd_ref[rows, qk] = (kr[rows] * kdec_ref[hh]).astype(BF16)
                kbd_ref[c * HEAD_PAIRS + pair, diag, diag] = kr[rows].T.astype(BF16)
                rhs_ref[hh * n_chunks + c, 0:RET_CHUNK, :] = v_h[rows]

        ubuf_ref[POOL_HALO:POOL_HALO + tm, :] = proj(OFF_POOL, POOL_WIDTH)
        row = lax.broadcasted_iota(jnp.int32, (tm, 1), 0) + tile_start
        dwins = []
        for gi, w in enumerate(POOL_WINDOWS):
            cols = slice(gi * POOL_GROUP_DIM, (gi + 1) * POOL_GROUP_DIM)
            win = ubuf_ref[:, cols]
            shift = 1
            while shift < w:
                win = win + pltpu.roll(win, shift, 0)
                shift *= 2
            cnt = jnp.minimum(row + 1, w).astype(F32)
            dwin = win[POOL_HALO:, :] / cnt - ubuf_ref[POOL_HALO:POOL_HALO + tm, cols]
            dwins.append(dwin.astype(BF16))
            if gi % 2 == 1:
                cols2 = slice((gi - 1) * POOL_GROUP_DIM, (gi + 1) * POOL_GROUP_DIM)
                yg = _dot(jnp.concatenate(dwins[gi - 1:gi + 1], axis=1), pool_w_ref[gi // 2])
                ypool_ref[:, cols2] = (yg * pool_scale[:, cols2]).astype(BF16)
            vv = slice(gi * RET_DV, (gi + 1) * RET_DV)
            g_h = proj(OFF_RG + gi * RET_DV, RET_DV)
            gsilu_ref[:, vv] = g_h * jax.nn.sigmoid(g_h) * ret_g[:, vv]
        ubuf_ref[0:POOL_HALO, :] = ubuf_ref[tm:tm + POOL_HALO, :]

        for pair in range(HEAD_PAIRS):
            qk2 = slice(2 * pair * RET_DK, (2 * pair + 2) * RET_DK)
            for c in range(n_chunks):
                rows = chunk_rows(c)
                scores = _dot(qb_ref[rows, qk2], kbd_ref[c * HEAD_PAIRS + pair]) * mask_ref[pair]
                scores = scores.astype(BF16)
                lhs_ref[2 * pair, rows, 0:RET_CHUNK] = scores[:, :RET_CHUNK]
                lhs_ref[2 * pair + 1, rows, 0:RET_CHUNK] = scores[:, RET_CHUNK:]

        kvs = [[lax.dot_general(kd_ref[chunk_rows(c), hh * RET_DK:(hh + 1) * RET_DK],
                                rhs_ref[hh * n_chunks + c, 0:RET_CHUNK, :],
                                (((0,), (0,)), ((), ())), preferred_element_type=F32)
                for c in range(n_chunks)] for hh in range(RET_HEADS)]
        for hh in range(RET_HEADS):
            s = state_ref[hh]
            for c in range(n_chunks):
                rhs_ref[hh * n_chunks + c, RET_CHUNK:, :] = s.astype(BF16)
                s = chunk_decay[hh] * s + kvs[hh][c]
            state_ref[hh] = s

        xq = proj(OFF_XQ, XATTN_WIDTH).astype(BF16)

        for hh in range(RET_HEADS):
            vv = slice(hh * RET_DV, (hh + 1) * RET_DV)
            for c in range(n_chunks):
                rows = chunk_rows(c)
                y = _dot(lhs_ref[hh, rows, :], rhs_ref[hh * n_chunks + c])
                y = y * lax.rsqrt(jnp.mean(y * y, axis=-1, keepdims=True) + NORM_EPS)
                yret_ref[rows, vv] = (gsilu_ref[rows, vv] * y).astype(BF16)
            mc = slice(hh * MERGE_CHUNK, (hh + 1) * MERGE_CHUNK)
            macc_ref[:, mc] = merge_chunk(0, ypool_ref, w_up_pool_ref, hh)

        logits_ref, probs_ref = gsilu_ref, lhs_ref
        assert logits_ref.shape == (tm, XATTN_HEADS * N_MEM) and probs_ref.shape == (XATTN_HEADS, tm, N_MEM)
        for hh in range(XATTN_HEADS):
            hd = slice(hh * XATTN_DH, (hh + 1) * XATTN_DH)
            logits_ref[:, hh * N_MEM:(hh + 1) * N_MEM] = _dot(xq[:, hd], kt_ref[hd, :])
        for hh in range(XATTN_HEADS):
            s = logits_ref[:, hh * N_MEM:(hh + 1) * N_MEM]
            p = jnp.exp2((s - jnp.max(s, axis=-1, keepdims=True)) * SOFTMAX_EXP2_SCALE)
            probs_ref[hh] = (p / jnp.sum(p, axis=-1, keepdims=True)).astype(BF16)
            mc = slice(hh * MERGE_CHUNK, (hh + 1) * MERGE_CHUNK)
            macc_ref[:, mc] = macc_ref[:, mc] + merge_chunk(1, yret_ref, w_up_ret_ref, hh)
        for hh in range(XATTN_HEADS):
            hd = slice(hh * XATTN_DH, (hh + 1) * XATTN_DH)
            ymem_ref[:, hd] = _dot(probs_ref[hh], vm_ref[:, hd]).astype(BF16)

        for j in range(D_MODEL // MERGE_CHUNK):
            mc = slice(j * MERGE_CHUNK, (j + 1) * MERGE_CHUNK)
            merged_ref[:, mc] = (macc_ref[:, mc]
                                 + merge_chunk(2, ymem_ref, w_up_mem_ref, j)).astype(BF16)
        o_ref[xr, :] = x_ref[xr, :] + _dot(merged_ref[...], w_out_ref[...])

    for sub in range(n_sub):
        tile(sub)
    _cast_blocks(cast_in, cast_out)


def _mixer_layer(layer, x2d, batch, cos2, sin2, kt, vm, tables, gmix, w_in, pool_w, pool_scale,
                 ret_g, w_up_pool, w_up_ret, w_up_mem, w_out, cast_params=(), weight_layer=0):
    t, d = x2d.shape
    seq = t // batch
    tm = min(TOKEN_TILE, seq)
    n_sub = MIXER_SUBTILES if seq % (MIXER_SUBTILES * tm) == 0 else 1
    bm = n_sub * tm
    steps = seq // bm
    mask, q_dec, k_dec, chunk_decay = tables
    tok = lambda b, s: (b * steps + s, 0)
    in_specs = [
        pl.BlockSpec((bm, d), tok),
        pl.BlockSpec((bm // 2, RET_DK), tok),
        pl.BlockSpec((bm // 2, RET_DK), tok),
        pl.BlockSpec((None, None, XATTN_WIDTH, N_MEM), lambda b, s: (layer, b, 0, 0)),
        pl.BlockSpec((None, None, N_MEM, XATTN_WIDTH), lambda b, s: (layer, b, 0, 0)),
        _const_spec(mask.shape), _const_spec(q_dec.shape), _const_spec(k_dec.shape),
        _const_spec(gmix.shape), _layer_spec(w_in.shape, weight_layer),
        _layer_spec(pool_w.shape, weight_layer), _const_spec(pool_scale.shape),
        _const_spec(ret_g.shape),
    ] + [_layer_spec(w.shape, weight_layer) for w in (w_up_pool, w_up_ret, w_up_mem, w_out)]
    assert len(in_specs) == N_MIXER_INPUTS
    cast_views, cast_in_specs, cast_out_specs, cast_shapes = _cast_specs(
        cast_params, batch * steps, lambda b, s: b * steps + s, layer=layer)
    outs = pl.pallas_call(
        functools.partial(_mixer_kernel, chunk_decay, layer, n_sub, len(cast_params)),
        grid=(batch, steps),
        in_specs=in_specs + cast_in_specs,
        out_specs=[pl.BlockSpec((bm, d), tok)] + cast_out_specs,
        out_shape=[jax.ShapeDtypeStruct((t, d), F32)] + cast_shapes,
        scratch_shapes=[
            pltpu.VMEM((tm, d), BF16),
            pltpu.VMEM((POOL_HALO + tm, POOL_WIDTH), F32),
            pltpu.VMEM((RET_HEADS, RET_DK, RET_DV), F32),
            pltpu.VMEM((tm, RET_QK_WIDTH), BF16),
            pltpu.VMEM(((tm // RET_CHUNK) * HEAD_PAIRS, 2 * RET_CHUNK, 2 * RET_DK), BF16),
            pltpu.VMEM((tm, RET_QK_WIDTH), BF16),
            pltpu.VMEM((RET_HEADS, tm, RET_CHUNK + RET_DK), BF16),
            pltpu.VMEM((RET_HEADS * (tm // RET_CHUNK), RET_CHUNK + RET_DK, RET_DV), BF16),
            pltpu.VMEM((tm, RET_V_WIDTH), F32),
            pltpu.VMEM((tm, POOL_WIDTH), BF16),
            pltpu.VMEM((tm, RET_V_WIDTH), BF16),
            pltpu.VMEM((tm, XATTN_WIDTH), BF16),
            pltpu.VMEM((tm, D_MODEL), F32),
            pltpu.VMEM((tm, D_MODEL), BF16),
        ],
        compiler_params=pltpu.CompilerParams(
            dimension_semantics=("arbitrary", "arbitrary"),
            vmem_limit_bytes=VMEM_LIMIT_BYTES),
        name="mixer",
    )(x2d, cos2, sin2, kt, vm, mask, q_dec, k_dec, gmix, w_in, pool_w, pool_scale, ret_g,
      w_up_pool, w_up_ret, w_up_mem, w_out, *cast_views)
    return outs[0], [o.reshape((1,) + p.shape[1:]) for o, p in zip(outs[1:], cast_params)]


def _mlp_kernel(apply_final_norm, layer, n_cast, *refs):
    x_ref, g_ref, w1_ref, w2_ref, gf_ref = refs[:N_MLP_INPUTS]
    cast_in = refs[N_MLP_INPUTS:N_MLP_INPUTS + n_cast]
    o_ref = refs[N_MLP_INPUTS + n_cast]
    cast_out = refs[N_MLP_INPUTS + n_cast + 1:N_MLP_INPUTS + 2 * n_cast + 1]
    (hid_ref,) = refs[N_MLP_INPUTS + 2 * n_cast + 1:]
    g = g_ref[layer:layer + 1, :]
    tm = x_ref.shape[0]
    n_parts = MLP_ROW_PARTS if tm % (MLP_ROW_PARTS * BF16_SUBLANES) == 0 else 1
    part_rows = [slice(i * (tm // n_parts), (i + 1) * (tm // n_parts)) for i in range(n_parts)]
    for rows in part_rows:
        h = _rms_norm(x_ref[rows, :], g).astype(BF16)
        for j in range(D_FF // N_CHUNK):
            nc = slice(j * N_CHUNK, (j + 1) * N_CHUNK)
            a = jnp.maximum(_dot(h, w1_ref[:, nc]), 0.0)
            hid_ref[rows, nc] = (a * a).astype(BF16)
    for rows in part_rows:
        y = x_ref[rows, :] + _dot(hid_ref[rows, :], w2_ref[...])
        if apply_final_norm:
            y = _rms_norm(y, gf_ref[...])
        o_ref[rows, :] = y
    _cast_blocks(cast_in, cast_out)


def _mlp_layer(layer, x2d, g, w1, w2, g_final, apply_final_norm, cast_params=(), cast_layer=None):
    t, d = x2d.shape
    tm = min(MLP_TOKEN_TILE, t)
    steps = t // tm
    cast_views, cast_in_specs, cast_out_specs, cast_shapes = _cast_specs(
        cast_params, steps, lambda i: i, layer=cast_layer)
    outs = pl.pallas_call(
        functools.partial(_mlp_kernel, apply_final_norm, layer, len(cast_params)),
        grid=(steps,),
        in_specs=[pl.BlockSpec((tm, d), lambda i: (i, 0)),
                  _const_spec(g.shape), _layer_spec(w1.shape, 0),
                  _layer_spec(w2.shape, 0), _const_spec(g_final.shape)] + cast_in_specs,
        out_specs=[pl.BlockSpec((tm, d), lambda i: (i, 0))] + cast_out_specs,
        out_shape=[jax.ShapeDtypeStruct((t, d), F32)] + cast_shapes,
        scratch_shapes=[pltpu.VMEM((tm, D_FF), BF16)],
        compiler_params=pltpu.CompilerParams(
            dimension_semantics=("arbitrary",),
            vmem_limit_bytes=VMEM_LIMIT_BYTES),
        name="mlp",
    )(x2d, g, w1, w2, g_final, *cast_views)
    return outs[0], [o.reshape((1,) + p.shape[1:]) for o, p in zip(outs[1:], cast_params)]


def kernel(x, mem, positions, norm_mix_g, w_in, pool_w, pool_scale, ret_norm_g, mem_norm_g,
           w_mem_kv, w_up_pool, w_up_ret, w_up_mem, w_out, norm_mlp_g, w_mlp1, w_mlp2,
           final_norm_g):
    batch, seq, d = x.shape
    depth = w_in.shape[0]
    assert d == D_MODEL and w_in.shape[-1] == IN_COLS and mem.shape[1] == N_MEM
    assert seq % RET_CHUNK == 0 and seq % min(TOKEN_TILE, seq) == 0

    mixer_w_f32 = (w_in, _pool_pair_weights(pool_w), w_up_pool, w_up_ret, w_up_mem, w_out)
    cos2, sin2, mixer_w = _rope_tables(positions, min(TOKEN_TILE, seq), mixer_w_f32, cast_layer=0)
    kt, vm = _mem_kv(mem, mem_norm_g, w_mem_kv)
    tables = _retention_tables()
    g_final = final_norm_g.reshape(1, -1)

    x2d = x.reshape(batch * seq, d)
    for l in range(depth):
        w_in_b, pool_w_b, w_up_pool_b, w_up_ret_b, w_up_mem_b, w_out_b = mixer_w
        x2d, mlp_w = _mixer_layer(
            l, x2d, batch, cos2, sin2, kt, vm, tables, norm_mix_g, w_in_b, pool_w_b, pool_scale,
            ret_norm_g, w_up_pool_b, w_up_ret_b, w_up_mem_b, w_out_b,
            cast_params=(w_mlp1, w_mlp2), weight_layer=0)
        last = l == depth - 1
        x2d, mixer_w = _mlp_layer(
            l, x2d, norm_mlp_g, mlp_w[0], mlp_w[1], g_final, apply_final_norm=last,
            cast_params=() if last else mixer_w_f32, cast_layer=None if last else l + 1)
    return x2d.reshape(batch, seq, d)
```

```python
import functools

import numpy as np
import jax
import jax.numpy as jnp
from jax import lax
from jax.experimental import pallas as pl
from jax.experimental.pallas import tpu as pltpu

D_MODEL = 1024
N_MEM = 256
POOL_WINDOWS = (2, 4, 8, 16)
POOL_GROUPS = 4
POOL_GROUP_DIM = 128
POOL_WIDTH = POOL_GROUPS * POOL_GROUP_DIM
POOL_HALO = 16
RET_HEADS = 4
RET_DK = 128
RET_DV = 256
RET_QK_WIDTH = RET_HEADS * RET_DK
RET_V_WIDTH = RET_HEADS * RET_DV
RET_CHUNK = 128
HEAD_PAIRS = RET_HEADS // 2
ROPE_BASE = 10000.0
XATTN_HEADS = 4
XATTN_DH = 128
XATTN_WIDTH = XATTN_HEADS * XATTN_DH
D_FF = 4 * D_MODEL
NORM_EPS = 1e-6

OFF_POOL = 0
OFF_RQ = OFF_POOL + POOL_WIDTH
OFF_RK = OFF_RQ + RET_QK_WIDTH
OFF_RV = OFF_RK + RET_QK_WIDTH
OFF_RG = OFF_RV + RET_V_WIDTH
OFF_XQ = OFF_RG + RET_V_WIDTH
OFF_GATE = OFF_XQ + XATTN_WIDTH
IN_COLS = OFF_GATE + 3 * D_MODEL

TOKEN_TILE = 512
MIXER_SUBTILES = 2
MIXER_NORM_PARTS = 2
MLP_TOKEN_TILE = 1024
MLP_ROW_PARTS = 2
ROPE_TILE = 2048
N_CHUNK = 512
MERGE_CHUNK = 256
VMEM_LIMIT_BYTES = 60 * 1024 * 1024
BF16_SUBLANES = 16
N_MIXER_INPUTS = 16
N_MLP_INPUTS = 5

SOFTMAX_EXP2_SCALE = float(XATTN_DH ** -0.5 * np.log2(np.e))

BF16 = jnp.bfloat16
F32 = jnp.float32


def _dot(a, b):
    return jnp.dot(a, b, preferred_element_type=F32)


def _rms_norm(x, g):
    return x * lax.rsqrt(jnp.mean(x * x, axis=-1, keepdims=True) + NORM_EPS) * g


def _const_spec(shape):
    nd = len(shape)
    return pl.BlockSpec(shape, lambda *_: (0,) * nd, pipeline_mode=pl.Buffered(1))


def _layer_spec(stacked_shape, layer):
    nd = len(stacked_shape) - 1
    return pl.BlockSpec((None,) + tuple(stacked_shape[1:]), lambda *_: (layer,) + (0,) * nd,
                        pipeline_mode=pl.Buffered(1))


def _cast_specs(params, n_steps, index_of_step, layer=None):
    views, in_specs, out_specs, out_shapes = [], [], [], []
    for p in params:
        cols = p.shape[-1]
        rows_per_layer = int(np.prod(p.shape[1:-1]))
        rows = rows_per_layer if layer is not None else p.shape[0] * rows_per_layer
        slab, rem = divmod(rows, n_steps)
        assert rem == 0 and slab % BF16_SUBLANES == 0, (p.shape, n_steps)
        first = 0 if layer is None else layer * n_steps
        views.append(p.reshape(-1, cols))
        in_specs.append(pl.BlockSpec((slab, cols), lambda *g, f=first: (f + index_of_step(*g), 0)))
        out_specs.append(pl.BlockSpec((slab, cols), lambda *g: (index_of_step(*g), 0)))
        out_shapes.append(jax.ShapeDtypeStruct((rows, cols), BF16))
    return views, in_specs, out_specs, out_shapes


def _cast_blocks(in_refs, out_refs):
    for src_ref, dst_ref in zip(in_refs, out_refs):
        dst_ref[...] = src_ref[...].astype(BF16)


def _rope_kernel(n_cast, pos_ref, freq_ref, *refs):
    cast_in = refs[:n_cast]
    cos_ref, sin_ref = refs[n_cast:n_cast + 2]
    _cast_blocks(cast_in, refs[n_cast + 2:])
    lane = lax.broadcasted_iota(jnp.int32, cos_ref.shape, 1)
    pos = pos_ref[...]
    ang = jnp.where(lane < RET_DK // 2, pos[:, 0:1], pos[:, 1:2]) * freq_ref[...]
    cos_ref[...] = jnp.cos(ang)
    sin_ref[...] = jnp.sin(ang)


def _rope_tables(positions, token_tile, cast_params=(), cast_layer=0):
    t = positions.size
    half_tile = token_tile // 2
    half = RET_DK // 2
    inv_freq = ROPE_BASE ** (-jnp.arange(half, dtype=F32) / half)
    freq2 = jnp.concatenate([inv_freq, inv_freq]).reshape(1, RET_DK)
    pos = positions.astype(F32).reshape(t // token_tile, 2, half_tile)
    pos = jnp.swapaxes(pos, 1, 2).reshape(t // 2, 2)
    rows = min(ROPE_TILE, t // 2)
    steps = t // 2 // rows
    cast_views, cast_in_specs, cast_out_specs, cast_shapes = _cast_specs(
        cast_params, steps, lambda i: i, layer=cast_layer)
    outs = pl.pallas_call(
        functools.partial(_rope_kernel, len(cast_params)),
        grid=(steps,),
        in_specs=[pl.BlockSpec((rows, 2), lambda i: (i, 0)),
                  pl.BlockSpec((1, RET_DK), lambda i: (0, 0))] + cast_in_specs,
        out_specs=[pl.BlockSpec((rows, RET_DK), lambda i: (i, 0))] * 2 + cast_out_specs,
        out_shape=[jax.ShapeDtypeStruct((t // 2, RET_DK), F32)] * 2 + cast_shapes,
        name="rope_tables",
    )(pos, freq2, *cast_views)
    return outs[0], outs[1], [o.reshape((1,) + p.shape[1:]) for o, p in zip(outs[2:], cast_params)]


def _mem_kv_kernel(mem_ref, g_ref, w_ref, kt_ref, v_ref):
    w = w_ref[...].astype(BF16)
    for i in range(mem_ref.shape[0]):
        mem_n = _rms_norm(mem_ref[i], g_ref[...]).astype(BF16)
        kv = _dot(mem_n, w)
        kt_ref[i] = kv[:, :XATTN_WIDTH].T.astype(BF16)
        v_ref[i] = kv[:, XATTN_WIDTH:].astype(BF16)


def _mem_kv(mem, mem_norm_g, w_mem_kv):
    depth = w_mem_kv.shape[0]
    b, m, d = mem.shape
    return pl.pallas_call(
        _mem_kv_kernel,
        grid=(depth,),
        in_specs=[pl.BlockSpec((b, m, d), lambda l: (0, 0, 0)),
                  pl.BlockSpec((None, 1, d), lambda l: (l, 0, 0)),
                  pl.BlockSpec((None, d, 2 * XATTN_WIDTH), lambda l: (l, 0, 0))],
        out_specs=[pl.BlockSpec((None, b, XATTN_WIDTH, m), lambda l: (l, 0, 0, 0)),
                   pl.BlockSpec((None, b, m, XATTN_WIDTH), lambda l: (l, 0, 0, 0))],
        out_shape=[jax.ShapeDtypeStruct((depth, b, XATTN_WIDTH, m), BF16),
                   jax.ShapeDtypeStruct((depth, b, m, XATTN_WIDTH), BF16)],
        name="mem_kv",
    )(mem, mem_norm_g.reshape(depth, 1, d), w_mem_kv)


def _pool_pair_weights(pool_w):
    zeros = jnp.zeros_like(pool_w[:, 0::2])
    top = jnp.concatenate([pool_w[:, 0::2], zeros], axis=-1)
    bottom = jnp.concatenate([zeros, pool_w[:, 1::2]], axis=-1)
    return jnp.concatenate([top, bottom], axis=-2)


def _retention_tables():
    c = RET_CHUNK
    log_gamma = np.log(1.0 - 2.0 ** (-5.0 - np.arange(RET_HEADS, dtype=np.float64)))
    idx = np.arange(c, dtype=np.float64)
    diff = idx[:, None] - idx[None, :]
    mask = np.where(diff[None] >= 0, np.exp(diff[None] * log_gamma[:, None, None]), 0.0)
    q_dec = np.exp((idx + 1.0)[None, :] * log_gamma[:, None])
    k_dec = np.exp((c - 1.0 - idx)[None, :] * log_gamma[:, None])
    q_dec = np.broadcast_to(q_dec[:, :, None], (RET_HEADS, c, RET_DK))
    k_dec = np.broadcast_to(k_dec[:, :, None], (RET_HEADS, c, RET_DK))
    chunk_decay = tuple(float(v) for v in np.exp(c * log_gamma))
    mask = np.concatenate([mask[0::2], mask[1::2]], axis=-1)
    as_f32 = lambda a: jnp.asarray(np.ascontiguousarray(a), dtype=F32)
    return as_f32(mask), as_f32(q_dec), as_f32(k_dec), chunk_decay


def _mixer_kernel(chunk_decay, layer, n_sub, n_cast, *refs):
    x_refs, refs = refs[:n_sub], refs[n_sub:]
    (cos_ref, sin_ref, kt_ref, vm_ref, mask_ref, qdec_ref, kdec_ref,
     gmix_ref, w_in_ref, pool_w_ref, pool_scale_ref, ret_g_ref,
     w_up_pool_ref, w_up_ret_ref, w_up_mem_ref, w_out_ref) = refs[:N_MIXER_INPUTS]
    cast_in = refs[N_MIXER_INPUTS:N_MIXER_INPUTS + n_cast]
    o_ref = refs[N_MIXER_INPUTS + n_cast]
    cast_out = refs[N_MIXER_INPUTS + n_cast + 1:N_MIXER_INPUTS + 2 * n_cast + 1]
    (h_ref, ubuf_ref, state_ref, qb_ref, kbd_ref, kd_ref, lhs_ref, rhs_ref, gsilu_ref,
     ypool_ref, yret_ref, ymem_ref, macc_ref, merged_ref) = refs[N_MIXER_INPUTS + 2 * n_cast + 1:]
    gmix = gmix_ref[layer:layer + 1, :]
    pool_scale = pool_scale_ref[layer:layer + 1, :]
    ret_g = ret_g_ref[layer:layer + 1, :]
    tm = x_refs[0].shape[0]
    n_chunks = tm // RET_CHUNK
    seq_step = pl.program_id(1)

    @pl.when(seq_step == 0)
    def _():
        ubuf_ref[0:POOL_HALO, :] = jnp.zeros((POOL_HALO, POOL_WIDTH), F32)
        state_ref[...] = jnp.zeros_like(state_ref)
        kbd_ref[...] = jnp.zeros_like(kbd_ref)

    def tile(sub):
        xr = slice(sub * tm, (sub + 1) * tm)
        tr = slice(sub * (tm // 2), (sub + 1) * (tm // 2))
        tile_start = (seq_step * n_sub + sub) * tm
        part = tm // MIXER_NORM_PARTS
        half_rows = tuple(slice(i * part, (i + 1) * part) for i in range(MIXER_NORM_PARTS))
        x_ref = x_refs[sub]
        for hr in half_rows:
            h_ref[hr, :] = _rms_norm(x_ref[hr, :], gmix).astype(BF16)

        def proj(off, width):
            return _dot(h_ref[...], w_in_ref[:, off:off + width])

        def chunk_rows(c):
            return slice(c * RET_CHUNK, (c + 1) * RET_CHUNK)

        def merge_chunk(b, y_ref, w_up_ref, j):
            cols = slice(j * MERGE_CHUNK, (j + 1) * MERGE_CHUNK)
            g = jax.nn.sigmoid(proj(OFF_GATE + b * D_MODEL + j * MERGE_CHUNK, MERGE_CHUNK))
            return g * _dot(y_ref[...], w_up_ref[:, cols])


        lane = lax.broadcasted_iota(jnp.int32, (tm // 2, RET_DK), 1)
        first = lane < RET_DK // 2
        cos_p = cos_ref[tr, :]
        sin_p = sin_ref[tr, :]
        cos_s = pltpu.roll(cos_p, RET_DK // 2, 1)
        sin_s = pltpu.roll(sin_p, RET_DK // 2, 1)
        cos2 = jnp.concatenate([jnp.where(first, cos_p, cos_s), jnp.where(first, cos_s, cos_p)], axis=0)
        sin2 = jnp.concatenate([jnp.where(first, -sin_p, sin_s), jnp.where(first, -sin_s, sin_p)], axis=0)
        q_all = jnp.concatenate(
            [_dot(h_ref[hr, :], w_in_ref[:, OFF_RQ:OFF_RQ + RET_QK_WIDTH]) for hr in half_rows], axis=0)
        k_all = proj(OFF_RK, RET_QK_WIDTH)
        for hh in range(RET_HEADS):
            qk = slice(hh * RET_DK, (hh + 1) * RET_DK)
            qh = q_all[:, qk]
            kh = k_all[:, qk]
            qr = qh * cos2 + pltpu.roll(qh, RET_DK // 2, 1) * sin2
            kr = (kh * cos2 + pltpu.roll(kh, RET_DK // 2, 1) * sin2) * (RET_DK ** -0.5)
            qb_ref[:, qk] = qr.astype(BF16)
            pair, side = divmod(hh, 2)
            diag = slice(side * RET_DK, (side + 1) * RET_DK)
            v_h = proj(OFF_RV + hh * RET_DV, RET_DV).astype(BF16)
            for c in range(n_chunks):
                rows = chunk_rows(c)
                lhs_ref[hh, rows, RET_CHUNK:] = (qr[rows] * qdec_ref[hh]).astype(BF16)
                kd_ref[rows, qk] = (kr[rows] * kdec_ref[hh]).astype(BF16)
                kbd_ref[c * HEAD_PAIRS + pair, diag, diag] = kr[rows].T.astype(BF16)
                rhs_ref[hh * n_chunks + c, 0:RET_CHUNK, :] = v_h[rows]

        ubuf_ref[POOL_HALO:POOL_HALO + tm, :] = proj(OFF_POOL, POOL_WIDTH)
        row = lax.broadcasted_iota(jnp.int32, (tm, 1), 0) + tile_start
        dwins = []
        for gi, w in enumerate(POOL_WINDOWS):
            cols = slice(gi * POOL_GROUP_DIM, (gi + 1) * POOL_GROUP_DIM)
            win = ubuf_ref[:, cols]
            shift = 1
            while shift < w:
                win = win + pltpu.roll(win, shift, 0)
                shift *= 2
            cnt = jnp.minimum(row + 1, w).astype(F32)
            dwin = win[POOL_HALO:, :] / cnt - ubuf_ref[POOL_HALO:POOL_HALO + tm, cols]
            dwins.append(dwin.astype(BF16))
            if gi % 2 == 1:
                cols2 = slice((gi - 1) * POOL_GROUP_DIM, (gi + 1) * POOL_GROUP_DIM)
                yg = _dot(jnp.concatenate(dwins[gi - 1:gi + 1], axis=1), pool_w_ref[gi // 2])
                ypool_ref[:, cols2] = (yg * pool_scale[:, cols2]).astype(BF16)
            vv = slice(gi * RET_DV, (gi + 1) * RET_DV)
            g_h = proj(OFF_RG + gi * RET_DV, RET_DV)
            gsilu_ref[:, vv] = g_h * jax.nn.sigmoid(g_h) * ret_g[:, vv]
        ubuf_ref[0:POOL_HALO, :] = ubuf_ref[tm:tm + POOL_HALO, :]

        for pair in range(HEAD_PAIRS):
            qk2 = slice(2 * pair * RET_DK, (2 * pair + 2) * RET_DK)
            for c in range(n_chunks):
                rows = chunk_rows(c)
                scores = _dot(qb_ref[rows, qk2], kbd_ref[c * HEAD_PAIRS + pair]) * mask_ref[pair]
                scores = scores.astype(BF16)
                lhs_ref[2 * pair, rows, 0:RET_CHUNK] = scores[:, :RET_CHUNK]
                lhs_ref[2 * pair + 1, rows, 0:RET_CHUNK] = scores[:, RET_CHUNK:]

        kvs = [[lax.dot_general(kd_ref[chunk_rows(c), hh * RET_DK:(hh + 1) * RET_DK],
                                rhs_ref[hh * n_chunks + c, 0:RET_CHUNK, :],
                                (((0,), (0,)), ((), ())), preferred_element_type=F32)
                for c in range(n_chunks)] for hh in range(RET_HEADS)]
        for hh in range(RET_HEADS):
            s = state_ref[hh]
            for c in range(n_chunks):
                rhs_ref[hh * n_chunks + c, RET_CHUNK:, :] = s.astype(BF16)
                s = chunk_decay[hh] * s + kvs[hh][c]
            state_ref[hh] = s

        xq = proj(OFF_XQ, XATTN_WIDTH).astype(BF16)

        for hh in range(RET_HEADS):
            vv = slice(hh * RET_DV, (hh + 1) * RET_DV)
            for c in range(n_chunks):
                rows = chunk_rows(c)
                y = _dot(lhs_ref[hh, rows, :], rhs_ref[hh * n_chunks + c])
                y = y * lax.rsqrt(jnp.mean(y * y, axis=-1, keepdims=True) + NORM_EPS)
                yret_ref[rows, vv] = (gsilu_ref[rows, vv] * y).astype(BF16)
            mc = slice(hh * MERGE_CHUNK, (hh + 1) * MERGE_CHUNK)
            macc_ref[:, mc] = merge_chunk(0, ypool_ref, w_up_pool_ref, hh)

        logits_ref, probs_ref = gsilu_ref, lhs_ref
        assert logits_ref.shape == (tm, XATTN_HEADS * N_MEM) and probs_ref.shape == (XATTN_HEADS, tm, N_MEM)
        for hh in range(XATTN_HEADS):
            hd = slice(hh * XATTN_DH, (hh + 1) * XATTN_DH)
            logits_ref[:, hh * N_MEM:(hh + 1) * N_MEM] = _dot(xq[:, hd], kt_ref[hd, :])
        for hh in range(XATTN_HEADS):
            s = logits_ref[:, hh * N_MEM:(hh + 1) * N_MEM]
            p = jnp.exp2((s - jnp.max(s, axis=-1, keepdims=True)) * SOFTMAX_EXP2_SCALE)
            probs_ref[hh] = (p / jnp.sum(p, axis=-1, keepdims=True)).astype(BF16)
            mc = slice(hh * MERGE_CHUNK, (hh + 1) * MERGE_CHUNK)
            macc_ref[:, mc] = macc_ref[:, mc] + merge_chunk(1, yret_ref, w_up_ret_ref, hh)
        for hh in range(XATTN_HEADS):
            hd = slice(hh * XATTN_DH, (hh + 1) * XATTN_DH)
            ymem_ref[:, hd] = _dot(probs_ref[hh], vm_ref[:, hd]).astype(BF16)

        for j in range(D_MODEL // MERGE_CHUNK):
            mc = slice(j * MERGE_CHUNK, (j + 1) * MERGE_CHUNK)
            merged_ref[:, mc] = (macc_ref[:, mc]
                                 + merge_chunk(2, ymem_ref, w_up_mem_ref, j)).astype(BF16)
        o_ref[xr, :] = x_ref[...] + _dot(merged_ref[...], w_out_ref[...])

    for sub in range(n_sub):
        tile(sub)
    _cast_blocks(cast_in, cast_out)


def _mixer_layer(layer, x2d, batch, cos2, sin2, kt, vm, tables, gmix, w_in, pool_w, pool_scale,
                 ret_g, w_up_pool, w_up_ret, w_up_mem, w_out, cast_params=(), weight_layer=0):
    t, d = x2d.shape
    seq = t // batch
    tm = min(TOKEN_TILE, seq)
    n_sub = MIXER_SUBTILES if seq % (MIXER_SUBTILES * tm) == 0 else 1
    bm = n_sub * tm
    steps = seq // bm
    mask, q_dec, k_dec, chunk_decay = tables
    tok = lambda b, s: (b * steps + s, 0)
    in_specs = [
        pl.BlockSpec((bm // 2, RET_DK), tok),
        pl.BlockSpec((bm // 2, RET_DK), tok),
        pl.BlockSpec((None, None, XATTN_WIDTH, N_MEM), lambda b, s: (layer, b, 0, 0)),
        pl.BlockSpec((None, None, N_MEM, XATTN_WIDTH), lambda b, s: (layer, b, 0, 0)),
        _const_spec(mask.shape), _const_spec(q_dec.shape), _const_spec(k_dec.shape),
        _const_spec(gmix.shape), _layer_spec(w_in.shape, weight_layer),
        _layer_spec(pool_w.shape, weight_layer), _const_spec(pool_scale.shape),
        _const_spec(ret_g.shape),
    ] + [_layer_spec(w.shape, weight_layer) for w in (w_up_pool, w_up_ret, w_up_mem, w_out)]
    assert len(in_specs) == N_MIXER_INPUTS
    x_specs = [pl.BlockSpec((tm, d), lambda b, s, i=i: ((b * steps + s) * n_sub + i, 0))
               for i in range(n_sub)]
    cast_views, cast_in_specs, cast_out_specs, cast_shapes = _cast_specs(
        cast_params, batch * steps, lambda b, s: b * steps + s, layer=layer)
    outs = pl.pallas_call(
        functools.partial(_mixer_kernel, chunk_decay, layer, n_sub, len(cast_params)),
        grid=(batch, steps),
        in_specs=x_specs + in_specs + cast_in_specs,
        out_specs=[pl.BlockSpec((bm, d), tok)] + cast_out_specs,
        out_shape=[jax.ShapeDtypeStruct((t, d), F32)] + cast_shapes,
        scratch_shapes=[
            pltpu.VMEM((tm, d), BF16),
            pltpu.VMEM((POOL_HALO + tm, POOL_WIDTH), F32),
            pltpu.VMEM((RET_HEADS, RET_DK, RET_DV), F32),
            pltpu.VMEM((tm, RET_QK_WIDTH), BF16),
            pltpu.VMEM(((tm // RET_CHUNK) * HEAD_PAIRS, 2 * RET_CHUNK, 2 * RET_DK), BF16),
            pltpu.VMEM((tm, RET_QK_WIDTH), BF16),
            pltpu.VMEM((RET_HEADS, tm, RET_CHUNK + RET_DK), BF16),
            pltpu.VMEM((RET_HEADS * (tm // RET_CHUNK), RET_CHUNK + RET_DK, RET_DV), BF16),
            pltpu.VMEM((tm, RET_V_WIDTH), F32),
            pltpu.VMEM((tm, POOL_WIDTH), BF16),
            pltpu.VMEM((tm, RET_V_WIDTH), BF16),
            pltpu.VMEM((tm, XATTN_WIDTH), BF16),
            pltpu.VMEM((tm, D_MODEL), F32),
            pltpu.VMEM((tm, D_MODEL), BF16),
        ],
        compiler_params=pltpu.CompilerParams(
            dimension_semantics=("arbitrary", "arbitrary"),
            vmem_limit_bytes=VMEM_LIMIT_BYTES),
        name="mixer",
    )(*([x2d] * n_sub), cos2, sin2, kt, vm, mask, q_dec, k_dec, gmix, w_in, pool_w, pool_scale, ret_g,
      w_up_pool, w_up_ret, w_up_mem, w_out, *cast_views)
    return outs[0], [o.reshape((1,) + p.shape[1:]) for o, p in zip(outs[1:], cast_params)]


def _mlp_kernel(apply_final_norm, layer, n_cast, *refs):
    x_ref, g_ref, w1_ref, w2_ref, gf_ref = refs[:N_MLP_INPUTS]
    cast_in = refs[N_MLP_INPUTS:N_MLP_INPUTS + n_cast]
    o_ref = refs[N_MLP_INPUTS + n_cast]
    cast_out = refs[N_MLP_INPUTS + n_cast + 1:N_MLP_INPUTS + 2 * n_cast + 1]
    (hid_ref,) = refs[N_MLP_INPUTS + 2 * n_cast + 1:]
    g = g_ref[layer:layer + 1, :]
    tm = x_ref.shape[0]
    n_parts = MLP_ROW_PARTS if tm % (MLP_ROW_PARTS * BF16_SUBLANES) == 0 else 1
    part_rows = [slice(i * (tm // n_parts), (i + 1) * (tm // n_parts)) for i in range(n_parts)]
    for rows in part_rows:
        h = _rms_norm(x_ref[rows, :], g).astype(BF16)
        for j in range(D_FF // N_CHUNK):
            nc = slice(j * N_CHUNK, (j + 1) * N_CHUNK)
            a = jnp.maximum(_dot(h, w1_ref[:, nc]), 0.0)
            hid_ref[rows, nc] = (a * a).astype(BF16)
    for rows in part_rows:
        y = x_ref[rows, :] + _dot(hid_ref[rows, :], w2_ref[...])
        if apply_final_norm:
            y = _rms_norm(y, gf_ref[...])
        o_ref[rows, :] = y
    _cast_blocks(cast_in, cast_out)


def _mlp_layer(layer, x2d, g, w1, w2, g_final, apply_final_norm, cast_params=(), cast_layer=None):
    t, d = x2d.shape
    tm = min(MLP_TOKEN_TILE, t)
    steps = t // tm
    cast_views, cast_in_specs, cast_out_specs, cast_shapes = _cast_specs(
        cast_params, steps, lambda i: i, layer=cast_layer)
    outs = pl.pallas_call(
        functools.partial(_mlp_kernel, apply_final_norm, layer, len(cast_params)),
        grid=(steps,),
        in_specs=[pl.BlockSpec((tm, d), lambda i: (i, 0)),
                  _const_spec(g.shape), _layer_spec(w1.shape, 0),
                  _layer_spec(w2.shape, 0), _const_spec(g_final.shape)] + cast_in_specs,
        out_specs=[pl.BlockSpec((tm, d), lambda i: (i, 0))] + cast_out_specs,
        out_shape=[jax.ShapeDtypeStruct((t, d), F32)] + cast_shapes,
        scratch_shapes=[pltpu.VMEM((tm, D_FF), BF16)],
        compiler_params=pltpu.CompilerParams(
            dimension_semantics=("arbitrary",),
            vmem_limit_bytes=VMEM_LIMIT_BYTES),
        name="mlp",
    )(x2d, g, w1, w2, g_final, *cast_views)
    return outs[0], [o.reshape((1,) + p.shape[1:]) for o, p in zip(outs[1:], cast_params)]


def kernel(x, mem, positions, norm_mix_g, w_in, pool_w, pool_scale, ret_norm_g, mem_norm_g,
           w_mem_kv, w_up_pool, w_up_ret, w_up_mem, w_out, norm_mlp_g, w_mlp1, w_mlp2,
           final_norm_g):
    batch, seq, d = x.shape
    depth = w_in.shape[0]
    assert d == D_MODEL and w_in.shape[-1] == IN_COLS and mem.shape[1] == N_MEM
    assert seq % RET_CHUNK == 0 and seq % min(TOKEN_TILE, seq) == 0

    mixer_w_f32 = (w_in, _pool_pair_weights(pool_w), w_up_pool, w_up_ret, w_up_mem, w_out)
    cos2, sin2, mixer_w = _rope_tables(positions, min(TOKEN_TILE, seq), mixer_w_f32, cast_layer=0)
    kt, vm = _mem_kv(mem, mem_norm_g, w_mem_kv)
    tables = _retention_tables()
    g_final = final_norm_g.reshape(1, -1)

    x2d = x.reshape(batch * seq, d)
    for l in range(depth):
        w_in_b, pool_w_b, w_up_pool_b, w_up_ret_b, w_up_mem_b, w_out_b = mixer_w
        x2d, mlp_w = _mixer_layer(
            l, x2d, batch, cos2, sin2, kt, vm, tables, norm_mix_g, w_in_b, pool_w_b, pool_scale,
            ret_norm_g, w_up_pool_b, w_up_ret_b, w_up_mem_b, w_out_b,
            cast_params=(w_mlp1, w_mlp2), weight_layer=0)
        last = l == depth - 1
        x2d, mixer_w = _mlp_layer(
            l, x2d, norm_mlp_g, mlp_w[0], mlp_w[1], g_final, apply_final_norm=last,
            cast_params=() if last else mixer_w_f32, cast_layer=None if last else l + 1)
    return x2d.reshape(batch, seq, d)
```

```python
import functools

import numpy as np
import jax
import jax.numpy as jnp
from jax import lax
from jax.experimental import pallas as pl
from jax.experimental.pallas import tpu as pltpu

D_MODEL = 1024
N_MEM = 256
POOL_WINDOWS = (2, 4, 8, 16)
POOL_GROUPS = 4
POOL_GROUP_DIM = 128
POOL_WIDTH = POOL_GROUPS * POOL_GROUP_DIM
POOL_HALO = 16
RET_HEADS = 4
RET_DK = 128
RET_DV = 256
RET_QK_WIDTH = RET_HEADS * RET_DK
RET_V_WIDTH = RET_HEADS * RET_DV
RET_CHUNK = 128
HEAD_PAIRS = RET_HEADS // 2
ROPE_BASE = 10000.0
XATTN_HEADS = 4
XATTN_DH = 128
XATTN_WIDTH = XATTN_HEADS * XATTN_DH
D_FF = 4 * D_MODEL
NORM_EPS = 1e-6

OFF_POOL = 0
OFF_RQ = OFF_POOL + POOL_WIDTH
OFF_RK = OFF_RQ + RET_QK_WIDTH
OFF_RV = OFF_RK + RET_QK_WIDTH
OFF_RG = OFF_RV + RET_V_WIDTH
OFF_XQ = OFF_RG + RET_V_WIDTH
OFF_GATE = OFF_XQ + XATTN_WIDTH
IN_COLS = OFF_GATE + 3 * D_MODEL

TOKEN_TILE = 512
MIXER_SUBTILES = 2
MIXER_NORM_PARTS = 2
MLP_TOKEN_TILE = 1024
MLP_ROW_PARTS = 2
N_CHUNK = 512
MERGE_CHUNK = 256
VMEM_LIMIT_BYTES = 60 * 1024 * 1024
BF16_SUBLANES = 16
N_MIXER_INPUTS = 17
N_MLP_INPUTS = 5

SOFTMAX_EXP2_SCALE = float(XATTN_DH ** -0.5 * np.log2(np.e))

BF16 = jnp.bfloat16
F32 = jnp.float32


def _dot(a, b):
    return jnp.dot(a, b, preferred_element_type=F32)


def _rms_norm(x, g):
    return x * lax.rsqrt(jnp.mean(x * x, axis=-1, keepdims=True) + NORM_EPS) * g


def _const_spec(shape):
    nd = len(shape)
    return pl.BlockSpec(shape, lambda *_: (0,) * nd, pipeline_mode=pl.Buffered(1))


def _layer_spec(stacked_shape, layer):
    nd = len(stacked_shape) - 1
    return pl.BlockSpec((None,) + tuple(stacked_shape[1:]), lambda *_: (layer,) + (0,) * nd,
                        pipeline_mode=pl.Buffered(1))


def _cast_specs(params, n_steps, index_of_step, layer=None):
    views, in_specs, out_specs, out_shapes = [], [], [], []
    for p in params:
        cols = p.shape[-1]
        rows_per_layer = int(np.prod(p.shape[1:-1]))
        rows = rows_per_layer if layer is not None else p.shape[0] * rows_per_layer
        slab, rem = divmod(rows, n_steps)
        assert rem == 0 and slab % BF16_SUBLANES == 0, (p.shape, n_steps)
        first = 0 if layer is None else layer * n_steps
        views.append(p.reshape(-1, cols))
        in_specs.append(pl.BlockSpec((slab, cols), lambda *g, f=first: (f + index_of_step(*g), 0)))
        out_specs.append(pl.BlockSpec((slab, cols), lambda *g: (index_of_step(*g), 0)))
        out_shapes.append(jax.ShapeDtypeStruct((rows, cols), BF16))
    return views, in_specs, out_specs, out_shapes


def _cast_blocks(in_refs, out_refs):
    for src_ref, dst_ref in zip(in_refs, out_refs):
        dst_ref[...] = src_ref[...].astype(BF16)


def _prologue_kernel(n_cast, pos_ref, freq_ref, mem_ref, g_ref, w_ref, *refs):
    cast_in = refs[:n_cast]
    cos_ref, sin_ref, kt_ref, v_ref = refs[n_cast:n_cast + 4]
    _cast_blocks(cast_in, refs[n_cast + 4:])
    mem_n = _rms_norm(mem_ref[...], g_ref[...]).astype(BF16)
    kv = _dot(mem_n, w_ref[...].astype(BF16))
    kt_ref[...] = kv[:, :XATTN_WIDTH].T.astype(BF16)
    v_ref[...] = kv[:, XATTN_WIDTH:].astype(BF16)
    lane = lax.broadcasted_iota(jnp.int32, cos_ref.shape, 1)
    pos = pos_ref[...]
    ang = jnp.where(lane < RET_DK // 2, pos[:, 0:1], pos[:, 1:2]) * freq_ref[...]
    cos_ref[...] = jnp.cos(ang)
    sin_ref[...] = jnp.sin(ang)


def _prologue(positions, token_tile, mem, mem_norm_g, w_mem_kv, cast_params=(), cast_layer=0):
    t = positions.size
    depth = w_mem_kv.shape[0]
    b, m, d = mem.shape
    steps = depth * b
    rows, rem = divmod(t // 2, steps)
    assert rem == 0 and rows % 8 == 0, (t, steps)
    half_tile = token_tile // 2
    half = RET_DK // 2
    inv_freq = ROPE_BASE ** (-jnp.arange(half, dtype=F32) / half)
    freq2 = jnp.concatenate([inv_freq, inv_freq]).reshape(1, RET_DK)
    pos = positions.astype(F32).reshape(t // token_tile, 2, half_tile)
    pos = jnp.swapaxes(pos, 1, 2).reshape(t // 2, 2)
    cast_views, cast_in_specs, cast_out_specs, cast_shapes = _cast_specs(
        cast_params, steps, lambda i: i, layer=cast_layer)
    outs = pl.pallas_call(
        functools.partial(_prologue_kernel, len(cast_params)),
        grid=(steps,),
        in_specs=[pl.BlockSpec((rows, 2), lambda i: (i, 0)),
                  pl.BlockSpec((1, RET_DK), lambda i: (0, 0)),
                  pl.BlockSpec((None, m, d), lambda i: (i % b, 0, 0)),
                  pl.BlockSpec((None, 1, d), lambda i: (i // b, 0, 0)),
                  pl.BlockSpec((None, d, 2 * XATTN_WIDTH), lambda i: (i // b, 0, 0))] + cast_in_specs,
        out_specs=[pl.BlockSpec((rows, RET_DK), lambda i: (i, 0))] * 2
                  + [pl.BlockSpec((None, None, XATTN_WIDTH, m), lambda i: (i // b, i % b, 0, 0)),
                     pl.BlockSpec((None, None, m, XATTN_WIDTH), lambda i: (i // b, i % b, 0, 0))]
                  + cast_out_specs,
        out_shape=[jax.ShapeDtypeStruct((t // 2, RET_DK), F32)] * 2
                  + [jax.ShapeDtypeStruct((depth, b, XATTN_WIDTH, m), BF16),
                     jax.ShapeDtypeStruct((depth, b, m, XATTN_WIDTH), BF16)] + cast_shapes,
        name="prologue",
    )(pos, freq2, mem, mem_norm_g.reshape(depth, 1, d), w_mem_kv, *cast_views)
    cos2, sin2, kt, vm = outs[:4]
    return cos2, sin2, kt, vm, [o.reshape((1,) + p.shape[1:]) for o, p in zip(outs[4:], cast_params)]


def _pool_pair_weights(pool_w):
    zeros = jnp.zeros_like(pool_w[:, 0::2])
    top = jnp.concatenate([pool_w[:, 0::2], zeros], axis=-1)
    bottom = jnp.concatenate([zeros, pool_w[:, 1::2]], axis=-1)
    return jnp.concatenate([top, bottom], axis=-2)


def _retention_tables():
    c = RET_CHUNK
    log_gamma = np.log(1.0 - 2.0 ** (-5.0 - np.arange(RET_HEADS, dtype=np.float64)))
    idx = np.arange(c, dtype=np.float64)
    diff = idx[:, None] - idx[None, :]
    mask = np.where(diff[None] >= 0, np.exp(diff[None] * log_gamma[:, None, None]), 0.0)
    q_dec = np.exp((idx + 1.0)[None, :] * log_gamma[:, None])
    k_dec = np.exp((c - 1.0 - idx)[None, :] * log_gamma[:, None])
    q_dec = np.broadcast_to(q_dec[:, :, None], (RET_HEADS, c, RET_DK))
    k_dec = np.broadcast_to(k_dec[:, :, None], (RET_HEADS, c, RET_DK))
    chunk_decay = tuple(float(v) for v in np.exp(c * log_gamma))
    mask = np.concatenate([mask[0::2], mask[1::2]], axis=-1)
    as_f32 = lambda a: jnp.asarray(np.ascontiguousarray(a), dtype=F32)
    return as_f32(mask), as_f32(q_dec), as_f32(k_dec), chunk_decay


def _mixer_kernel(chunk_decay, layer, n_sub, n_cast, *refs):
    (x_ref, cos_ref, sin_ref, kt_ref, vm_ref, mask_ref, qdec_ref, kdec_ref,
     gmix_ref, w_in_ref, pool_w_ref, pool_scale_ref, ret_g_ref,
     w_up_pool_ref, w_up_ret_ref, w_up_mem_ref, w_out_ref) = refs[:N_MIXER_INPUTS]
    cast_in = refs[N_MIXER_INPUTS:N_MIXER_INPUTS + n_cast]
    o_ref = refs[N_MIXER_INPUTS + n_cast]
    cast_out = refs[N_MIXER_INPUTS + n_cast + 1:N_MIXER_INPUTS + 2 * n_cast + 1]
    (h_ref, ubuf_ref, state_ref, qb_ref, kbd_ref, kd_ref, lhs_ref, rhs_ref, gsilu_ref,
     ypool_ref, yret_ref, ymem_ref, macc_ref, merged_ref) = refs[N_MIXER_INPUTS + 2 * n_cast + 1:]
    gmix = gmix_ref[layer:layer + 1, :]
    pool_scale = pool_scale_ref[layer:layer + 1, :]
    ret_g = ret_g_ref[layer:layer + 1, :]
    tm = x_ref.shape[0] // n_sub
    n_chunks = tm // RET_CHUNK
    seq_step = pl.program_id(1)

    @pl.when(seq_step == 0)
    def _():
        ubuf_ref[0:POOL_HALO, :] = jnp.zeros((POOL_HALO, POOL_WIDTH), F32)
        state_ref[...] = jnp.zeros_like(state_ref)
        kbd_ref[...] = jnp.zeros_like(kbd_ref)

    def tile(sub):
        xr = slice(sub * tm, (sub + 1) * tm)
        tr = slice(sub * (tm // 2), (sub + 1) * (tm // 2))
        tile_start = (seq_step * n_sub + sub) * tm
        part = tm // MIXER_NORM_PARTS
        half_rows = tuple(slice(i * part, (i + 1) * part) for i in range(MIXER_NORM_PARTS))
        for hr in half_rows:
            h_ref[hr, :] = _rms_norm(x_ref[sub * tm + hr.start:sub * tm + hr.stop, :],
                                     gmix).astype(BF16)

        def proj(off, width):
            return _dot(h_ref[...], w_in_ref[:, off:off + width])

        def chunk_rows(c):
            return slice(c * RET_CHUNK, (c + 1) * RET_CHUNK)

        def merge_chunk(b, y_ref, w_up_ref, j):
            cols = slice(j * MERGE_CHUNK, (j + 1) * MERGE_CHUNK)
            g = jax.nn.sigmoid(proj(OFF_GATE + b * D_MODEL + j * MERGE_CHUNK, MERGE_CHUNK))
            return g * _dot(y_ref[...], w_up_ref[:, cols])


        lane = lax.broadcasted_iota(jnp.int32, (tm // 2, RET_DK), 1)
        first = lane < RET_DK // 2
        cos_p = cos_ref[tr, :]
        sin_p = sin_ref[tr, :]
        cos_s = pltpu.roll(cos_p, RET_DK // 2, 1)
        sin_s = pltpu.roll(sin_p, RET_DK // 2, 1)
        cos2 = jnp.concatenate([jnp.where(first, cos_p, cos_s), jnp.where(first, cos_s, cos_p)], axis=0)
        sin2 = jnp.concatenate([jnp.where(first, -sin_p, sin_s), jnp.where(first, -sin_s, sin_p)], axis=0)
        q_all = jnp.concatenate(
            [_dot(h_ref[hr, :], w_in_ref[:, OFF_RQ:OFF_RQ + RET_QK_WIDTH]) for hr in half_rows], axis=0)
        k_all = proj(OFF_RK, RET_QK_WIDTH)
        for hh in range(RET_HEADS):
            qk = slice(hh * RET_DK, (hh + 1) * RET_DK)
            qh = q_all[:, qk]
            kh = k_all[:, qk]
            qr = qh * cos2 + pltpu.roll(qh, RET_DK // 2, 1) * sin2
            kr = (kh * cos2 + pltpu.roll(kh, RET_DK // 2, 1) * sin2) * (RET_DK ** -0.5)
            qb_ref[:, qk] = qr.astype(BF16)
            pair, side = divmod(hh, 2)
            diag = slice(side * RET_DK, (side + 1) * RET_DK)
            v_h = proj(OFF_RV + hh * RET_DV, RET_DV).astype(BF16)
            for c in range(n_chunks):
                rows = chunk_rows(c)
                lhs_ref[hh, rows, RET_CHUNK:] = (qr[rows] * qdec_ref[hh]).astype(BF16)
                kd_ref[rows, qk] = (kr[rows] * kdec_ref[hh]).astype(BF16)
                kbd_ref[c * HEAD_PAIRS + pair, diag, diag] = kr[rows].T.astype(BF16)
                rhs_ref[hh * n_chunks + c, 0:RET_CHUNK, :] = v_h[rows]

        ubuf_ref[POOL_HALO:POOL_HALO + tm, :] = proj(OFF_POOL, POOL_WIDTH)
        row = lax.broadcasted_iota(jnp.int32, (tm, 1), 0) + tile_start
        dwins = []
        for gi, w in enumerate(POOL_WINDOWS):
            cols = slice(gi * POOL_GROUP_DIM, (gi + 1) * POOL_GROUP_DIM)
            win = ubuf_ref[:, cols]
            shift = 1
            while shift < w:
                win = win + pltpu.roll(win, shift, 0)
                shift *= 2
            cnt = jnp.minimum(row + 1, w).astype(F32)
            dwin = win[POOL_HALO:, :] / cnt - ubuf_ref[POOL_HALO:POOL_HALO + tm, cols]
            dwins.append(dwin.astype(BF16))
            if gi % 2 == 1:
                cols2 = slice((gi - 1) * POOL_GROUP_DIM, (gi + 1) * POOL_GROUP_DIM)
                yg = _dot(jnp.concatenate(dwins[gi - 1:gi + 1], axis=1), pool_w_ref[gi // 2])
                ypool_ref[:, cols2] = (yg * pool_scale[:, cols2]).astype(BF16)
            vv = slice(gi * RET_DV, (gi + 1) * RET_DV)
            g_h = proj(OFF_RG + gi * RET_DV, RET_DV)
            gsilu_ref[:, vv] = g_h * jax.nn.sigmoid(g_h) * ret_g[:, vv]
        ubuf_ref[0:POOL_HALO, :] = ubuf_ref[tm:tm + POOL_HALO, :]

        for pair in range(HEAD_PAIRS):
            qk2 = slice(2 * pair * RET_DK, (2 * pair + 2) * RET_DK)
            for c in range(n_chunks):
                rows = chunk_rows(c)
                scores = _dot(qb_ref[rows, qk2], kbd_ref[c * HEAD_PAIRS + pair]) * mask_ref[pair]
                scores = scores.astype(BF16)
                lhs_ref[2 * pair, rows, 0:RET_CHUNK] = scores[:, :RET_CHUNK]
                lhs_ref[2 * pair + 1, rows, 0:RET_CHUNK] = scores[:, RET_CHUNK:]

        kvs = [[lax.dot_general(kd_ref[chunk_rows(c), hh * RET_DK:(hh + 1) * RET_DK],
                                rhs_ref[hh * n_chunks + c, 0:RET_CHUNK, :],
                                (((0,), (0,)), ((), ())), preferred_element_type=F32)
                for c in range(n_chunks)] for hh in range(RET_HEADS)]
        for hh in range(RET_HEADS):
            s = state_ref[hh]
            for c in range(n_chunks):
                rhs_ref[hh * n_chunks + c, RET_CHUNK:, :] = s.astype(BF16)
                s = chunk_decay[hh] * s + kvs[hh][c]
            state_ref[hh] = s

        xq = proj(OFF_XQ, XATTN_WIDTH).astype(BF16)

        for hh in range(RET_HEADS):
            vv = slice(hh * RET_DV, (hh + 1) * RET_DV)
            for c in range(n_chunks):
                rows = chunk_rows(c)
                y = _dot(lhs_ref[hh, rows, :], rhs_ref[hh * n_chunks + c])
                y = y * lax.rsqrt(jnp.mean(y * y, axis=-1, keepdims=True) + NORM_EPS)
                yret_ref[rows, vv] = (gsilu_ref[rows, vv] * y).astype(BF16)
            mc = slice(hh * MERGE_CHUNK, (hh + 1) * MERGE_CHUNK)
            macc_ref[:, mc] = merge_chunk(0, ypool_ref, w_up_pool_ref, hh)

        logits_ref, probs_ref = gsilu_ref, lhs_ref
        assert logits_ref.shape == (tm, XATTN_HEADS * N_MEM) and probs_ref.shape == (XATTN_HEADS, tm, N_MEM)
        for hh in range(XATTN_HEADS):
            hd = slice(hh * XATTN_DH, (hh + 1) * XATTN_DH)
            logits_ref[:, hh * N_MEM:(hh + 1) * N_MEM] = _dot(xq[:, hd], kt_ref[hd, :])
        for hh in range(XATTN_HEADS):
            s = logits_ref[:, hh * N_MEM:(hh + 1) * N_MEM]
            p = jnp.exp2((s - jnp.max(s, axis=-1, keepdims=True)) * SOFTMAX_EXP2_SCALE)
            probs_ref[hh] = (p / jnp.sum(p, axis=-1, keepdims=True)).astype(BF16)
            mc = slice(hh * MERGE_CHUNK, (hh + 1) * MERGE_CHUNK)
            macc_ref[:, mc] = macc_ref[:, mc] + merge_chunk(1, yret_ref, w_up_ret_ref, hh)
        for hh in range(XATTN_HEADS):
            hd = slice(hh * XATTN_DH, (hh + 1) * XATTN_DH)
            ymem_ref[:, hd] = _dot(probs_ref[hh], vm_ref[:, hd]).astype(BF16)

        for j in range(D_MODEL // MERGE_CHUNK):
            mc = slice(j * MERGE_CHUNK, (j + 1) * MERGE_CHUNK)
            merged_ref[:, mc] = (macc_ref[:, mc]
                                 + merge_chunk(2, ymem_ref, w_up_mem_ref, j)).astype(BF16)
        o_ref[xr, :] = x_ref[xr, :] + _dot(merged_ref[...], w_out_ref[...])

    for sub in range(n_sub):
        tile(sub)
    _cast_blocks(cast_in, cast_out)


def _mixer_layer(layer, x2d, batch, cos2, sin2, kt, vm, tables, gmix, w_in, pool_w, pool_scale,
                 ret_g, w_up_pool, w_up_ret, w_up_mem, w_out, cast_params=(), weight_layer=0):
    t, d = x2d.shape
    seq = t // batch
    tm = min(TOKEN_TILE, seq)
    n_sub = MIXER_SUBTILES if seq % (MIXER_SUBTILES * tm) == 0 else 1
    bm = n_sub * tm
    steps = seq // bm
    mask, q_dec, k_dec, chunk_decay = tables
    tok = lambda b, s: (b * steps + s, 0)
    in_specs = [
        pl.BlockSpec((bm, d), tok),
        pl.BlockSpec((bm // 2, RET_DK), tok),
        pl.BlockSpec((bm // 2, RET_DK), tok),
        pl.BlockSpec((None, None, XATTN_WIDTH, N_MEM), lambda b, s: (layer, b, 0, 0)),
        pl.BlockSpec((None, None, N_MEM, XATTN_WIDTH), lambda b, s: (layer, b, 0, 0)),
        _const_spec(mask.shape), _const_spec(q_dec.shape), _const_spec(k_dec.shape),
        _const_spec(gmix.shape), _layer_spec(w_in.shape, weight_layer),
        _layer_spec(pool_w.shape, weight_layer), _const_spec(pool_scale.shape),
        _const_spec(ret_g.shape),
    ] + [_layer_spec(w.shape, weight_layer) for w in (w_up_pool, w_up_ret, w_up_mem, w_out)]
    assert len(in_specs) == N_MIXER_INPUTS
    cast_views, cast_in_specs, cast_out_specs, cast_shapes = _cast_specs(
        cast_params, batch * steps, lambda b, s: b * steps + s, layer=layer)
    outs = pl.pallas_call(
        functools.partial(_mixer_kernel, chunk_decay, layer, n_sub, len(cast_params)),
        grid=(batch, steps),
        in_specs=in_specs + cast_in_specs,
        out_specs=[pl.BlockSpec((bm, d), tok)] + cast_out_specs,
        out_shape=[jax.ShapeDtypeStruct((t, d), F32)] + cast_shapes,
        scratch_shapes=[
            pltpu.VMEM((tm, d), BF16),
            pltpu.VMEM((POOL_HALO + tm, POOL_WIDTH), F32),
            pltpu.VMEM((RET_HEADS, RET_DK, RET_DV), F32),
            pltpu.VMEM((tm, RET_QK_WIDTH), BF16),
            pltpu.VMEM(((tm // RET_CHUNK) * HEAD_PAIRS, 2 * RET_CHUNK, 2 * RET_DK), BF16),
            pltpu.VMEM((tm, RET_QK_WIDTH), BF16),
            pltpu.VMEM((RET_HEADS, tm, RET_CHUNK + RET_DK), BF16),
            pltpu.VMEM((RET_HEADS * (tm // RET_CHUNK), RET_CHUNK + RET_DK, RET_DV), BF16),
            pltpu.VMEM((tm, RET_V_WIDTH), F32),
            pltpu.VMEM((tm, POOL_WIDTH), BF16),
            pltpu.VMEM((tm, RET_V_WIDTH), BF16),
            pltpu.VMEM((tm, XATTN_WIDTH), BF16),
            pltpu.VMEM((tm, D_MODEL), F32),
            pltpu.VMEM((tm, D_MODEL), BF16),
        ],
        compiler_params=pltpu.CompilerParams(
            dimension_semantics=("arbitrary", "arbitrary"),
            vmem_limit_bytes=VMEM_LIMIT_BYTES),
        name="mixer",
    )(x2d, cos2, sin2, kt, vm, mask, q_dec, k_dec, gmix, w_in, pool_w, pool_scale, ret_g,
      w_up_pool, w_up_ret, w_up_mem, w_out, *cast_views)
    return outs[0], [o.reshape((1,) + p.shape[1:]) for o, p in zip(outs[1:], cast_params)]


def _mlp_kernel(apply_final_norm, layer, n_cast, *refs):
    x_ref, g_ref, w1_ref, w2_ref, gf_ref = refs[:N_MLP_INPUTS]
    cast_in = refs[N_MLP_INPUTS:N_MLP_INPUTS + n_cast]
    o_ref = refs[N_MLP_INPUTS + n_cast]
    cast_out = refs[N_MLP_INPUTS + n_cast + 1:N_MLP_INPUTS + 2 * n_cast + 1]
    (hid_ref,) = refs[N_MLP_INPUTS + 2 * n_cast + 1:]
    g = g_ref[layer:layer + 1, :]
    tm = x_ref.shape[0]
    n_parts = MLP_ROW_PARTS if tm % (MLP_ROW_PARTS * BF16_SUBLANES) == 0 else 1
    part_rows = [slice(i * (tm // n_parts), (i + 1) * (tm // n_parts)) for i in range(n_parts)]
    for rows in part_rows:
        h = _rms_norm(x_ref[rows, :], g).astype(BF16)
        for j in range(D_FF // N_CHUNK):
            nc = slice(j * N_CHUNK, (j + 1) * N_CHUNK)
            a = jnp.maximum(_dot(h, w1_ref[:, nc]), 0.0)
            hid_ref[rows, nc] = (a * a).astype(BF16)
    for rows in part_rows:
        y = x_ref[rows, :] + _dot(hid_ref[rows, :], w2_ref[...])
        if apply_final_norm:
            y = _rms_norm(y, gf_ref[...])
        o_ref[rows, :] = y
    _cast_blocks(cast_in, cast_out)


def _mlp_layer(layer, x2d, g, w1, w2, g_final, apply_final_norm, cast_params=(), cast_layer=None):
    t, d = x2d.shape
    tm = min(MLP_TOKEN_TILE, t)
    steps = t // tm
    cast_views, cast_in_specs, cast_out_specs, cast_shapes = _cast_specs(
        cast_params, steps, lambda i: i, layer=cast_layer)
    outs = pl.pallas_call(
        functools.partial(_mlp_kernel, apply_final_norm, layer, len(cast_params)),
        grid=(steps,),
        in_specs=[pl.BlockSpec((tm, d), lambda i: (i, 0)),
                  _const_spec(g.shape), _layer_spec(w1.shape, 0),
                  _layer_spec(w2.shape, 0), _const_spec(g_final.shape)] + cast_in_specs,
        out_specs=[pl.BlockSpec((tm, d), lambda i: (i, 0))] + cast_out_specs,
        out_shape=[jax.ShapeDtypeStruct((t, d), F32)] + cast_shapes,
        scratch_shapes=[pltpu.VMEM((tm, D_FF), BF16)],
        compiler_params=pltpu.CompilerParams(
            dimension_semantics=("arbitrary",),
            vmem_limit_bytes=VMEM_LIMIT_BYTES),
        name="mlp",
    )(x2d, g, w1, w2, g_final, *cast_views)
    return outs[0], [o.reshape((1,) + p.shape[1:]) for o, p in zip(outs[1:], cast_params)]


def kernel(x, mem, positions, norm_mix_g, w_in, pool_w, pool_scale, ret_norm_g, mem_norm_g,
           w_mem_kv, w_up_pool, w_up_ret, w_up_mem, w_out, norm_mlp_g, w_mlp1, w_mlp2,
           final_norm_g):
    batch, seq, d = x.shape
    depth = w_in.shape[0]
    assert d == D_MODEL and w_in.shape[-1] == IN_COLS and mem.shape[1] == N_MEM
    assert seq % RET_CHUNK == 0 and seq % min(TOKEN_TILE, seq) == 0

    mixer_w_f32 = (w_in, _pool_pair_weights(pool_w), w_up_pool, w_up_ret, w_up_mem, w_out)
    cos2, sin2, kt, vm, mixer_w = _prologue(positions, min(TOKEN_TILE, seq), mem, mem_norm_g, w_mem_kv,
                                            mixer_w_f32, cast_layer=0)
    tables = _retention_tables()
    g_final = final_norm_g.reshape(1, -1)

    x2d = x.reshape(batch * seq, d)
    for l in range(depth):
        w_in_b, pool_w_b, w_up_pool_b, w_up_ret_b, w_up_mem_b, w_out_b = mixer_w
        x2d, mlp_w = _mixer_layer(
            l, x2d, batch, cos2, sin2, kt, vm, tables, norm_mix_g, w_in_b, pool_w_b, pool_scale,
            ret_norm_g, w_up_pool_b, w_up_ret_b, w_up_mem_b, w_out_b,
            cast_params=(w_mlp1, w_mlp2), weight_layer=0)
        last = l == depth - 1
        x2d, mixer_w = _mlp_layer(
            l, x2d, norm_mlp_g, mlp_w[0], mlp_w[1], g_final, apply_final_norm=last,
            cast_params=() if last else mixer_w_f32, cast_layer=None if last else l + 1)
    return x2d.reshape(batch, seq, d)
```

```python
import functools

import numpy as np
import jax
import jax.numpy as jnp
from jax import lax
from jax.experimental import pallas as pl
from jax.experimental.pallas import tpu as pltpu

D_MODEL = 1024
N_MEM = 256
POOL_WINDOWS = (2, 4, 8, 16)
POOL_GROUPS = 4
POOL_GROUP_DIM = 128
POOL_WIDTH = POOL_GROUPS * POOL_GROUP_DIM
POOL_HALO = 16
RET_HEADS = 4
RET_DK = 128
RET_DV = 256
RET_QK_WIDTH = RET_HEADS * RET_DK
RET_V_WIDTH = RET_HEADS * RET_DV
RET_CHUNK = 128
HEAD_PAIRS = RET_HEADS // 2
ROPE_BASE = 10000.0
XATTN_HEADS = 4
XATTN_DH = 128
XATTN_WIDTH = XATTN_HEADS * XATTN_DH
D_FF = 4 * D_MODEL
NORM_EPS = 1e-6

OFF_POOL = 0
OFF_RQ = OFF_POOL + POOL_WIDTH
OFF_RK = OFF_RQ + RET_QK_WIDTH
OFF_RV = OFF_RK + RET_QK_WIDTH
OFF_RG = OFF_RV + RET_V_WIDTH
OFF_XQ = OFF_RG + RET_V_WIDTH
OFF_GATE = OFF_XQ + XATTN_WIDTH
IN_COLS = OFF_GATE + 3 * D_MODEL

TOKEN_TILE = 512
MIXER_SUBTILES = 2
MIXER_NORM_PARTS = 2
MLP_TOKEN_TILE = 1024
MLP_ROW_PARTS = 2
N_CHUNK = 512
MERGE_CHUNK = 256
VMEM_LIMIT_BYTES = 60 * 1024 * 1024
BF16_SUBLANES = 16
N_MIXER_INPUTS = 17
N_MLP_INPUTS = 5

SOFTMAX_EXP2_SCALE = float(XATTN_DH ** -0.5 * np.log2(np.e))

BF16 = jnp.bfloat16
F32 = jnp.float32


def _dot(a, b):
    return jnp.dot(a, b, preferred_element_type=F32)


def _rms_norm(x, g):
    return x * lax.rsqrt(jnp.mean(x * x, axis=-1, keepdims=True) + NORM_EPS) * g


def _const_spec(shape):
    nd = len(shape)
    return pl.BlockSpec(shape, lambda *_: (0,) * nd, pipeline_mode=pl.Buffered(1))


def _layer_spec(stacked_shape, layer):
    nd = len(stacked_shape) - 1
    return pl.BlockSpec((None,) + tuple(stacked_shape[1:]), lambda *_: (layer,) + (0,) * nd,
                        pipeline_mode=pl.Buffered(1))


def _cast_specs(params, n_steps, index_of_step, layer=None):
    views, in_specs, out_specs, out_shapes = [], [], [], []
    for p in params:
        cols = p.shape[-1]
        rows_per_layer = int(np.prod(p.shape[1:-1]))
        rows = rows_per_layer if layer is not None else p.shape[0] * rows_per_layer
        slab, rem = divmod(rows, n_steps)
        assert rem == 0 and slab % BF16_SUBLANES == 0, (p.shape, n_steps)
        first = 0 if layer is None else layer * n_steps
        views.append(p.reshape(-1, cols))
        in_specs.append(pl.BlockSpec((slab, cols), lambda *g, f=first: (f + index_of_step(*g), 0)))
        out_specs.append(pl.BlockSpec((slab, cols), lambda *g: (index_of_step(*g), 0)))
        out_shapes.append(jax.ShapeDtypeStruct((rows, cols), BF16))
    return views, in_specs, out_specs, out_shapes


def _cast_blocks(in_refs, out_refs):
    for src_ref, dst_ref in zip(in_refs, out_refs):
        dst_ref[...] = src_ref[...].astype(BF16)


def _prologue_kernel(n_cast, pos_ref, freq_ref, mem_ref, g_ref, w_ref, *refs):
    cast_in = refs[:n_cast]
    cos_ref, sin_ref, kt_ref, v_ref = refs[n_cast:n_cast + 4]
    _cast_blocks(cast_in, refs[n_cast + 4:])
    mem_n = _rms_norm(mem_ref[...], g_ref[...]).astype(BF16)
    kv = _dot(mem_n, w_ref[...].astype(BF16))
    kt_ref[...] = kv[:, :XATTN_WIDTH].T.astype(BF16)
    v_ref[...] = kv[:, XATTN_WIDTH:].astype(BF16)
    half_tile = pos_ref.shape[1]
    lane = lax.broadcasted_iota(jnp.int32, (half_tile, RET_DK), 1)
    pos_t = pos_ref[...].T
    for j in range(pos_ref.shape[0] // 2):
        ang = jnp.where(lane < RET_DK // 2, pos_t[:, 2 * j:2 * j + 1],
                        pos_t[:, 2 * j + 1:2 * j + 2]) * freq_ref[...]
        tile_rows = slice(j * half_tile, (j + 1) * half_tile)
        cos_ref[tile_rows, :] = jnp.cos(ang)
        sin_ref[tile_rows, :] = jnp.sin(ang)


def _prologue(positions, token_tile, mem, mem_norm_g, w_mem_kv, cast_params=(), cast_layer=0):
    t = positions.size
    depth = w_mem_kv.shape[0]
    b, m, d = mem.shape
    steps = depth * b
    rows, rem = divmod(t // 2, steps)
    assert rem == 0 and rows % 8 == 0, (t, steps)
    half_tile = token_tile // 2
    half = RET_DK // 2
    inv_freq = ROPE_BASE ** (-jnp.arange(half, dtype=F32) / half)
    freq2 = jnp.concatenate([inv_freq, inv_freq]).reshape(1, RET_DK)
    assert rows % half_tile == 0
    pos_rows = 2 * (rows // half_tile)
    pos = positions.astype(F32).reshape(t // half_tile, half_tile)
    cast_views, cast_in_specs, cast_out_specs, cast_shapes = _cast_specs(
        cast_params, steps, lambda i: i, layer=cast_layer)
    outs = pl.pallas_call(
        functools.partial(_prologue_kernel, len(cast_params)),
        grid=(steps,),
        in_specs=[pl.BlockSpec((pos_rows, half_tile), lambda i: (i, 0)),
                  pl.BlockSpec((1, RET_DK), lambda i: (0, 0)),
                  pl.BlockSpec((None, m, d), lambda i: (i % b, 0, 0)),
                  pl.BlockSpec((None, 1, d), lambda i: (i // b, 0, 0)),
                  pl.BlockSpec((None, d, 2 * XATTN_WIDTH), lambda i: (i // b, 0, 0))] + cast_in_specs,
        out_specs=[pl.BlockSpec((rows, RET_DK), lambda i: (i, 0))] * 2
                  + [pl.BlockSpec((None, None, XATTN_WIDTH, m), lambda i: (i // b, i % b, 0, 0)),
                     pl.BlockSpec((None, None, m, XATTN_WIDTH), lambda i: (i // b, i % b, 0, 0))]
                  + cast_out_specs,
        out_shape=[jax.ShapeDtypeStruct((t // 2, RET_DK), F32)] * 2
                  + [jax.ShapeDtypeStruct((depth, b, XATTN_WIDTH, m), BF16),
                     jax.ShapeDtypeStruct((depth, b, m, XATTN_WIDTH), BF16)] + cast_shapes,
        name="prologue",
    )(pos, freq2, mem, mem_norm_g.reshape(depth, 1, d), w_mem_kv, *cast_views)
    cos2, sin2, kt, vm = outs[:4]
    return cos2, sin2, kt, vm, [o.reshape((1,) + p.shape[1:]) for o, p in zip(outs[4:], cast_params)]


def _pool_pair_weights(pool_w):
    zeros = jnp.zeros_like(pool_w[:, 0::2])
    top = jnp.concatenate([pool_w[:, 0::2], zeros], axis=-1)
    bottom = jnp.concatenate([zeros, pool_w[:, 1::2]], axis=-1)
    return jnp.concatenate([top, bottom], axis=-2)


def _retention_tables():
    c = RET_CHUNK
    log_gamma = np.log(1.0 - 2.0 ** (-5.0 - np.arange(RET_HEADS, dtype=np.float64)))
    idx = np.arange(c, dtype=np.float64)
    diff = idx[:, None] - idx[None, :]
    mask = np.where(diff[None] >= 0, np.exp(diff[None] * log_gamma[:, None, None]), 0.0)
    q_dec = np.exp((idx + 1.0)[None, :] * log_gamma[:, None])
    k_dec = np.exp((c - 1.0 - idx)[None, :] * log_gamma[:, None])
    q_dec = np.broadcast_to(q_dec[:, :, None], (RET_HEADS, c, RET_DK))
    k_dec = np.broadcast_to(k_dec[:, :, None], (RET_HEADS, c, RET_DK))
    chunk_decay = tuple(float(v) for v in np.exp(c * log_gamma))
    mask = np.concatenate([mask[0::2], mask[1::2]], axis=-1)
    as_f32 = lambda a: jnp.asarray(np.ascontiguousarray(a), dtype=F32)
    return as_f32(mask), as_f32(q_dec), as_f32(k_dec), chunk_decay


def _mixer_kernel(chunk_decay, layer, n_sub, n_cast, *refs):
    (x_ref, cos_ref, sin_ref, kt_ref, vm_ref, mask_ref, qdec_ref, kdec_ref,
     gmix_ref, w_in_ref, pool_w_ref, pool_scale_ref, ret_g_ref,
     w_up_pool_ref, w_up_ret_ref, w_up_mem_ref, w_out_ref) = refs[:N_MIXER_INPUTS]
    cast_in = refs[N_MIXER_INPUTS:N_MIXER_INPUTS + n_cast]
    o_ref = refs[N_MIXER_INPUTS + n_cast]
    cast_out = refs[N_MIXER_INPUTS + n_cast + 1:N_MIXER_INPUTS + 2 * n_cast + 1]
    (h_ref, ubuf_ref, state_ref, qb_ref, kbd_ref, kd_ref, lhs_ref, rhs_ref, gsilu_ref,
     ypool_ref, yret_ref, ymem_ref, macc_ref, merged_ref) = refs[N_MIXER_INPUTS + 2 * n_cast + 1:]
    gmix = gmix_ref[layer:layer + 1, :]
    pool_scale = pool_scale_ref[layer:layer + 1, :]
    ret_g = ret_g_ref[layer:layer + 1, :]
    tm = x_ref.shape[0] // n_sub
    n_chunks = tm // RET_CHUNK
    seq_step = pl.program_id(1)

    @pl.when(seq_step == 0)
    def _():
        ubuf_ref[0:POOL_HALO, :] = jnp.zeros((POOL_HALO, POOL_WIDTH), F32)
        state_ref[...] = jnp.zeros_like(state_ref)
        kbd_ref[...] = jnp.zeros_like(kbd_ref)

    def tile(sub):
        xr = slice(sub * tm, (sub + 1) * tm)
        tr = slice(sub * (tm // 2), (sub + 1) * (tm // 2))
        tile_start = (seq_step * n_sub + sub) * tm
        part = tm // MIXER_NORM_PARTS
        half_rows = tuple(slice(i * part, (i + 1) * part) for i in range(MIXER_NORM_PARTS))
        for hr in half_rows:
            h_ref[hr, :] = _rms_norm(x_ref[sub * tm + hr.start:sub * tm + hr.stop, :],
                                     gmix).astype(BF16)

        def proj(off, width):
            return _dot(h_ref[...], w_in_ref[:, off:off + width])

        def chunk_rows(c):
            return slice(c * RET_CHUNK, (c + 1) * RET_CHUNK)

        def merge_chunk(b, y_ref, w_up_ref, j):
            cols = slice(j * MERGE_CHUNK, (j + 1) * MERGE_CHUNK)
            g = jax.nn.sigmoid(proj(OFF_GATE + b * D_MODEL + j * MERGE_CHUNK, MERGE_CHUNK))
            return g * _dot(y_ref[...], w_up_ref[:, cols])


        lane = lax.broadcasted_iota(jnp.int32, (tm // 2, RET_DK), 1)
        first = lane < RET_DK // 2
        cos_p = cos_ref[tr, :]
        sin_p = sin_ref[tr, :]
        cos_s = pltpu.roll(cos_p, RET_DK // 2, 1)
        sin_s = pltpu.roll(sin_p, RET_DK // 2, 1)
        cos2 = jnp.concatenate([jnp.where(first, cos_p, cos_s), jnp.where(first, cos_s, cos_p)], axis=0)
        sin2 = jnp.concatenate([jnp.where(first, -sin_p, sin_s), jnp.where(first, -sin_s, sin_p)], axis=0)
        q_all = jnp.concatenate(
            [_dot(h_ref[hr, :], w_in_ref[:, OFF_RQ:OFF_RQ + RET_QK_WIDTH]) for hr in half_rows], axis=0)
        k_all = proj(OFF_RK, RET_QK_WIDTH)
        for hh in range(RET_HEADS):
            qk = slice(hh * RET_DK, (hh + 1) * RET_DK)
            qh = q_all[:, qk]
            kh = k_all[:, qk]
            qr = qh * cos2 + pltpu.roll(qh, RET_DK // 2, 1) * sin2
            kr = (kh * cos2 + pltpu.roll(kh, RET_DK // 2, 1) * sin2) * (RET_DK ** -0.5)
            qb_ref[:, qk] = qr.astype(BF16)
            pair, side = divmod(hh, 2)
            diag = slice(side * RET_DK, (side + 1) * RET_DK)
            v_h = proj(OFF_RV + hh * RET_DV, RET_DV).astype(BF16)
            for c in range(n_chunks):
                rows = chunk_rows(c)
                lhs_ref[hh, rows, RET_CHUNK:] = (qr[rows] * qdec_ref[hh]).astype(BF16)
                kd_ref[rows, qk] = (kr[rows] * kdec_ref[hh]).astype(BF16)
                kbd_ref[c * HEAD_PAIRS + pair, diag, diag] = kr[rows].T.astype(BF16)
                rhs_ref[hh * n_chunks + c, 0:RET_CHUNK, :] = v_h[rows]

        ubuf_ref[POOL_HALO:POOL_HALO + tm, :] = proj(OFF_POOL, POOL_WIDTH)
        row = lax.broadcasted_iota(jnp.int32, (tm, 1), 0) + tile_start
        dwins = []
        for gi, w in enumerate(POOL_WINDOWS):
            cols = slice(gi * POOL_GROUP_DIM, (gi + 1) * POOL_GROUP_DIM)
            win = ubuf_ref[:, cols]
            shift = 1
            while shift < w:
                win = win + pltpu.roll(win, shift, 0)
                shift *= 2
            cnt = jnp.minimum(row + 1, w).astype(F32)
            dwin = win[POOL_HALO:, :] / cnt - ubuf_ref[POOL_HALO:POOL_HALO + tm, cols]
            dwins.append(dwin.astype(BF16))
            if gi % 2 == 1:
                cols2 = slice((gi - 1) * POOL_GROUP_DIM, (gi + 1) * POOL_GROUP_DIM)
                yg = _dot(jnp.concatenate(dwins[gi - 1:gi + 1], axis=1), pool_w_ref[gi // 2])
                ypool_ref[:, cols2] = (yg * pool_scale[:, cols2]).astype(BF16)
            vv = slice(gi * RET_DV, (gi + 1) * RET_DV)
            g_h = proj(OFF_RG + gi * RET_DV, RET_DV)
            gsilu_ref[:, vv] = g_h * jax.nn.sigmoid(g_h) * ret_g[:, vv]
        ubuf_ref[0:POOL_HALO, :] = ubuf_ref[tm:tm + POOL_HALO, :]

        for pair in range(HEAD_PAIRS):
            qk2 = slice(2 * pair * RET_DK, (2 * pair + 2) * RET_DK)
            for c in range(n_chunks):
                rows = chunk_rows(c)
                scores = _dot(qb_ref[rows, qk2], kbd_ref[c * HEAD_PAIRS + pair]) * mask_ref[pair]
                scores = scores.astype(BF16)
                lhs_ref[2 * pair, rows, 0:RET_CHUNK] = scores[:, :RET_CHUNK]
                lhs_ref[2 * pair + 1, rows, 0:RET_CHUNK] = scores[:, RET_CHUNK:]

        kvs = [[lax.dot_general(kd_ref[chunk_rows(c), hh * RET_DK:(hh + 1) * RET_DK],
                                rhs_ref[hh * n_chunks + c, 0:RET_CHUNK, :],
                                (((0,), (0,)), ((), ())), preferred_element_type=F32)
                for c in range(n_chunks)] for hh in range(RET_HEADS)]
        for hh in range(RET_HEADS):
            s = state_ref[hh]
            for c in range(n_chunks):
                rhs_ref[hh * n_chunks + c, RET_CHUNK:, :] = s.astype(BF16)
                s = chunk_decay[hh] * s + kvs[hh][c]
            state_ref[hh] = s

        xq = proj(OFF_XQ, XATTN_WIDTH).astype(BF16)

        for hh in range(RET_HEADS):
            vv = slice(hh * RET_DV, (hh + 1) * RET_DV)
            for c in range(n_chunks):
                rows = chunk_rows(c)
                y = _dot(lhs_ref[hh, rows, :], rhs_ref[hh * n_chunks + c])
                y = y * lax.rsqrt(jnp.mean(y * y, axis=-1, keepdims=True) + NORM_EPS)
                yret_ref[rows, vv] = (gsilu_ref[rows, vv] * y).astype(BF16)
            mc = slice(hh * MERGE_CHUNK, (hh + 1) * MERGE_CHUNK)
            macc_ref[:, mc] = merge_chunk(0, ypool_ref, w_up_pool_ref, hh)

        logits_ref, probs_ref = gsilu_ref, lhs_ref
        assert logits_ref.shape == (tm, XATTN_HEADS * N_MEM) and probs_ref.shape == (XATTN_HEADS, tm, N_MEM)
        for hh in range(XATTN_HEADS):
            hd = slice(hh * XATTN_DH, (hh + 1) * XATTN_DH)
            logits_ref[:, hh * N_MEM:(hh + 1) * N_MEM] = _dot(xq[:, hd], kt_ref[hd, :])
        for hh in range(XATTN_HEADS):
            s = logits_ref[:, hh * N_MEM:(hh + 1) * N_MEM]
            p = jnp.exp2((s - jnp.max(s, axis=-1, keepdims=True)) * SOFTMAX_EXP2_SCALE)
            probs_ref[hh] = (p / jnp.sum(p, axis=-1, keepdims=True)).astype(BF16)
            mc = slice(hh * MERGE_CHUNK, (hh + 1) * MERGE_CHUNK)
            macc_ref[:, mc] = macc_ref[:, mc] + merge_chunk(1, yret_ref, w_up_ret_ref, hh)
        for hh in range(XATTN_HEADS):
            hd = slice(hh * XATTN_DH, (hh + 1) * XATTN_DH)
            ymem_ref[:, hd] = _dot(probs_ref[hh], vm_ref[:, hd]).astype(BF16)

        for j in range(D_MODEL // MERGE_CHUNK):
            mc = slice(j * MERGE_CHUNK, (j + 1) * MERGE_CHUNK)
            merged_ref[:, mc] = (macc_ref[:, mc]
                                 + merge_chunk(2, ymem_ref, w_up_mem_ref, j)).astype(BF16)
        o_ref[xr, :] = x_ref[xr, :] + _dot(merged_ref[...], w_out_ref[...])

    for sub in range(n_sub):
        tile(sub)
    _cast_blocks(cast_in, cast_out)


def _mixer_layer(layer, x2d, batch, cos2, sin2, kt, vm, tables, gmix, w_in, pool_w, pool_scale,
                 ret_g, w_up_pool, w_up_ret, w_up_mem, w_out, cast_params=(), weight_layer=0):
    t, d = x2d.shape
    seq = t // batch
    tm = min(TOKEN_TILE, seq)
    n_sub = MIXER_SUBTILES if seq % (MIXER_SUBTILES * tm) == 0 else 1
    bm = n_sub * tm
    steps = seq // bm
    mask, q_dec, k_dec, chunk_decay = tables
    tok = lambda b, s: (b * steps + s, 0)
    in_specs = [
        pl.BlockSpec((bm, d), tok),
        pl.BlockSpec((bm // 2, RET_DK), tok),
        pl.BlockSpec((bm // 2, RET_DK), tok),
        pl.BlockSpec((None, None, XATTN_WIDTH, N_MEM), lambda b, s: (layer, b, 0, 0)),
        pl.BlockSpec((None, None, N_MEM, XATTN_WIDTH), lambda b, s: (layer, b, 0, 0)),
        _const_spec(mask.shape), _const_spec(q_dec.shape), _const_spec(k_dec.shape),
        _const_spec(gmix.shape), _layer_spec(w_in.shape, weight_layer),
        _layer_spec(pool_w.shape, weight_layer), _const_spec(pool_scale.shape),
        _const_spec(ret_g.shape),
    ] + [_layer_spec(w.shape, weight_layer) for w in (w_up_pool, w_up_ret, w_up_mem, w_out)]
    assert len(in_specs) == N_MIXER_INPUTS
    cast_views, cast_in_specs, cast_out_specs, cast_shapes = _cast_specs(
        cast_params, batch * steps, lambda b, s: b * steps + s, layer=layer)
    outs = pl.pallas_call(
        functools.partial(_mixer_kernel, chunk_decay, layer, n_sub, len(cast_params)),
        grid=(batch, steps),
        in_specs=in_specs + cast_in_specs,
        out_specs=[pl.BlockSpec((bm, d), tok)] + cast_out_specs,
        out_shape=[jax.ShapeDtypeStruct((t, d), F32)] + cast_shapes,
        scratch_shapes=[
            pltpu.VMEM((tm, d), BF16),
            pltpu.VMEM((POOL_HALO + tm, POOL_WIDTH), F32),
            pltpu.VMEM((RET_HEADS, RET_DK, RET_DV), F32),
            pltpu.VMEM((tm, RET_QK_WIDTH), BF16),
            pltpu.VMEM(((tm // RET_CHUNK) * HEAD_PAIRS, 2 * RET_CHUNK, 2 * RET_DK), BF16),
            pltpu.VMEM((tm, RET_QK_WIDTH), BF16),
            pltpu.VMEM((RET_HEADS, tm, RET_CHUNK + RET_DK), BF16),
            pltpu.VMEM((RET_HEADS * (tm // RET_CHUNK), RET_CHUNK + RET_DK, RET_DV), BF16),
            pltpu.VMEM((tm, RET_V_WIDTH), F32),
            pltpu.VMEM((tm, POOL_WIDTH), BF16),
            pltpu.VMEM((tm, RET_V_WIDTH), BF16),
            pltpu.VMEM((tm, XATTN_WIDTH), BF16),
            pltpu.VMEM((tm, D_MODEL), F32),
            pltpu.VMEM((tm, D_MODEL), BF16),
        ],
        compiler_params=pltpu.CompilerParams(
            dimension_semantics=("arbitrary", "arbitrary"),
            vmem_limit_bytes=VMEM_LIMIT_BYTES),
        name="mixer",
    )(x2d, cos2, sin2, kt, vm, mask, q_dec, k_dec, gmix, w_in, pool_w, pool_scale, ret_g,
      w_up_pool, w_up_ret, w_up_mem, w_out, *cast_views)
    return outs[0], [o.reshape((1,) + p.shape[1:]) for o, p in zip(outs[1:], cast_params)]


def _mlp_kernel(apply_final_norm, layer, n_cast, *refs):
    x_ref, g_ref, w1_ref, w2_ref, gf_ref = refs[:N_MLP_INPUTS]
    cast_in = refs[N_MLP_INPUTS:N_MLP_INPUTS + n_cast]
    o_ref = refs[N_MLP_INPUTS + n_cast]
    cast_out = refs[N_MLP_INPUTS + n_cast + 1:N_MLP_INPUTS + 2 * n_cast + 1]
    (hid_ref,) = refs[N_MLP_INPUTS + 2 * n_cast + 1:]
    g = g_ref[layer:layer + 1, :]
    tm = x_ref.shape[0]
    n_parts = MLP_ROW_PARTS if tm % (MLP_ROW_PARTS * BF16_SUBLANES) == 0 else 1
    part_rows = [slice(i * (tm // n_parts), (i + 1) * (tm // n_parts)) for i in range(n_parts)]
    for rows in part_rows:
        h = _rms_norm(x_ref[rows, :], g).astype(BF16)
        for j in range(D_FF // N_CHUNK):
            nc = slice(j * N_CHUNK, (j + 1) * N_CHUNK)
            a = jnp.maximum(_dot(h, w1_ref[:, nc]), 0.0)
            hid_ref[rows, nc] = (a * a).astype(BF16)
    for rows in part_rows:
        y = x_ref[rows, :] + _dot(hid_ref[rows, :], w2_ref[...])
        if apply_final_norm:
            y = _rms_norm(y, gf_ref[...])
        o_ref[rows, :] = y
    _cast_blocks(cast_in, cast_out)


def _mlp_layer(layer, x2d, g, w1, w2, g_final, apply_final_norm, cast_params=(), cast_layer=None):
    t, d = x2d.shape
    tm = min(MLP_TOKEN_TILE, t)
    steps = t // tm
    cast_views, cast_in_specs, cast_out_specs, cast_shapes = _cast_specs(
        cast_params, steps, lambda i: i, layer=cast_layer)
    outs = pl.pallas_call(
        functools.partial(_mlp_kernel, apply_final_norm, layer, len(cast_params)),
        grid=(steps,),
        in_specs=[pl.BlockSpec((tm, d), lambda i: (i, 0)),
                  _const_spec(g.shape), _layer_spec(w1.shape, 0),
                  _layer_spec(w2.shape, 0), _const_spec(g_final.shape)] + cast_in_specs,
        out_specs=[pl.BlockSpec((tm, d), lambda i: (i, 0))] + cast_out_specs,
        out_shape=[jax.ShapeDtypeStruct((t, d), F32)] + cast_shapes,
        scratch_shapes=[pltpu.VMEM((tm, D_FF), BF16)],
        compiler_params=pltpu.CompilerParams(
            dimension_semantics=("arbitrary",),
            vmem_limit_bytes=VMEM_LIMIT_BYTES),
        name="mlp",
    )(x2d, g, w1, w2, g_final, *cast_views)
    return outs[0], [o.reshape((1,) + p.shape[1:]) for o, p in zip(outs[1:], cast_params)]


def kernel(x, mem, positions, norm_mix_g, w_in, pool_w, pool_scale, ret_norm_g, mem_norm_g,
           w_mem_kv, w_up_pool, w_up_ret, w_up_mem, w_out, norm_mlp_g, w_mlp1, w_mlp2,
           final_norm_g):
    batch, seq, d = x.shape
    depth = w_in.shape[0]
    assert d == D_MODEL and w_in.shape[-1] == IN_COLS and mem.shape[1] == N_MEM
    assert seq % RET_CHUNK == 0 and seq % min(TOKEN_TILE, seq) == 0

    mixer_w_f32 = (w_in, _pool_pair_weights(pool_w), w_up_pool, w_up_ret, w_up_mem, w_out)
    cos2, sin2, kt, vm, mixer_w = _prologue(positions, min(TOKEN_TILE, seq), mem, mem_norm_g, w_mem_kv,
                                            mixer_w_f32, cast_layer=0)
    tables = _retention_tables()
    g_final = final_norm_g.reshape(1, -1)

    x2d = x.reshape(batch * seq, d)
    for l in range(depth):
        w_in_b, pool_w_b, w_up_pool_b, w_up_ret_b, w_up_mem_b, w_out_b = mixer_w
        x2d, mlp_w = _mixer_layer(
            l, x2d, batch, cos2, sin2, kt, vm, tables, norm_mix_g, w_in_b, pool_w_b, pool_scale,
            ret_norm_g, w_up_pool_b, w_up_ret_b, w_up_mem_b, w_out_b,
            cast_params=(w_mlp1, w_mlp2), weight_layer=0)
        last = l == depth - 1
        x2d, mixer_w = _mlp_layer(
            l, x2d, norm_mlp_g, mlp_w[0], mlp_w[1], g_final, apply_final_norm=last,
            cast_params=() if last else mixer_w_f32, cast_layer=None if last else l + 1)
    return x2d.reshape(batch, seq, d)
```

```python
import functools

import numpy as np
import jax
import jax.numpy as jnp
from jax import lax
from jax.experimental import pallas as pl
from jax.experimental.pallas import tpu as pltpu

D_MODEL = 1024
N_MEM = 256
POOL_WINDOWS = (2, 4, 8, 16)
POOL_GROUPS = 4
POOL_GROUP_DIM = 128
POOL_WIDTH = POOL_GROUPS * POOL_GROUP_DIM
POOL_HALO = 16
RET_HEADS = 4
RET_DK = 128
RET_DV = 256
RET_QK_WIDTH = RET_HEADS * RET_DK
RET_V_WIDTH = RET_HEADS * RET_DV
RET_CHUNK = 128
HEAD_PAIRS = RET_HEADS // 2
ROPE_BASE = 10000.0
XATTN_HEADS = 4
XATTN_DH = 128
XATTN_WIDTH = XATTN_HEADS * XATTN_DH
D_FF = 4 * D_MODEL
NORM_EPS = 1e-6

OFF_POOL = 0
OFF_RQ = OFF_POOL + POOL_WIDTH
OFF_RK = OFF_RQ + RET_QK_WIDTH
OFF_RV = OFF_RK + RET_QK_WIDTH
OFF_RG = OFF_RV + RET_V_WIDTH
OFF_XQ = OFF_RG + RET_V_WIDTH
OFF_GATE = OFF_XQ + XATTN_WIDTH
IN_COLS = OFF_GATE + 3 * D_MODEL

TOKEN_TILE = 512
MIXER_SUBTILES = 2
MIXER_NORM_PARTS = 2
MLP_TOKEN_TILE = 1024
MLP_ROW_PARTS = 2
N_CHUNK = 512
MERGE_CHUNK = 256
VMEM_LIMIT_BYTES = 60 * 1024 * 1024
BF16_SUBLANES = 16
N_MIXER_INPUTS = 17
N_MLP_INPUTS = 5

SOFTMAX_EXP2_SCALE = float(XATTN_DH ** -0.5 * np.log2(np.e))

BF16 = jnp.bfloat16
F32 = jnp.float32


def _dot(a, b):
    return jnp.dot(a, b, preferred_element_type=F32)


def _rms_norm(x, g):
    return x * lax.rsqrt(jnp.mean(x * x, axis=-1, keepdims=True) + NORM_EPS) * g


def _const_spec(shape):
    nd = len(shape)
    return pl.BlockSpec(shape, lambda *_: (0,) * nd, pipeline_mode=pl.Buffered(1))


def _layer_spec(stacked_shape, layer):
    nd = len(stacked_shape) - 1
    return pl.BlockSpec((None,) + tuple(stacked_shape[1:]), lambda *_: (layer,) + (0,) * nd,
                        pipeline_mode=pl.Buffered(1))


def _cast_specs(params, n_steps, index_of_step, layer=None):
    views, in_specs, out_specs, out_shapes = [], [], [], []
    for p in params:
        cols = p.shape[-1]
        rows_per_layer = int(np.prod(p.shape[1:-1]))
        rows = rows_per_layer if layer is not None else p.shape[0] * rows_per_layer
        slab, rem = divmod(rows, n_steps)
        assert rem == 0 and slab % BF16_SUBLANES == 0, (p.shape, n_steps)
        first = 0 if layer is None else layer * n_steps
        views.append(p.reshape(-1, cols))
        in_specs.append(pl.BlockSpec((slab, cols), lambda *g, f=first: (f + index_of_step(*g), 0)))
        out_specs.append(pl.BlockSpec((slab, cols), lambda *g: (index_of_step(*g), 0)))
        out_shapes.append(jax.ShapeDtypeStruct((rows, cols), BF16))
    return views, in_specs, out_specs, out_shapes


def _cast_blocks(in_refs, out_refs):
    for src_ref, dst_ref in zip(in_refs, out_refs):
        dst_ref[...] = src_ref[...].astype(BF16)


def _prologue_kernel(n_cast, pos_ref, freq_ref, mem_ref, g_ref, w_ref, *refs):
    cast_in = refs[:n_cast]
    cos_ref, sin_ref, kt_ref, v_ref = refs[n_cast:n_cast + 4]
    _cast_blocks(cast_in, refs[n_cast + 4:])
    mem_n = _rms_norm(mem_ref[...], g_ref[...]).astype(BF16)
    kv = _dot(mem_n, w_ref[...].astype(BF16))
    kt_ref[...] = kv[:, :XATTN_WIDTH].T.astype(BF16)
    v_ref[...] = kv[:, XATTN_WIDTH:].astype(BF16)
    half_tile = pos_ref.shape[1]
    lane = lax.broadcasted_iota(jnp.int32, (half_tile, RET_DK), 1)
    pos_t = pos_ref[...].astype(F32).T
    for j in range(pos_ref.shape[0] // 2):
        ang = jnp.where(lane < RET_DK // 2, pos_t[:, 2 * j:2 * j + 1],
                        pos_t[:, 2 * j + 1:2 * j + 2]) * freq_ref[...]
        tile_rows = slice(j * half_tile, (j + 1) * half_tile)
        cos_ref[tile_rows, :] = jnp.cos(ang)
        sin_ref[tile_rows, :] = jnp.sin(ang)


def _prologue(positions, token_tile, mem, mem_norm_g, w_mem_kv, cast_params=(), cast_layer=0):
    t = positions.size
    depth = w_mem_kv.shape[0]
    b, m, d = mem.shape
    steps = depth * b
    rows, rem = divmod(t // 2, steps)
    assert rem == 0 and rows % 8 == 0, (t, steps)
    half_tile = token_tile // 2
    half = RET_DK // 2
    inv_freq = ROPE_BASE ** (-jnp.arange(half, dtype=F32) / half)
    freq2 = jnp.concatenate([inv_freq, inv_freq]).reshape(1, RET_DK)
    assert rows % half_tile == 0
    pos_rows = 2 * (rows // half_tile)
    pos = positions.reshape(t // half_tile, half_tile)
    cast_views, cast_in_specs, cast_out_specs, cast_shapes = _cast_specs(
        cast_params, steps, lambda i: i, layer=cast_layer)
    outs = pl.pallas_call(
        functools.partial(_prologue_kernel, len(cast_params)),
        grid=(steps,),
        in_specs=[pl.BlockSpec((pos_rows, half_tile), lambda i: (i, 0)),
                  pl.BlockSpec((1, RET_DK), lambda i: (0, 0)),
                  pl.BlockSpec((None, m, d), lambda i: (i % b, 0, 0)),
                  pl.BlockSpec((None, 1, d), lambda i: (i // b, 0, 0)),
                  pl.BlockSpec((None, d, 2 * XATTN_WIDTH), lambda i: (i // b, 0, 0))] + cast_in_specs,
        out_specs=[pl.BlockSpec((rows, RET_DK), lambda i: (i, 0))] * 2
                  + [pl.BlockSpec((None, None, XATTN_WIDTH, m), lambda i: (i // b, i % b, 0, 0)),
                     pl.BlockSpec((None, None, m, XATTN_WIDTH), lambda i: (i // b, i % b, 0, 0))]
                  + cast_out_specs,
        out_shape=[jax.ShapeDtypeStruct((t // 2, RET_DK), F32)] * 2
                  + [jax.ShapeDtypeStruct((depth, b, XATTN_WIDTH, m), BF16),
                     jax.ShapeDtypeStruct((depth, b, m, XATTN_WIDTH), BF16)] + cast_shapes,
        name="prologue",
    )(pos, freq2, mem, mem_norm_g.reshape(depth, 1, d), w_mem_kv, *cast_views)
    cos2, sin2, kt, vm = outs[:4]
    return cos2, sin2, kt, vm, [o.reshape((1,) + p.shape[1:]) for o, p in zip(outs[4:], cast_params)]


def _pool_pair_weights(pool_w):
    zeros = jnp.zeros_like(pool_w[:, 0::2])
    top = jnp.concatenate([pool_w[:, 0::2], zeros], axis=-1)
    bottom = jnp.concatenate([zeros, pool_w[:, 1::2]], axis=-1)
    return jnp.concatenate([top, bottom], axis=-2)


def _retention_tables():
    c = RET_CHUNK
    log_gamma = np.log(1.0 - 2.0 ** (-5.0 - np.arange(RET_HEADS, dtype=np.float64)))
    idx = np.arange(c, dtype=np.float64)
    diff = idx[:, None] - idx[None, :]
    mask = np.where(diff[None] >= 0, np.exp(diff[None] * log_gamma[:, None, None]), 0.0)
    q_dec = np.exp((idx + 1.0)[None, :] * log_gamma[:, None])
    k_dec = np.exp((c - 1.0 - idx)[None, :] * log_gamma[:, None])
    q_dec = np.broadcast_to(q_dec[:, :, None], (RET_HEADS, c, RET_DK))
    k_dec = np.broadcast_to(k_dec[:, :, None], (RET_HEADS, c, RET_DK))
    chunk_decay = tuple(float(v) for v in np.exp(c * log_gamma))
    mask = np.concatenate([mask[0::2], mask[1::2]], axis=-1)
    as_f32 = lambda a: jnp.asarray(np.ascontiguousarray(a), dtype=F32)
    return as_f32(mask), as_f32(q_dec), as_f32(k_dec), chunk_decay


def _mixer_kernel(chunk_decay, layer, n_sub, n_cast, *refs):
    (x_ref, cos_ref, sin_ref, kt_ref, vm_ref, mask_ref, qdec_ref, kdec_ref,
     gmix_ref, w_in_ref, pool_w_ref, pool_scale_ref, ret_g_ref,
     w_up_pool_ref, w_up_ret_ref, w_up_mem_ref, w_out_ref) = refs[:N_MIXER_INPUTS]
    cast_in = refs[N_MIXER_INPUTS:N_MIXER_INPUTS + n_cast]
    o_ref = refs[N_MIXER_INPUTS + n_cast]
    cast_out = refs[N_MIXER_INPUTS + n_cast + 1:N_MIXER_INPUTS + 2 * n_cast + 1]
    (h_ref, ubuf_ref, state_ref, qb_ref, kbd_ref, kd_ref, lhs_ref, rhs_ref, gsilu_ref,
     ypool_ref, yret_ref, ymem_ref, macc_ref, merged_ref) = refs[N_MIXER_INPUTS + 2 * n_cast + 1:]
    gmix = gmix_ref[layer:layer + 1, :]
    pool_scale = pool_scale_ref[layer:layer + 1, :]
    ret_g = ret_g_ref[layer:layer + 1, :]
    tm = x_ref.shape[0] // n_sub
    n_chunks = tm // RET_CHUNK
    seq_step = pl.program_id(1)

    @pl.when(seq_step == 0)
    def _():
        ubuf_ref[0:POOL_HALO, :] = jnp.zeros((POOL_HALO, POOL_WIDTH), F32)
        state_ref[...] = jnp.zeros_like(state_ref)
        kbd_ref[...] = jnp.zeros_like(kbd_ref)

    def tile(sub):
        xr = slice(sub * tm, (sub + 1) * tm)
        tr = slice(sub * (tm // 2), (sub + 1) * (tm // 2))
        tile_start = (seq_step * n_sub + sub) * tm
        part = tm // MIXER_NORM_PARTS
        half_rows = tuple(slice(i * part, (i + 1) * part) for i in range(MIXER_NORM_PARTS))
        for hr in half_rows:
            h_ref[hr, :] = _rms_norm(x_ref[sub * tm + hr.start:sub * tm + hr.stop, :],
                                     gmix).astype(BF16)

        def proj(off, width):
            return _dot(h_ref[...], w_in_ref[:, off:off + width])

        def chunk_rows(c):
            return slice(c * RET_CHUNK, (c + 1) * RET_CHUNK)

        def merge_chunk(b, y_ref, w_up_ref, j):
            cols = slice(j * MERGE_CHUNK, (j + 1) * MERGE_CHUNK)
            g = jax.nn.sigmoid(proj(OFF_GATE + b * D_MODEL + j * MERGE_CHUNK, MERGE_CHUNK))
            return g * _dot(y_ref[...], w_up_ref[:, cols])


        lane = lax.broadcasted_iota(jnp.int32, (tm // 2, RET_DK), 1)
        first = lane < RET_DK // 2
        cos_p = cos_ref[tr, :]
        sin_p = sin_ref[tr, :]
        cos_s = pltpu.roll(cos_p, RET_DK // 2, 1)
        sin_s = pltpu.roll(sin_p, RET_DK // 2, 1)
        cos2 = jnp.concatenate([jnp.where(first, cos_p, cos_s), jnp.where(first, cos_s, cos_p)], axis=0)
        sin2 = jnp.concatenate([jnp.where(first, -sin_p, sin_s), jnp.where(first, -sin_s, sin_p)], axis=0)
        q_all = jnp.concatenate(
            [_dot(h_ref[hr, :], w_in_ref[:, OFF_RQ:OFF_RQ + RET_QK_WIDTH]) for hr in half_rows], axis=0)
        k_all = proj(OFF_RK, RET_QK_WIDTH)
        for hh in range(RET_HEADS):
            qk = slice(hh * RET_DK, (hh + 1) * RET_DK)
            qh = q_all[:, qk]
            kh = k_all[:, qk]
            qr = qh * cos2 + pltpu.roll(qh, RET_DK // 2, 1) * sin2
            kr = (kh * cos2 + pltpu.roll(kh, RET_DK // 2, 1) * sin2) * (RET_DK ** -0.5)
            qb_ref[:, qk] = qr.astype(BF16)
            pair, side = divmod(hh, 2)
            diag = slice(side * RET_DK, (side + 1) * RET_DK)
            v_h = proj(OFF_RV + hh * RET_DV, RET_DV).astype(BF16)
            for c in range(n_chunks):
                rows = chunk_rows(c)
                lhs_ref[hh, rows, RET_CHUNK:] = (qr[rows] * qdec_ref[hh]).astype(BF16)
                kd_ref[rows, qk] = (kr[rows] * kdec_ref[hh]).astype(BF16)
                kbd_ref[c * HEAD_PAIRS + pair, diag, diag] = kr[rows].T.astype(BF16)
                rhs_ref[hh * n_chunks + c, 0:RET_CHUNK, :] = v_h[rows]

        ubuf_ref[POOL_HALO:POOL_HALO + tm, :] = proj(OFF_POOL, POOL_WIDTH)
        row = lax.broadcasted_iota(jnp.int32, (tm, 1), 0) + tile_start
        dwins = []
        for gi, w in enumerate(POOL_WINDOWS):
            cols = slice(gi * POOL_GROUP_DIM, (gi + 1) * POOL_GROUP_DIM)
            win = ubuf_ref[:, cols]
            shift = 1
            while shift < w:
                win = win + pltpu.roll(win, shift, 0)
                shift *= 2
            cnt = jnp.minimum(row + 1, w).astype(F32)
            dwin = win[POOL_HALO:, :] / cnt - ubuf_ref[POOL_HALO:POOL_HALO + tm, cols]
            dwins.append(dwin.astype(BF16))
            if gi % 2 == 1:
                cols2 = slice((gi - 1) * POOL_GROUP_DIM, (gi + 1) * POOL_GROUP_DIM)
                yg = _dot(jnp.concatenate(dwins[gi - 1:gi + 1], axis=1), pool_w_ref[gi // 2])
                ypool_ref[:, cols2] = (yg * pool_scale[:, cols2]).astype(BF16)
            vv = slice(gi * RET_DV, (gi + 1) * RET_DV)
            g_h = proj(OFF_RG + gi * RET_DV, RET_DV)
            gsilu_ref[:, vv] = g_h * jax.nn.sigmoid(g_h) * ret_g[:, vv]
        ubuf_ref[0:POOL_HALO, :] = ubuf_ref[tm:tm + POOL_HALO, :]

        for pair in range(HEAD_PAIRS):
            qk2 = slice(2 * pair * RET_DK, (2 * pair + 2) * RET_DK)
            for c in range(n_chunks):
                rows = chunk_rows(c)
                scores = _dot(qb_ref[rows, qk2], kbd_ref[c * HEAD_PAIRS + pair]) * mask_ref[pair]
                scores = scores.astype(BF16)
                lhs_ref[2 * pair, rows, 0:RET_CHUNK] = scores[:, :RET_CHUNK]
                lhs_ref[2 * pair + 1, rows, 0:RET_CHUNK] = scores[:, RET_CHUNK:]

        kvs = [[lax.dot_general(kd_ref[chunk_rows(c), hh * RET_DK:(hh + 1) * RET_DK],
                                rhs_ref[hh * n_chunks + c, 0:RET_CHUNK, :],
                                (((0,), (0,)), ((), ())), preferred_element_type=F32)
                for c in range(n_chunks)] for hh in range(RET_HEADS)]
        for hh in range(RET_HEADS):
            s = state_ref[hh]
            for c in range(n_chunks):
                rhs_ref[hh * n_chunks + c, RET_CHUNK:, :] = s.astype(BF16)
                s = chunk_decay[hh] * s + kvs[hh][c]
            state_ref[hh] = s

        xq = proj(OFF_XQ, XATTN_WIDTH).astype(BF16)

        for hh in range(RET_HEADS):
            vv = slice(hh * RET_DV, (hh + 1) * RET_DV)
            for c in range(n_chunks):
                rows = chunk_rows(c)
                y = _dot(lhs_ref[hh, rows, :], rhs_ref[hh * n_chunks + c])
                y = y * lax.rsqrt(jnp.mean(y * y, axis=-1, keepdims=True) + NORM_EPS)
                yret_ref[rows, vv] = (gsilu_ref[rows, vv] * y).astype(BF16)
            mc = slice(hh * MERGE_CHUNK, (hh + 1) * MERGE_CHUNK)
            macc_ref[:, mc] = merge_chunk(0, ypool_ref, w_up_pool_ref, hh)

        logits_ref, probs_ref = gsilu_ref, lhs_ref
        assert logits_ref.shape == (tm, XATTN_HEADS * N_MEM) and probs_ref.shape == (XATTN_HEADS, tm, N_MEM)
        for hh in range(XATTN_HEADS):
            hd = slice(hh * XATTN_DH, (hh + 1) * XATTN_DH)
            logits_ref[:, hh * N_MEM:(hh + 1) * N_MEM] = _dot(xq[:, hd], kt_ref[hd, :])
        for hh in range(XATTN_HEADS):
            s = logits_ref[:, hh * N_MEM:(hh + 1) * N_MEM]
            p = jnp.exp2((s - jnp.max(s, axis=-1, keepdims=True)) * SOFTMAX_EXP2_SCALE)
            probs_ref[hh] = (p / jnp.sum(p, axis=-1, keepdims=True)).astype(BF16)
            mc = slice(hh * MERGE_CHUNK, (hh + 1) * MERGE_CHUNK)
            macc_ref[:, mc] = macc_ref[:, mc] + merge_chunk(1, yret_ref, w_up_ret_ref, hh)
        for hh in range(XATTN_HEADS):
            hd = slice(hh * XATTN_DH, (hh + 1) * XATTN_DH)
            ymem_ref[:, hd] = _dot(probs_ref[hh], vm_ref[:, hd]).astype(BF16)

        for j in range(D_MODEL // MERGE_CHUNK):
            mc = slice(j * MERGE_CHUNK, (j + 1) * MERGE_CHUNK)
            merged_ref[:, mc] = (macc_ref[:, mc]
                                 + merge_chunk(2, ymem_ref, w_up_mem_ref, j)).astype(BF16)
        o_ref[xr, :] = x_ref[xr, :] + _dot(merged_ref[...], w_out_ref[...])

    for sub in range(n_sub):
        tile(sub)
    _cast_blocks(cast_in, cast_out)


def _mixer_layer(layer, x2d, batch, cos2, sin2, kt, vm, tables, gmix, w_in, pool_w, pool_scale,
                 ret_g, w_up_pool, w_up_ret, w_up_mem, w_out, cast_params=(), weight_layer=0):
    t, d = x2d.shape
    seq = t // batch
    tm = min(TOKEN_TILE, seq)
    n_sub = MIXER_SUBTILES if seq % (MIXER_SUBTILES * tm) == 0 else 1
    bm = n_sub * tm
    steps = seq // bm
    mask, q_dec, k_dec, chunk_decay = tables
    tok = lambda b, s: (b * steps + s, 0)
    in_specs = [
        pl.BlockSpec((bm, d), tok),
        pl.BlockSpec((bm // 2, RET_DK), tok),
        pl.BlockSpec((bm // 2, RET_DK), tok),
        pl.BlockSpec((None, None, XATTN_WIDTH, N_MEM), lambda b, s: (layer, b, 0, 0)),
        pl.BlockSpec((None, None, N_MEM, XATTN_WIDTH), lambda b, s: (layer, b, 0, 0)),
        _const_spec(mask.shape), _const_spec(q_dec.shape), _const_spec(k_dec.shape),
        _const_spec(gmix.shape), _layer_spec(w_in.shape, weight_layer),
        _layer_spec(pool_w.shape, weight_layer), _const_spec(pool_scale.shape),
        _const_spec(ret_g.shape),
    ] + [_layer_spec(w.shape, weight_layer) for w in (w_up_pool, w_up_ret, w_up_mem, w_out)]
    assert len(in_specs) == N_MIXER_INPUTS
    cast_views, cast_in_specs, cast_out_specs, cast_shapes = _cast_specs(
        cast_params, batch * steps, lambda b, s: b * steps + s, layer=layer)
    outs = pl.pallas_call(
        functools.partial(_mixer_kernel, chunk_decay, layer, n_sub, len(cast_params)),
        grid=(batch, steps),
        in_specs=in_specs + cast_in_specs,
        out_specs=[pl.BlockSpec((bm, d), tok)] + cast_out_specs,
        out_shape=[jax.ShapeDtypeStruct((t, d), F32)] + cast_shapes,
        scratch_shapes=[
            pltpu.VMEM((tm, d), BF16),
            pltpu.VMEM((POOL_HALO + tm, POOL_WIDTH), F32),
            pltpu.VMEM((RET_HEADS, RET_DK, RET_DV), F32),
            pltpu.VMEM((tm, RET_QK_WIDTH), BF16),
            pltpu.VMEM(((tm // RET_CHUNK) * HEAD_PAIRS, 2 * RET_CHUNK, 2 * RET_DK), BF16),
            pltpu.VMEM((tm, RET_QK_WIDTH), BF16),
            pltpu.VMEM((RET_HEADS, tm, RET_CHUNK + RET_DK), BF16),
            pltpu.VMEM((RET_HEADS * (tm // RET_CHUNK), RET_CHUNK + RET_DK, RET_DV), BF16),
            pltpu.VMEM((tm, RET_V_WIDTH), F32),
            pltpu.VMEM((tm, POOL_WIDTH), BF16),
            pltpu.VMEM((tm, RET_V_WIDTH), BF16),
            pltpu.VMEM((tm, XATTN_WIDTH), BF16),
            pltpu.VMEM((tm, D_MODEL), F32),
            pltpu.VMEM((tm, D_MODEL), BF16),
        ],
        compiler_params=pltpu.CompilerParams(
            dimension_semantics=("arbitrary", "arbitrary"),
            vmem_limit_bytes=VMEM_LIMIT_BYTES),
        name="mixer",
    )(x2d, cos2, sin2, kt, vm, mask, q_dec, k_dec, gmix, w_in, pool_w, pool_scale, ret_g,
      w_up_pool, w_up_ret, w_up_mem, w_out, *cast_views)
    return outs[0], [o.reshape((1,) + p.shape[1:]) for o, p in zip(outs[1:], cast_params)]


def _mlp_kernel(apply_final_norm, layer, n_cast, *refs):
    x_ref, g_ref, w1_ref, w2_ref, gf_ref = refs[:N_MLP_INPUTS]
    cast_in = refs[N_MLP_INPUTS:N_MLP_INPUTS + n_cast]
    o_ref = refs[N_MLP_INPUTS + n_cast]
    cast_out = refs[N_MLP_INPUTS + n_cast + 1:N_MLP_INPUTS + 2 * n_cast + 1]
    (hid_ref,) = refs[N_MLP_INPUTS + 2 * n_cast + 1:]
    g = g_ref[layer:layer + 1, :]
    tm = x_ref.shape[0]
    n_parts = MLP_ROW_PARTS if tm % (MLP_ROW_PARTS * BF16_SUBLANES) == 0 else 1
    part_rows = [slice(i * (tm // n_parts), (i + 1) * (tm // n_parts)) for i in range(n_parts)]
    for rows in part_rows:
        h = _rms_norm(x_ref[rows, :], g).astype(BF16)
        for j in range(D_FF // N_CHUNK):
            nc = slice(j * N_CHUNK, (j + 1) * N_CHUNK)
            a = jnp.maximum(_dot(h, w1_ref[:, nc]), 0.0)
            hid_ref[rows, nc] = (a * a).astype(BF16)
    for rows in part_rows:
        y = x_ref[rows, :] + _dot(hid_ref[rows, :], w2_ref[...])
        if apply_final_norm:
            y = _rms_norm(y, gf_ref[...])
        o_ref[rows, :] = y
    _cast_blocks(cast_in, cast_out)


def _mlp_layer(layer, x2d, g, w1, w2, g_final, apply_final_norm, cast_params=(), cast_layer=None):
    t, d = x2d.shape
    tm = min(MLP_TOKEN_TILE, t)
    steps = t // tm
    cast_views, cast_in_specs, cast_out_specs, cast_shapes = _cast_specs(
        cast_params, steps, lambda i: i, layer=cast_layer)
    outs = pl.pallas_call(
        functools.partial(_mlp_kernel, apply_final_norm, layer, len(cast_params)),
        grid=(steps,),
        in_specs=[pl.BlockSpec((tm, d), lambda i: (i, 0)),
                  _const_spec(g.shape), _layer_spec(w1.shape, 0),
                  _layer_spec(w2.shape, 0), _const_spec(g_final.shape)] + cast_in_specs,
        out_specs=[pl.BlockSpec((tm, d), lambda i: (i, 0))] + cast_out_specs,
        out_shape=[jax.ShapeDtypeStruct((t, d), F32)] + cast_shapes,
        scratch_shapes=[pltpu.VMEM((tm, D_FF), BF16)],
        compiler_params=pltpu.CompilerParams(
            dimension_semantics=("arbitrary",),
            vmem_limit_bytes=VMEM_LIMIT_BYTES),
        name="mlp",
    )(x2d, g, w1, w2, g_final, *cast_views)
    return outs[0], [o.reshape((1,) + p.shape[1:]) for o, p in zip(outs[1:], cast_params)]


def kernel(x, mem, positions, norm_mix_g, w_in, pool_w, pool_scale, ret_norm_g, mem_norm_g,
           w_mem_kv, w_up_pool, w_up_ret, w_up_mem, w_out, norm_mlp_g, w_mlp1, w_mlp2,
           final_norm_g):
    batch, seq, d = x.shape
    depth = w_in.shape[0]
    assert d == D_MODEL and w_in.shape[-1] == IN_COLS and mem.shape[1] == N_MEM
    assert seq % RET_CHUNK == 0 and seq % min(TOKEN_TILE, seq) == 0

    mixer_w_f32 = (w_in, _pool_pair_weights(pool_w), w_up_pool, w_up_ret, w_up_mem, w_out)
    cos2, sin2, kt, vm, mixer_w = _prologue(positions, min(TOKEN_TILE, seq), mem, mem_norm_g, w_mem_kv,
                                            mixer_w_f32, cast_layer=0)
    tables = _retention_tables()
    g_final = final_norm_g.reshape(1, -1)

    x2d = x.reshape(batch * seq, d)
    for l in range(depth):
        w_in_b, pool_w_b, w_up_pool_b, w_up_ret_b, w_up_mem_b, w_out_b = mixer_w
        x2d, mlp_w = _mixer_layer(
            l, x2d, batch, cos2, sin2, kt, vm, tables, norm_mix_g, w_in_b, pool_w_b, pool_scale,
            ret_norm_g, w_up_pool_b, w_up_ret_b, w_up_mem_b, w_out_b,
            cast_params=(w_mlp1, w_mlp2), weight_layer=0)
        last = l == depth - 1
        x2d, mixer_w = _mlp_layer(
            l, x2d, norm_mlp_g, mlp_w[0], mlp_w[1], g_final, apply_final_norm=last,
            cast_params=() if last else mixer_w_f32, cast_layer=None if last else l + 1)
    return x2d.reshape(batch, seq, d)
```
